```python
import jax, jax.numpy as jnp
from jax import lax
import numpy as np

D_MODEL = 1024
BATCH = 16
SEQ = 2048
DEPTH = 2

CHUNK = 64
MEM_LEN = 256
HEAD_GROUP = 64
A_WIDTH = 384
B_WIDTH = 320
C_WIDTH = 320
MIX_WIDTH = A_WIDTH + B_WIDTH + C_WIDTH
N_MIX_GROUPS = MIX_WIDTH // HEAD_GROUP
C_GROUPS = C_WIDTH // HEAD_GROUP
A_KERNEL = 31
B_KERNEL = 3
GMLP_BLOCK = 128
PROJ_WIDTH = 2 * A_WIDTH + 3 * B_WIDTH + 2 * C_WIDTH
X_HEADS = 4
X_HEAD_DIM = D_MODEL // X_HEADS
D_FF = 2816
N_EXPERTS = 8
TOP_K = 2
D_EXPERT = 3584
N_DENSE = (DEPTH + 1) // 2
N_MOE = DEPTH // 2
EPS = 1e-6

kernel_name = "hybrid_conv_gmlp_moe_encoder"


def rmsnorm(x, g):
    xf = x.astype(jnp.float32)
    y = xf * lax.rsqrt(jnp.mean(xf * xf, axis=-1, keepdims=True) + EPS)
    return (y * g.astype(jnp.float32)).astype(x.dtype)


def layernorm(x, g, b):
    xf = x.astype(jnp.float32)
    mu = jnp.mean(xf, axis=-1, keepdims=True)
    xc = xf - mu
    var = jnp.mean(xc * xc, axis=-1, keepdims=True)
    y = xc * lax.rsqrt(var + EPS) * g.astype(jnp.float32) + b.astype(jnp.float32)
    return y.astype(x.dtype)


def causal_depthwise_conv(x, w):
    k, c = w.shape
    return lax.conv_general_dilated(
        x, w[:, None, :], window_strides=(1,), padding=[(k - 1, 0)],
        dimension_numbers=("NWC", "WIO", "NWC"), feature_group_count=c)


def chunk_spatial_gate(u, v, ws, bias):
    bsz, s, c = v.shape
    n_blocks = s // GMLP_BLOCK
    chunk_id = jnp.arange(GMLP_BLOCK) // CHUNK
    mask = chunk_id[:, None] >= chunk_id[None, :]
    ws = jnp.where(mask[None], ws, jnp.zeros((), ws.dtype))
    vb = v.reshape(bsz, n_blocks, GMLP_BLOCK, C_GROUPS, HEAD_GROUP)
    mixed = jnp.einsum("gij,bnjgc->bnigc", ws, vb) + bias.T[None, None, :, :, None]
    return u * mixed.reshape(bsz, s, c)


def hybrid_mixer(xn, w_in, conv_a_w, conv_a_b, ln_a_g, ln_a_b, conv_b_w,
                 ln_c_g, ln_c_b, gmlp_ws, gmlp_b, mix_out_g, w_mix_out):
    z = xn @ w_in
    splits = [A_WIDTH, 2 * A_WIDTH, 2 * A_WIDTH + B_WIDTH,
              2 * A_WIDTH + 2 * B_WIDTH, 2 * A_WIDTH + 3 * B_WIDTH]
    a_val, a_gate, b_gate, c_gate, b_in, c_z = jnp.split(z, splits, axis=-1)
    ya = a_val * jax.nn.sigmoid(a_gate)
    ya = causal_depthwise_conv(ya, conv_a_w) + conv_a_b
    ya = jax.nn.silu(layernorm(ya, ln_a_g, ln_a_b))
    yb = b_gate * causal_depthwise_conv(c_gate * b_in, conv_b_w)
    c_z = jax.nn.gelu(c_z)
    c_u, c_v = jnp.split(c_z, 2, axis=-1)
    c_v = layernorm(c_v, ln_c_g, ln_c_b)
    yc = chunk_spatial_gate(c_u, c_v, gmlp_ws, gmlp_b)
    y = jnp.concatenate([ya, yb, yc], axis=-1)
    bsz, s, _ = y.shape
    y = rmsnorm(y.reshape(bsz, s, N_MIX_GROUPS, HEAD_GROUP),
                mix_out_g.reshape(N_MIX_GROUPS, HEAD_GROUP)).reshape(bsz, s, MIX_WIDTH)
    return y @ w_mix_out


def memory_cross_attention(xn, memn, w_xq, w_xkv, w_xo):
    bsz, s, _ = xn.shape
    m = memn.shape[1]
    q = (xn @ w_xq).reshape(bsz, s, X_HEADS, X_HEAD_DIM)
    k, v = jnp.split(memn @ w_xkv, 2, axis=-1)
    k = k.reshape(bsz, m, X_HEADS, X_HEAD_DIM)
    v = v.reshape(bsz, m, X_HEADS, X_HEAD_DIM)
    scores = jnp.einsum("bshd,bmhd->bhsm", q, k).astype(jnp.float32) * (X_HEAD_DIM ** -0.5)
    p = jax.nn.softmax(scores, axis=-1).astype(v.dtype)
    o = jnp.einsum("bhsm,bmhd->bshd", p, v).reshape(bsz, s, D_MODEL)
    return o @ w_xo


def swiglu(x, w_gate, w_up, w_down):
    return (jax.nn.silu(x @ w_gate) * (x @ w_up)) @ w_down


def moe_swiglu(xn, w_router, w_gate, w_up, w_down):
    bsz, s, d = xn.shape
    xt = xn.reshape(-1, d)
    logits = (xt @ w_router).astype(jnp.float32)
    top_vals, top_idx = lax.top_k(logits, TOP_K)
    top_w = jax.nn.softmax(top_vals, axis=-1)
    combine = jnp.sum(jax.nn.one_hot(top_idx, N_EXPERTS, dtype=jnp.float32) * top_w[..., None], axis=1)
    combine = combine.astype(xn.dtype)
    y = jnp.zeros_like(xt)
    for e in range(N_EXPERTS):
        y = y + combine[:, e:e + 1] * swiglu(xt, w_gate[e], w_up[e], w_down[e])
    return y.reshape(bsz, s, d)


def setup_inputs(seed: int = 0) -> dict:
    key = jax.random.key(seed)
    ks = jax.random.split(key, 32)
    f32 = jnp.float32

    def nrm(k, shape, scale):
        return jax.random.normal(k, shape, f32) * scale

    def gain(k, shape):
        return 1.0 + 0.02 * jax.random.normal(k, shape, f32)

    return {
        "x": nrm(ks[0], (BATCH, SEQ, D_MODEL), 1.0),
        "mem": nrm(ks[1], (BATCH, MEM_LEN, D_MODEL), 1.0),
        "norm_mix_g": gain(ks[2], (DEPTH, D_MODEL)),
        "w_in": nrm(ks[3], (DEPTH, D_MODEL, PROJ_WIDTH), D_MODEL ** -0.5),
        "conv_a_w": nrm(ks[4], (DEPTH, A_KERNEL, A_WIDTH), A_KERNEL ** -0.5),
        "conv_a_b": nrm(ks[5], (DEPTH, A_WIDTH), 0.02),
        "ln_a_g": gain(ks[6], (DEPTH, A_WIDTH)),
        "ln_a_b": nrm(ks[7], (DEPTH, A_WIDTH), 0.02),
        "conv_b_w": nrm(ks[8], (DEPTH, B_KERNEL, B_WIDTH), B_KERNEL ** -0.5),
        "ln_c_g": gain(ks[9], (DEPTH, C_WIDTH)),
        "ln_c_b": nrm(ks[10], (DEPTH, C_WIDTH), 0.02),
        "gmlp_ws": nrm(ks[11], (DEPTH, C_GROUPS, GMLP_BLOCK, GMLP_BLOCK), GMLP_BLOCK ** -0.5),
        "gmlp_b": 1.0 + nrm(ks[12], (DEPTH, C_GROUPS, GMLP_BLOCK), 0.1),
        "mix_out_g": gain(ks[13], (DEPTH, MIX_WIDTH)),
        "w_mix_out": nrm(ks[14], (DEPTH, MIX_WIDTH, D_MODEL), MIX_WIDTH ** -0.5),
        "norm_x_g": gain(ks[15], (DEPTH, D_MODEL)),
        "norm_mem_g": gain(ks[16], (DEPTH, D_MODEL)),
        "w_xq": nrm(ks[17], (DEPTH, D_MODEL, D_MODEL), D_MODEL ** -0.5),
        "w_xkv": nrm(ks[18], (DEPTH, D_MODEL, 2 * D_MODEL), D_MODEL ** -0.5),
        "w_xo": nrm(ks[19], (DEPTH, D_MODEL, D_MODEL), D_MODEL ** -0.5),
        "norm_ffn_g": gain(ks[20], (DEPTH, D_MODEL)),
        "ffn_w_gate": nrm(ks[21], (N_DENSE, D_MODEL, D_FF), D_MODEL ** -0.5),
        "ffn_w_up": nrm(ks[22], (N_DENSE, D_MODEL, D_FF), D_MODEL ** -0.5),
        "ffn_w_down": nrm(ks[23], (N_DENSE, D_FF, D_MODEL), D_FF ** -0.5),
        "moe_router": nrm(ks[24], (N_MOE, D_MODEL, N_EXPERTS), D_MODEL ** -0.5),
        "moe_w_gate": nrm(ks[25], (N_MOE, N_EXPERTS, D_MODEL, D_EXPERT), D_MODEL ** -0.5),
        "moe_w_up": nrm(ks[26], (N_MOE, N_EXPERTS, D_MODEL, D_EXPERT), D_MODEL ** -0.5),
        "moe_w_down": nrm(ks[27], (N_MOE, N_EXPERTS, D_EXPERT, D_MODEL), D_EXPERT ** -0.5),
        "norm_final_g": gain(ks[28], (D_MODEL,)),
    }


def reference(x, mem, norm_mix_g, w_in, conv_a_w, conv_a_b, ln_a_g, ln_a_b, conv_b_w,
              ln_c_g, ln_c_b, gmlp_ws, gmlp_b, mix_out_g, w_mix_out, norm_x_g, norm_mem_g,
              w_xq, w_xkv, w_xo, norm_ffn_g, ffn_w_gate, ffn_w_up, ffn_w_down,
              moe_router, moe_w_gate, moe_w_up, moe_w_down, norm_final_g):
    h = x
    for layer in range(DEPTH):
        h = h + hybrid_mixer(rmsnorm(h, norm_mix_g[layer]), w_in[layer], conv_a_w[layer],
                             conv_a_b[layer], ln_a_g[layer], ln_a_b[layer], conv_b_w[layer],
                             ln_c_g[layer], ln_c_b[layer], gmlp_ws[layer], gmlp_b[layer],
                             mix_out_g[layer], w_mix_out[layer])
        h = h + memory_cross_attention(rmsnorm(h, norm_x_g[layer]), rmsnorm(mem, norm_mem_g[layer]),
                                       w_xq[layer], w_xkv[layer], w_xo[layer])
        hn = rmsnorm(h, norm_ffn_g[layer])
        i = layer // 2
        if layer % 2 == 0:
            h = h + swiglu(hn, ffn_w_gate[i], ffn_w_up[i], ffn_w_down[i])
        else:
            h = h + moe_swiglu(hn, moe_router[i], moe_w_gate[i], moe_w_up[i], moe_w_down[i])
    return rmsnorm(h, norm_final_g)
```

```python
import functools

import jax
import jax.numpy as jnp
from jax import lax
from jax.experimental import pallas as pl
from jax.experimental.pallas import tpu as pltpu

F32 = jnp.float32
BF16 = jnp.bfloat16

D_MODEL = 1024
EPS = 1e-6
CHUNK = 64
HEAD_GROUP = 64
A_WIDTH, B_WIDTH, C_WIDTH = 384, 320, 320
A_KERNEL, B_KERNEL = 31, 3
GMLP_BLOCK = 128
C_GROUPS = 5
X_HEADS = 4
X_HEAD_DIM = D_MODEL // X_HEADS
N_EXPERTS = 8

LANES = 128
SUBLANES = 8
MXU_DIM = 256
VMEM_LIMIT_BYTES = 56 * 1024 * 1024

SEG = 384
N_IN_SEG = 7
MIX_PAD = 3 * SEG
HALO_A = 32
HALO_B = SUBLANES
CONV_ROWS = 32
SHIFT_EXTRA = HALO_A - SUBLANES

TILE_MIX = 512
TILE_ATT = 512
TILE_FFN = 512
TILE_ROUTE = 512
TILE_COMBINE = 256
TILE_EXPERT = 512
FF_CHUNK = 512
KV_ROWS = 1024


def _rmsnorm(x, g):
    ms = jnp.mean(x * x, axis=-1, keepdims=True)
    return x * lax.rsqrt(ms + EPS) * g


def _const_spec(shape):
    zeros = (0,) * len(shape)
    return pl.BlockSpec(shape, lambda *_: zeros, pipeline_mode=pl.Buffered(1))


def _params(*semantics):
    return pltpu.CompilerParams(dimension_semantics=semantics,
                                vmem_limit_bytes=VMEM_LIMIT_BYTES)


def _mixer_kernel(h_ref, ng_ref, win_ref, caw_ref, cab_ref, lag_ref, lab_ref, cbw_ref,
                  lcg_ref, lcb_ref, ws_ref, gbias_ref, og_ref, wout_ref, gsum_ref,
                  o_ref, zbuf, abuf, bbuf, shifted, ybuf, *, ts, tiles_per_seq):
    seq_tile = lax.rem(pl.program_id(0), tiles_per_seq)

    @pl.when(seq_tile == 0)
    def _():
        abuf[0:HALO_A, :] = jnp.zeros((HALO_A, SEG), F32)
        bbuf[0:HALO_B, :] = jnp.zeros((HALO_B, SEG), F32)

    @pl.when(seq_tile > 0)
    def _():
        abuf[0:HALO_A, :] = abuf[ts:ts + HALO_A, :]
        bbuf[0:HALO_B, :] = bbuf[ts:ts + HALO_B, :]

    h = h_ref[...]
    xn = _rmsnorm(h, ng_ref[...]).astype(BF16)
    zbuf[...] = jnp.dot(xn, win_ref[...], preferred_element_type=F32)

    lane_seg = lax.broadcasted_iota(jnp.int32, (1, SEG), 1)
    c_valid = lane_seg < C_WIDTH
    lane_blk = lax.broadcasted_iota(jnp.int32, (GMLP_BLOCK, LANES), 1)
    low_group = lane_blk < HEAD_GROUP
    row_chunk = lax.broadcasted_iota(jnp.int32, (GMLP_BLOCK, GMLP_BLOCK), 0) // CHUNK
    col_chunk = lax.broadcasted_iota(jnp.int32, (GMLP_BLOCK, GMLP_BLOCK), 1) // CHUNK
    chunk_causal = row_chunk >= col_chunk

    def block_body(i, carry):
        r = pl.multiple_of(i * GMLP_BLOCK, GMLP_BLOCK)
        rows = pl.ds(r, GMLP_BLOCK)
        a_val = zbuf[rows, 0 * SEG:1 * SEG]
        a_gate = zbuf[rows, 1 * SEG:2 * SEG]
        abuf[pl.ds(pl.multiple_of(HALO_A + r, SUBLANES), GMLP_BLOCK), :] = (
            a_val * jax.nn.sigmoid(a_gate))
        c_gate = zbuf[rows, 3 * SEG:4 * SEG]
        b_in = zbuf[rows, 4 * SEG:5 * SEG]
        bbuf[pl.ds(pl.multiple_of(HALO_B + r, SUBLANES), GMLP_BLOCK), :] = c_gate * b_in

        c_u = jax.nn.gelu(zbuf[rows, 5 * SEG:6 * SEG])
        c_v = jax.nn.gelu(zbuf[rows, 6 * SEG:7 * SEG])
        mu = jnp.sum(c_v, axis=-1, keepdims=True) * (1.0 / C_WIDTH)
        xc = jnp.where(c_valid, c_v - mu, 0.0)
        var = jnp.sum(xc * xc, axis=-1, keepdims=True) * (1.0 / C_WIDTH)
        v = (xc * lax.rsqrt(var + EPS) * lcg_ref[...] + lcb_ref[...]).astype(BF16)
        cols = []
        for j in range(SEG // LANES):
            vj = v[:, j * LANES:(j + 1) * LANES]
            w_lo = jnp.where(chunk_causal, ws_ref[2 * j], jnp.zeros((), BF16))
            col = jnp.dot(w_lo, vj, preferred_element_type=F32)
            if 2 * j + 1 < C_GROUPS:
                w_hi = jnp.where(chunk_causal, ws_ref[2 * j + 1], jnp.zeros((), BF16))
                col = jnp.where(low_group, col, jnp.dot(w_hi, vj, preferred_element_type=F32))
            cols.append(col)
        mixed = jnp.concatenate(cols, axis=1) + gbias_ref[...]
        ybuf[rows, 2 * SEG:3 * SEG] = c_u * mixed
        return carry

    lax.fori_loop(0, ts // GMLP_BLOCK, block_body, 0)

    for phase in range(1, SUBLANES):
        shifted[phase - 1] = abuf[phase:phase + ts + SHIFT_EXTRA, :]

    def conv_body(j, carry):
        r = pl.multiple_of(j * CONV_ROWS, CONV_ROWS)
        rows = pl.ds(r, CONV_ROWS)
        acc = jnp.concatenate([cab_ref[...]] * (CONV_ROWS // SUBLANES), axis=0)
        for k in range(A_KERNEL):
            groups, phase = divmod(HALO_A - A_KERNEL + 1 + k, SUBLANES)
            start = pl.multiple_of(r + groups * SUBLANES, SUBLANES)
            if phase == 0:
                window = abuf[pl.ds(start, CONV_ROWS), :]
            else:
                window = shifted[phase - 1, pl.ds(start, CONV_ROWS), :]
            tap = caw_ref[k * SUBLANES:(k + 1) * SUBLANES, :]
            acc = acc + jnp.concatenate([tap] * (CONV_ROWS // SUBLANES), axis=0) * window
        mu = jnp.mean(acc, axis=-1, keepdims=True)
        xc = acc - mu
        var = jnp.mean(xc * xc, axis=-1, keepdims=True)
        ya = jax.nn.silu(xc * lax.rsqrt(var + EPS) * lag_ref[...] + lab_ref[...])
        ybuf[rows, 0 * SEG:1 * SEG] = ya
        return carry

    lax.fori_loop(0, ts // CONV_ROWS, conv_body, 0)

    accb = jnp.zeros((ts, SEG), F32)
    for k in range(B_KERNEL):
        first = HALO_B - B_KERNEL + 1 + k
        accb = accb + cbw_ref[k:k + 1, :] * bbuf[first:first + ts, :]
    ybuf[:, 1 * SEG:2 * SEG] = zbuf[:, 2 * SEG:3 * SEG] * accb

    y = ybuf[...]
    y2 = (y * y).astype(BF16)
    gsum = gsum_ref[...]
    sums = []
    for c0 in range(0, MIX_PAD, MXU_DIM):
        width = min(MXU_DIM, MIX_PAD - c0)
        sums.append(jnp.dot(y2[:, c0:c0 + width], gsum[0:width, 0:width],
                            preferred_element_type=F32))
    ms = jnp.concatenate(sums, axis=1) * (1.0 / HEAD_GROUP)
    yn = (y * lax.rsqrt(ms + EPS) * og_ref[...]).astype(BF16)
    o_ref[...] = h + jnp.dot(yn, wout_ref[...], preferred_element_type=F32)


def _mixer(h, p, seq_len):
    n_tok = h.shape[0]
    ts = TILE_MIX
    tiles_per_seq = seq_len // ts
    kern = functools.partial(_mixer_kernel, ts=ts, tiles_per_seq=tiles_per_seq)
    consts = [p["norm_g"], p["w_in"], p["conv_a_w"], p["conv_a_b"], p["ln_a_g"], p["ln_a_b"],
              p["conv_b_w"], p["ln_c_g"], p["ln_c_b"], p["gmlp_ws"], p["gmlp_bias"],
              p["out_g"], p["w_out"], p["gsum"]]
    return pl.pallas_call(
        kern,
        grid=(n_tok // ts,),
        in_specs=[pl.BlockSpec((ts, D_MODEL), lambda i: (i, 0))]
                 + [_const_spec(c.shape) for c in consts],
        out_specs=pl.BlockSpec((ts, D_MODEL), lambda i: (i, 0)),
        out_shape=jax.ShapeDtypeStruct((n_tok, D_MODEL), F32),
        scratch_shapes=[
            pltpu.VMEM((ts, N_IN_SEG * SEG), F32),
            pltpu.VMEM((HALO_A + ts, SEG), F32),
            pltpu.VMEM((HALO_B + ts, SEG), F32),
            pltpu.VMEM((SUBLANES - 1, ts + SHIFT_EXTRA, SEG), F32),
            pltpu.VMEM((ts, MIX_PAD), F32),
        ],
        compiler_params=_params("arbitrary"),
        name="mixer",
    )(h, *consts)


def _kv_kernel(mem_ref, g_ref, wkv_ref, k_ref, v_ref):
    mn = _rmsnorm(mem_ref[...], g_ref[...]).astype(BF16)
    kv = jnp.dot(mn, wkv_ref[...], preferred_element_type=F32)
    k_ref[...] = kv[:, 0:D_MODEL].astype(BF16)
    v_ref[...] = kv[:, D_MODEL:2 * D_MODEL].astype(BF16)


def _kv_proj(mem2d, g, wkv):
    n = mem2d.shape[0]
    rows = min(KV_ROWS, n)
    return pl.pallas_call(
        _kv_kernel,
        grid=(n // rows,),
        in_specs=[pl.BlockSpec((rows, D_MODEL), lambda i: (i, 0)),
                  _const_spec(g.shape), _const_spec(wkv.shape)],
        out_specs=[pl.BlockSpec((rows, D_MODEL), lambda i: (i, 0))] * 2,
        out_shape=[jax.ShapeDtypeStruct((n, D_MODEL), BF16)] * 2,
        compiler_params=_params("arbitrary"),
        name="kv_proj",
    )(mem2d, g, wkv)


def _xattn_kernel(h_ref, g_ref, wq_ref, k_ref, v_ref, wo_ref, o_ref):
    h = h_ref[...]
    xn = _rmsnorm(h, g_ref[...]).astype(BF16)
    q = (jnp.dot(xn, wq_ref[...], preferred_element_type=F32) * (X_HEAD_DIM ** -0.5)).astype(BF16)
    heads = []
    for hd in range(X_HEADS):
        cols = slice(hd * X_HEAD_DIM, (hd + 1) * X_HEAD_DIM)
        s = lax.dot_general(q[:, cols], k_ref[:, cols], (((1,), (1,)), ((), ())),
                            preferred_element_type=F32)
        e = jnp.exp(s - jnp.max(s, axis=-1, keepdims=True))
        pv = jnp.dot(e.astype(BF16), v_ref[:, cols], preferred_element_type=F32)
        heads.append(pv / jnp.sum(e, axis=-1, keepdims=True))
    o = jnp.concatenate(heads, axis=1).astype(BF16)
    o_ref[...] = h + jnp.dot(o, wo_ref[...], preferred_element_type=F32)


def _xattn(h, k, v, g, wq, wo, seq_len, mem_len):
    n_tok = h.shape[0]
    ts = TILE_ATT
    tiles_per_seq = seq_len // ts
    return pl.pallas_call(
        _xattn_kernel,
        grid=(n_tok // ts,),
        in_specs=[pl.BlockSpec((ts, D_MODEL), lambda i: (i, 0)),
                  _const_spec(g.shape), _const_spec(wq.shape),
                  pl.BlockSpec((mem_len, D_MODEL), lambda i: (i // tiles_per_seq, 0)),
                  pl.BlockSpec((mem_len, D_MODEL), lambda i: (i // tiles_per_seq, 0)),
                  _const_spec(wo.shape)],
        out_specs=pl.BlockSpec((ts, D_MODEL), lambda i: (i, 0)),
        out_shape=jax.ShapeDtypeStruct((n_tok, D_MODEL), F32),
        compiler_params=_params("arbitrary"),
        name="xattn",
    )(h, g, wq, k, v, wo)


def _ffn_chunks(d_ff):
    step = -(-d_ff // (3 * MXU_DIM)) * MXU_DIM
    return [(c0, min(c0 + step, d_ff)) for c0 in range(0, d_ff, step)]


def _ffn_kernel(h_ref, g_ref, wg_ref, wu_ref, wd_ref, o_ref, *, chunks):
    h = h_ref[...]
    xn = _rmsnorm(h, g_ref[...]).astype(BF16)
    acc = h
    for c0, c1 in chunks:
        gate = jnp.dot(xn, wg_ref[:, c0:c1], preferred_element_type=F32)
        up = jnp.dot(xn, wu_ref[:, c0:c1], preferred_element_type=F32)
        act = (jax.nn.silu(gate) * up).astype(BF16)
        acc = acc + jnp.dot(act, wd_ref[c0:c1, :], preferred_element_type=F32)
    o_ref[...] = acc


def _ffn(h, g, wg, wu, wd):
    n_tok = h.shape[0]
    ts = TILE_FFN
    kern = functools.partial(_ffn_kernel, chunks=_ffn_chunks(wg.shape[1]))
    return pl.pallas_call(
        kern,
        grid=(n_tok // ts,),
        in_specs=[pl.BlockSpec((ts, D_MODEL), lambda i: (i, 0)),
                  _const_spec(g.shape), _const_spec(wg.shape),
                  _const_spec(wu.shape), _const_spec(wd.shape)],
        out_specs=pl.BlockSpec((ts, D_MODEL), lambda i: (i, 0)),
        out_shape=jax.ShapeDtypeStruct((n_tok, D_MODEL), F32),
        compiler_params=_params("arbitrary"),
        name="ffn",
    )(h, g, wg, wu, wd)


META_IDX, META_RANK, META_W = 0, 2, 4


def _router_kernel(h_ref, g_ref, wr_ref, tri_ref, meta_ref, cnt_ref, run_ref):
    @pl.when(pl.program_id(0) == 0)
    def _():
        run_ref[...] = jnp.zeros_like(run_ref)

    hn = _rmsnorm(h_ref[...], g_ref[...])
    logits = jnp.dot(hn, wr_ref[...], preferred_element_type=F32,
                     precision=lax.Precision.HIGHEST)
    lane = lax.broadcasted_iota(jnp.int32, logits.shape, 1).astype(F32)
    neg_inf = jnp.float32(-jnp.inf)
    l1 = jnp.where(lane < N_EXPERTS, logits, neg_inf)
    m1 = jnp.max(l1, axis=-1, keepdims=True)
    i1 = jnp.min(jnp.where(l1 == m1, lane, float(LANES)), axis=-1, keepdims=True)
    l2 = jnp.where(lane == i1, neg_inf, l1)
    m2 = jnp.max(l2, axis=-1, keepdims=True)
    i2 = jnp.min(jnp.where(l2 == m2, lane, float(LANES)), axis=-1, keepdims=True)
    e2 = jnp.exp(m2 - m1)
    w1 = 1.0 / (1.0 + e2)
    w2 = e2 / (1.0 + e2)
    sel1 = lane == i1
    sel2 = lane == i2
    onehot = jnp.where(sel1 | sel2, 1.0, 0.0)
    before = jnp.dot(tri_ref[...], onehot.astype(BF16), preferred_element_type=F32)
    rank = before + run_ref[0:1, :]
    r1 = jnp.sum(jnp.where(sel1, rank, 0.0), axis=-1, keepdims=True)
    r2 = jnp.sum(jnp.where(sel2, rank, 0.0), axis=-1, keepdims=True)
    total = run_ref[0:1, :] + jnp.sum(onehot, axis=0, keepdims=True)
    run_ref[0:1, :] = total
    cnt_ref[...] = jnp.broadcast_to(total, cnt_ref.shape)
    meta = jnp.zeros(logits.shape, F32)
    for col, val in ((META_IDX, i1), (META_IDX + 1, i2), (META_RANK, r1), (META_RANK + 1, r2),
                     (META_W, w1), (META_W + 1, w2)):
        meta = jnp.where(lane == col, val, meta)
    meta_ref[...] = meta


def _router(h, g, wr_pad, tri):
    n_tok = h.shape[0]
    ts = TILE_ROUTE
    return pl.pallas_call(
        _router_kernel,
        grid=(n_tok // ts,),
        in_specs=[pl.BlockSpec((ts, D_MODEL), lambda i: (i, 0)),
                  _const_spec(g.shape), _const_spec(wr_pad.shape), _const_spec(tri.shape)],
        out_specs=[pl.BlockSpec((ts, LANES), lambda i: (i, 0)),
                   pl.BlockSpec((SUBLANES, LANES), lambda i: (0, 0))],
        out_shape=[jax.ShapeDtypeStruct((n_tok, LANES), F32),
                   jax.ShapeDtypeStruct((SUBLANES, LANES), F32)],
        scratch_shapes=[pltpu.VMEM((SUBLANES, LANES), F32)],
        compiler_params=_params("arbitrary"),
        name="router",
    )(h, g, wr_pad, tri)


def _row_copy(src, src_row, dst, dst_row, sem):
    return pltpu.make_async_copy(src.at[pl.ds(src_row, 1), :], dst.at[pl.ds(dst_row, 1), :], sem)


def _dispatch_kernel(pos_ref, fill_ref, h_ref, g_ref, xs_ref, hn_buf, zero_buf, sems, *, ts, tm):
    @pl.when(pl.program_id(0) == 0)
    def _():
        zero_buf[...] = jnp.zeros_like(zero_buf)
        for slot in range(2 * N_EXPERTS):
            @pl.when(fill_ref[slot] >= 0)
            def _():
                start = pl.multiple_of(fill_ref[slot] * tm, tm)
                cp = pltpu.make_async_copy(zero_buf, xs_ref.at[pl.ds(start, tm), :], sems.at[2])
                cp.start()
                cp.wait()

    hn_buf[...] = _rmsnorm(h_ref[...], g_ref[...])

    def issue(r, carry):
        _row_copy(hn_buf, r, xs_ref, pos_ref[0, r], sems.at[0]).start()
        _row_copy(hn_buf, r, xs_ref, pos_ref[1, r], sems.at[1]).start()
        return carry

    lax.fori_loop(0, ts, issue, 0)

    def drain(r, carry):
        _row_copy(hn_buf, r, xs_ref, pos_ref[0, r], sems.at[0]).wait()
        _row_copy(hn_buf, r, xs_ref, pos_ref[1, r], sems.at[1]).wait()
        return carry

    lax.fori_loop(0, ts, drain, 0)


def _dispatch(h, g, pos_tiles, fill_tiles, n_rows, tm):
    n_tok = h.shape[0]
    ts = pos_tiles.shape[2]
    kern = functools.partial(_dispatch_kernel, ts=ts, tm=tm)
    return pl.pallas_call(
        kern,
        grid=(n_tok // ts,),
        in_specs=[pl.BlockSpec((None, 2, ts), lambda i: (i, 0, 0), memory_space=pltpu.SMEM),
                  pl.BlockSpec(memory_space=pltpu.SMEM),
                  pl.BlockSpec((ts, D_MODEL), lambda i: (i, 0)),
                  _const_spec(g.shape)],
        out_specs=pl.BlockSpec(memory_space=pl.ANY),
        out_shape=jax.ShapeDtypeStruct((n_rows, D_MODEL), F32),
        scratch_shapes=[pltpu.VMEM((ts, D_MODEL), F32),
                        pltpu.VMEM((tm, D_MODEL), F32),
                        pltpu.SemaphoreType.DMA((3,))],
        compiler_params=_params("arbitrary"),
        name="dispatch",
    )(pos_tiles, fill_tiles, h, g)


def _expert_kernel(te_ref, na_ref, x_ref, wg_ref, wu_ref, wd_ref, o_ref):
    del te_ref
    f = pl.program_id(1)
    active = pl.program_id(0) < na_ref[0]

    @pl.when(jnp.logical_and(jnp.logical_not(active), f == 0))
    def _():
        o_ref[...] = jnp.zeros_like(o_ref)

    @pl.when(active)
    def _():
        x = x_ref[...].astype(BF16)
        gate = jnp.dot(x, wg_ref[...], preferred_element_type=F32)
        up = jnp.dot(x, wu_ref[...], preferred_element_type=F32)
        act = (jax.nn.silu(gate) * up).astype(BF16)
        part = jnp.dot(act, wd_ref[...], preferred_element_type=F32)

        @pl.when(f == 0)
        def _():
            o_ref[...] = part

        @pl.when(f > 0)
        def _():
            o_ref[...] += part


def _experts(xs, wg, wu, wd, tile_expert, n_active, tm):
    n_rows = xs.shape[0]
    d_exp = wg.shape[2]
    n_f = d_exp // FF_CHUNK
    n_tiles = n_rows // tm

    def x_tile(i, na):
        return jnp.minimum(i, na[0] - 1)

    def chunk(i, f, na):
        return jnp.where(i < na[0], f, n_f - 1)

    grid_spec = pltpu.PrefetchScalarGridSpec(
        num_scalar_prefetch=2,
        grid=(n_tiles, n_f),
        in_specs=[
            pl.BlockSpec((tm, D_MODEL), lambda i, f, te, na: (x_tile(i, na), 0)),
            pl.BlockSpec((None, D_MODEL, FF_CHUNK), lambda i, f, te, na: (te[i], 0, chunk(i, f, na))),
            pl.BlockSpec((None, D_MODEL, FF_CHUNK), lambda i, f, te, na: (te[i], 0, chunk(i, f, na))),
            pl.BlockSpec((None, FF_CHUNK, D_MODEL), lambda i, f, te, na: (te[i], chunk(i, f, na), 0)),
        ],
        out_specs=pl.BlockSpec((tm, D_MODEL), lambda i, f, te, na: (i, 0)),
    )
    return pl.pallas_call(
        _expert_kernel,
        grid_spec=grid_spec,
        out_shape=jax.ShapeDtypeStruct((n_rows, D_MODEL), F32),
        compiler_params=_params("arbitrary", "arbitrary"),
        name="experts",
    )(tile_expert, n_active, xs, wg, wu, wd)


def _combine_kernel(pos_ref, h_ref, meta_ref, g_ref, ys_ref, o_ref, buf_a, buf_b, sems, *, ts):
    def issue(r, carry):
        _row_copy(ys_ref, pos_ref[0, r], buf_a, r, sems.at[0]).start()
        _row_copy(ys_ref, pos_ref[1, r], buf_b, r, sems.at[1]).start()
        return carry

    lax.fori_loop(0, ts, issue, 0)

    def drain(r, carry):
        _row_copy(ys_ref, pos_ref[0, r], buf_a, r, sems.at[0]).wait()
        _row_copy(ys_ref, pos_ref[1, r], buf_b, r, sems.at[1]).wait()
        return carry

    lax.fori_loop(0, ts, drain, 0)

    meta = meta_ref[...]
    w1 = meta[:, META_W:META_W + 1]
    w2 = meta[:, META_W + 1:META_W + 2]
    y = h_ref[...] + (w1 * buf_a[...] + w2 * buf_b[...])
    o_ref[...] = _rmsnorm(y, g_ref[...])


def _combine(h, meta, g, ys, pos_tiles):
    n_tok = h.shape[0]
    ts = pos_tiles.shape[2]
    kern = functools.partial(_combine_kernel, ts=ts)
    return pl.pallas_call(
        kern,
        grid=(n_tok // ts,),
        in_specs=[pl.BlockSpec((None, 2, ts), lambda i: (i, 0, 0), memory_space=pltpu.SMEM),
                  pl.BlockSpec((ts, D_MODEL), lambda i: (i, 0)),
                  pl.BlockSpec((ts, LANES), lambda i: (i, 0)),
                  _const_spec(g.shape),
                  pl.BlockSpec(memory_space=pl.ANY)],
        out_specs=pl.BlockSpec((ts, D_MODEL), lambda i: (i, 0)),
        out_shape=jax.ShapeDtypeStruct((n_tok, D_MODEL), F32),
        scratch_shapes=[pltpu.VMEM((ts, D_MODEL), F32),
                        pltpu.VMEM((ts, D_MODEL), F32),
                        pltpu.SemaphoreType.DMA((2,))],
        compiler_params=_params("arbitrary"),
        name="combine",
    )(pos_tiles, h, meta, g, ys)


def _pos_tiles(pos, ts):
    n_tok = pos.shape[0]
    return pos.reshape(n_tok // ts, ts, 2).transpose(0, 2, 1)


def _moe(h, norm_g, wr_pad, tri, wg, wu, wd, final_g):
    n_tok = h.shape[0]
    tm = TILE_EXPERT
    meta, cnt = _router(h, norm_g, wr_pad, tri)

    counts = cnt[0, :N_EXPERTS].astype(jnp.int32)
    tiles = (counts + (tm - 1)) // tm
    tile_end = jnp.cumsum(tiles)
    tile_start = tile_end - tiles
    n_tiles = (2 * n_tok) // tm + N_EXPERTS
    n_rows = n_tiles * tm
    idx = meta[:, META_IDX:META_IDX + 2].astype(jnp.int32)
    rank = meta[:, META_RANK:META_RANK + 2].astype(jnp.int32)
    pos = tile_start[idx] * tm + rank
    n_active = tile_end[-1:].astype(jnp.int32)
    tail_tiles = jnp.where(tiles > 0, tile_end - 1, -1)
    slack_tiles = n_active[0] + jnp.arange(N_EXPERTS)
    slack_tiles = jnp.where(slack_tiles < n_tiles, slack_tiles, -1)
    fill_tiles = jnp.concatenate([tail_tiles, slack_tiles]).astype(jnp.int32)
    t = jnp.minimum(jnp.arange(n_tiles, dtype=jnp.int32), n_active[0] - 1)
    tile_expert = jnp.sum(t[:, None] >= tile_end[None, :], axis=1).astype(jnp.int32)

    xs = _dispatch(h, norm_g, _pos_tiles(pos, TILE_ROUTE), fill_tiles, n_rows, tm)
    ys = _experts(xs, wg, wu, wd, tile_expert, n_active, tm)
    return _combine(h, meta, final_g, ys, _pos_tiles(pos, TILE_COMBINE))


def _pad_last(a, width):
    return jnp.pad(a, [(0, 0)] * (a.ndim - 1) + [(0, width - a.shape[-1])])


def _row(a):
    return a.reshape(1, -1)


def _mixer_params(norm_g, w_in, conv_a_w, conv_a_b, ln_a_g, ln_a_b, conv_b_w, ln_c_g, ln_c_b,
                  gmlp_ws, gmlp_b, mix_out_g, w_mix_out):
    widths = [A_WIDTH, A_WIDTH, B_WIDTH, B_WIDTH, B_WIDTH, C_WIDTH, C_WIDTH]
    bounds = [0]
    for w in widths:
        bounds.append(bounds[-1] + w)
    w_in_p = jnp.concatenate(
        [_pad_last(w_in[:, bounds[s]:bounds[s + 1]], SEG) for s in range(N_IN_SEG)], axis=1)
    out_bounds = [0, A_WIDTH, A_WIDTH + B_WIDTH, A_WIDTH + B_WIDTH + C_WIDTH]
    out_g = jnp.concatenate(
        [_pad_last(mix_out_g[out_bounds[s]:out_bounds[s + 1]], SEG) for s in range(3)])
    w_out_p = jnp.concatenate(
        [jnp.pad(w_mix_out[out_bounds[s]:out_bounds[s + 1]],
                 [(0, SEG - (out_bounds[s + 1] - out_bounds[s])), (0, 0)]) for s in range(3)], axis=0)
    gbias = _pad_last(jnp.repeat(gmlp_b.T, HEAD_GROUP, axis=1), SEG)
    group = jnp.arange(MXU_DIM) // HEAD_GROUP
    gsum = (group[:, None] == group[None, :]).astype(BF16)
    return {
        "norm_g": _row(norm_g), "w_in": w_in_p.astype(BF16),
        "conv_a_w": jnp.repeat(conv_a_w, SUBLANES, axis=0),
        "conv_a_b": jnp.broadcast_to(conv_a_b, (SUBLANES, A_WIDTH)),
        "ln_a_g": _row(ln_a_g), "ln_a_b": _row(ln_a_b),
        "conv_b_w": _pad_last(conv_b_w, SEG),
        "ln_c_g": _row(_pad_last(ln_c_g, SEG)), "ln_c_b": _row(_pad_last(ln_c_b, SEG)),
        "gmlp_ws": gmlp_ws.astype(BF16), "gmlp_bias": gbias,
        "out_g": _row(out_g), "w_out": w_out_p.astype(BF16), "gsum": gsum,
    }


def kernel(x, mem, norm_mix_g, w_in, conv_a_w, conv_a_b, ln_a_g, ln_a_b, conv_b_w, ln_c_g, ln_c_b,
           gmlp_ws, gmlp_b, mix_out_g, w_mix_out, norm_x_g, norm_mem_g, w_xq, w_xkv, w_xo,
           norm_ffn_g, ffn_w_gate, ffn_w_up, ffn_w_down, moe_router, moe_w_gate, moe_w_up,
           moe_w_down, norm_final_g):
    bsz, seq_len, _ = x.shape
    mem_len = mem.shape[1]
    depth = w_in.shape[0]
    assert depth == 2 and ffn_w_gate.shape[0] == 1 and moe_router.shape[0] == 1
    assert seq_len % TILE_MIX == 0 and seq_len % TILE_ATT == 0

    h = x.reshape(bsz * seq_len, D_MODEL)
    mem2d = mem.reshape(bsz * mem_len, D_MODEL)
    tri = (jnp.arange(TILE_ROUTE)[:, None] > jnp.arange(TILE_ROUTE)[None, :]).astype(BF16)
    out = None
    for layer in range(depth):
        mp = _mixer_params(norm_mix_g[layer], w_in[layer], conv_a_w[layer], conv_a_b[layer],
                           ln_a_g[layer], ln_a_b[layer], conv_b_w[layer], ln_c_g[layer],
                           ln_c_b[layer], gmlp_ws[layer], gmlp_b[layer], mix_out_g[layer],
                           w_mix_out[layer])
        h = _mixer(h, mp, seq_len)
        k, v = _kv_proj(mem2d, _row(norm_mem_g[layer]), w_xkv[layer].astype(BF16))
        h = _xattn(h, k, v, _row(norm_x_g[layer]), w_xq[layer].astype(BF16),
                   w_xo[layer].astype(BF16), seq_len, mem_len)
        if layer % 2 == 0:
            i = layer // 2
            h = _ffn(h, _row(norm_ffn_g[layer]), ffn_w_gate[i].astype(BF16),
                     ffn_w_up[i].astype(BF16), ffn_w_down[i].astype(BF16))
        else:
            i = layer // 2
            out = _moe(h, _row(norm_ffn_g[layer]), _pad_last(moe_router[i], LANES), tri,
                       moe_w_gate[i].astype(BF16), moe_w_up[i].astype(BF16),
                       moe_w_down[i].astype(BF16), _row(norm_final_g))
    return out.reshape(bsz, seq_len, D_MODEL)
```

```python
import functools

import jax
import jax.numpy as jnp
from jax import lax
from jax.experimental import pallas as pl
from jax.experimental.pallas import tpu as pltpu

F32 = jnp.float32
BF16 = jnp.bfloat16

D_MODEL = 1024
EPS = 1e-6
CHUNK = 64
HEAD_GROUP = 64
A_WIDTH, B_WIDTH, C_WIDTH = 384, 320, 320
A_KERNEL, B_KERNEL = 31, 3
GMLP_BLOCK = 128
C_GROUPS = 5
X_HEADS = 4
X_HEAD_DIM = D_MODEL // X_HEADS
N_EXPERTS = 8

LANES = 128
SUBLANES = 8
MXU_DIM = 256
VMEM_LIMIT_BYTES = 56 * 1024 * 1024

SEG = 384
N_IN_SEG = 7
MIX_PAD = 3 * SEG
HALO_A = 32
HALO_B = SUBLANES
CONV_ROWS = 32
SHIFT_EXTRA = HALO_A - SUBLANES

TILE_MIX = 512
TILE_ATT = 512
TILE_FFN = 512
TILE_ROUTE = 512
TILE_COMBINE = 256
TILE_EXPERT = 512
FF_CHUNK = 1792
FF_SUBCHUNK = 512
KV_ROWS = 1024
ROW_TILE = 8


def _rmsnorm(x, g):
    ms = jnp.mean(x * x, axis=-1, keepdims=True)
    return x * lax.rsqrt(ms + EPS) * g


def _const_spec(shape):
    zeros = (0,) * len(shape)
    return pl.BlockSpec(shape, lambda *_: zeros, pipeline_mode=pl.Buffered(1))


def _params(*semantics):
    return pltpu.CompilerParams(dimension_semantics=semantics,
                                vmem_limit_bytes=VMEM_LIMIT_BYTES)


def _mixer_kernel(h_ref, ng_ref, win_ref, caw_ref, cab_ref, lag_ref, lab_ref, cbw_ref,
                  lcg_ref, lcb_ref, ws_ref, gbias_ref, og_ref, wout_ref, gsum_ref,
                  o_ref, zbuf, abuf, bbuf, shifted, ybuf, *, ts, tiles_per_seq):
    seq_tile = lax.rem(pl.program_id(0), tiles_per_seq)

    @pl.when(seq_tile == 0)
    def _():
        abuf[0:HALO_A, :] = jnp.zeros((HALO_A, SEG), F32)
        bbuf[0:HALO_B, :] = jnp.zeros((HALO_B, SEG), F32)

    @pl.when(seq_tile > 0)
    def _():
        abuf[0:HALO_A, :] = abuf[ts:ts + HALO_A, :]
        bbuf[0:HALO_B, :] = bbuf[ts:ts + HALO_B, :]

    h = h_ref[...]
    xn = _rmsnorm(h, ng_ref[...]).astype(BF16)

    z_a = jnp.dot(xn, win_ref[:, 0:2 * SEG], preferred_element_type=F32)
    abuf[HALO_A:HALO_A + ts, :] = z_a[:, 0:SEG] * jax.nn.sigmoid(z_a[:, SEG:2 * SEG])
    for phase in range(1, SUBLANES):
        shifted[phase - 1] = abuf[phase:phase + ts + SHIFT_EXTRA, :]

    zbuf[...] = jnp.dot(xn, win_ref[:, 2 * SEG:N_IN_SEG * SEG], preferred_element_type=F32)

    for r in range(0, ts, CONV_ROWS):
        acc = jnp.concatenate([cab_ref[...]] * (CONV_ROWS // SUBLANES), axis=0)
        for k in range(A_KERNEL):
            groups, phase = divmod(HALO_A - A_KERNEL + 1 + k, SUBLANES)
            start = r + groups * SUBLANES
            if phase == 0:
                window = abuf[start:start + CONV_ROWS, :]
            else:
                window = shifted[phase - 1, start:start + CONV_ROWS, :]
            tap = caw_ref[k * SUBLANES:(k + 1) * SUBLANES, :]
            acc = acc + jnp.concatenate([tap] * (CONV_ROWS // SUBLANES), axis=0) * window
        mu = jnp.mean(acc, axis=-1, keepdims=True)
        xc = acc - mu
        var = jnp.mean(xc * xc, axis=-1, keepdims=True)
        ybuf[r:r + CONV_ROWS, 0:SEG] = jax.nn.silu(
            xc * lax.rsqrt(var + EPS) * lag_ref[...] + lab_ref[...])

    bbuf[HALO_B:HALO_B + ts, :] = zbuf[:, 1 * SEG:2 * SEG] * zbuf[:, 2 * SEG:3 * SEG]
    accb = jnp.zeros((ts, SEG), F32)
    for k in range(B_KERNEL):
        first = HALO_B - B_KERNEL + 1 + k
        accb = accb + cbw_ref[k:k + 1, :] * bbuf[first:first + ts, :]
    ybuf[:, 1 * SEG:2 * SEG] = zbuf[:, 0:SEG] * accb

    c_valid = lax.broadcasted_iota(jnp.int32, (1, SEG), 1) < C_WIDTH
    low_group = lax.broadcasted_iota(jnp.int32, (GMLP_BLOCK, LANES), 1) < HEAD_GROUP
    row_chunk = lax.broadcasted_iota(jnp.int32, (GMLP_BLOCK, GMLP_BLOCK), 0) // CHUNK
    col_chunk = lax.broadcasted_iota(jnp.int32, (GMLP_BLOCK, GMLP_BLOCK), 1) // CHUNK
    chunk_causal = row_chunk >= col_chunk
    ws = [jnp.where(chunk_causal, ws_ref[g], jnp.zeros((), BF16)) for g in range(C_GROUPS)]
    for r in range(0, ts, GMLP_BLOCK):
        rows = slice(r, r + GMLP_BLOCK)
        c_u = jax.nn.gelu(zbuf[rows, 3 * SEG:4 * SEG])
        c_v = jax.nn.gelu(zbuf[rows, 4 * SEG:5 * SEG])
        mu = jnp.sum(c_v, axis=-1, keepdims=True) * (1.0 / C_WIDTH)
        xc = jnp.where(c_valid, c_v - mu, 0.0)
        var = jnp.sum(xc * xc, axis=-1, keepdims=True) * (1.0 / C_WIDTH)
        v = (xc * lax.rsqrt(var + EPS) * lcg_ref[...] + lcb_ref[...]).astype(BF16)
        cols = []
        for j in range(SEG // LANES):
            vj = v[:, j * LANES:(j + 1) * LANES]
            col = jnp.dot(ws[2 * j], vj, preferred_element_type=F32)
            if 2 * j + 1 < C_GROUPS:
                col = jnp.where(low_group, col, jnp.dot(ws[2 * j + 1], vj, preferred_element_type=F32))
            cols.append(col)
        mixed = jnp.concatenate(cols, axis=1) + gbias_ref[...]
        ybuf[rows, 2 * SEG:3 * SEG] = c_u * mixed

    y = ybuf[...]
    y2 = (y * y).astype(BF16)
    gsum = gsum_ref[...]
    sums = []
    for c0 in range(0, MIX_PAD, MXU_DIM):
        width = min(MXU_DIM, MIX_PAD - c0)
        sums.append(jnp.dot(y2[:, c0:c0 + width], gsum[0:width, 0:width],
                            preferred_element_type=F32))
    ms = jnp.concatenate(sums, axis=1) * (1.0 / HEAD_GROUP)
    yn = (y * lax.rsqrt(ms + EPS) * og_ref[...]).astype(BF16)
    o_ref[...] = h + jnp.dot(yn, wout_ref[...], preferred_element_type=F32)


def _mixer(h, p, seq_len):
    n_tok = h.shape[0]
    ts = TILE_MIX
    tiles_per_seq = seq_len // ts
    kern = functools.partial(_mixer_kernel, ts=ts, tiles_per_seq=tiles_per_seq)
    consts = [p["norm_g"], p["w_in"], p["conv_a_w"], p["conv_a_b"], p["ln_a_g"], p["ln_a_b"],
              p["conv_b_w"], p["ln_c_g"], p["ln_c_b"], p["gmlp_ws"], p["gmlp_bias"],
              p["out_g"], p["w_out"], p["gsum"]]
    return pl.pallas_call(
        kern,
        grid=(n_tok // ts,),
        in_specs=[pl.BlockSpec((ts, D_MODEL), lambda i: (i, 0))]
                 + [_const_spec(c.shape) for c in consts],
        out_specs=pl.BlockSpec((ts, D_MODEL), lambda i: (i, 0)),
        out_shape=jax.ShapeDtypeStruct((n_tok, D_MODEL), F32),
        scratch_shapes=[
            pltpu.VMEM((ts, (N_IN_SEG - 2) * SEG), F32),
            pltpu.VMEM((HALO_A + ts, SEG), F32),
            pltpu.VMEM((HALO_B + ts, SEG), F32),
            pltpu.VMEM((SUBLANES - 1, ts + SHIFT_EXTRA, SEG), F32),
            pltpu.VMEM((ts, MIX_PAD), F32),
        ],
        compiler_params=_params("arbitrary"),
        name="mixer",
    )(h, *consts)


def _kv_kernel(mem_ref, g_ref, wkv_ref, k_ref, v_ref):
    mn = _rmsnorm(mem_ref[...], g_ref[...]).astype(BF16)
    kv = jnp.dot(mn, wkv_ref[...], preferred_element_type=F32)
    k_ref[...] = kv[:, 0:D_MODEL].astype(BF16)
    v_ref[...] = kv[:, D_MODEL:2 * D_MODEL].astype(BF16)


def _kv_proj(mem2d, g, wkv):
    n = mem2d.shape[0]
    rows = min(KV_ROWS, n)
    return pl.pallas_call(
        _kv_kernel,
        grid=(n // rows,),
        in_specs=[pl.BlockSpec((rows, D_MODEL), lambda i: (i, 0)),
                  _const_spec(g.shape), _const_spec(wkv.shape)],
        out_specs=[pl.BlockSpec((rows, D_MODEL), lambda i: (i, 0))] * 2,
        out_shape=[jax.ShapeDtypeStruct((n, D_MODEL), BF16)] * 2,
        compiler_params=_params("arbitrary"),
        name="kv_proj",
    )(mem2d, g, wkv)


def _xattn_kernel(h_ref, g_ref, wq_ref, k_ref, v_ref, wo_ref, o_ref):
    h = h_ref[...]
    xn = _rmsnorm(h, g_ref[...]).astype(BF16)
    q = (jnp.dot(xn, wq_ref[...], preferred_element_type=F32) * (X_HEAD_DIM ** -0.5)).astype(BF16)
    heads = []
    for hd in range(X_HEADS):
        cols = slice(hd * X_HEAD_DIM, (hd + 1) * X_HEAD_DIM)
        s = lax.dot_general(q[:, cols], k_ref[:, cols], (((1,), (1,)), ((), ())),
                            preferred_element_type=F32)
        e = jnp.exp(s - jnp.max(s, axis=-1, keepdims=True))
        pv = jnp.dot(e.astype(BF16), v_ref[:, cols], preferred_element_type=F32)
        heads.append(pv / jnp.sum(e, axis=-1, keepdims=True))
    o = jnp.concatenate(heads, axis=1).astype(BF16)
    o_ref[...] = h + jnp.dot(o, wo_ref[...], preferred_element_type=F32)


def _xattn(h, k, v, g, wq, wo, seq_len, mem_len):
    n_tok = h.shape[0]
    ts = TILE_ATT
    tiles_per_seq = seq_len // ts
    return pl.pallas_call(
        _xattn_kernel,
        grid=(n_tok // ts,),
        in_specs=[pl.BlockSpec((ts, D_MODEL), lambda i: (i, 0)),
                  _const_spec(g.shape), _const_spec(wq.shape),
                  pl.BlockSpec((mem_len, D_MODEL), lambda i: (i // tiles_per_seq, 0)),
                  pl.BlockSpec((mem_len, D_MODEL), lambda i: (i // tiles_per_seq, 0)),
                  _const_spec(wo.shape)],
        out_specs=pl.BlockSpec((ts, D_MODEL), lambda i: (i, 0)),
        out_shape=jax.ShapeDtypeStruct((n_tok, D_MODEL), F32),
        compiler_params=_params("arbitrary"),
        name="xattn",
    )(h, g, wq, k, v, wo)


def _ffn_chunks(d_ff):
    step = -(-d_ff // (3 * MXU_DIM)) * MXU_DIM
    return [(c0, min(c0 + step, d_ff)) for c0 in range(0, d_ff, step)]


def _ffn_kernel(h_ref, g_ref, wg_ref, wu_ref, wd_ref, o_ref, *, chunks):
    h = h_ref[...]
    xn = _rmsnorm(h, g_ref[...]).astype(BF16)
    acc = h
    for c0, c1 in chunks:
        gate = jnp.dot(xn, wg_ref[:, c0:c1], preferred_element_type=F32)
        up = jnp.dot(xn, wu_ref[:, c0:c1], preferred_element_type=F32)
        act = (jax.nn.silu(gate) * up).astype(BF16)
        acc = acc + jnp.dot(act, wd_ref[c0:c1, :], preferred_element_type=F32)
    o_ref[...] = acc


def _ffn(h, g, wg, wu, wd):
    n_tok = h.shape[0]
    ts = TILE_FFN
    kern = functools.partial(_ffn_kernel, chunks=_ffn_chunks(wg.shape[1]))
    return pl.pallas_call(
        kern,
        grid=(n_tok // ts,),
        in_specs=[pl.BlockSpec((ts, D_MODEL), lambda i: (i, 0)),
                  _const_spec(g.shape), _const_spec(wg.shape),
                  _const_spec(wu.shape), _const_spec(wd.shape)],
        out_specs=pl.BlockSpec((ts, D_MODEL), lambda i: (i, 0)),
        out_shape=jax.ShapeDtypeStruct((n_tok, D_MODEL), F32),
        compiler_params=_params("arbitrary"),
        name="ffn",
    )(h, g, wg, wu, wd)


META_IDX, META_RANK, META_W = 0, 2, 4


def _router_kernel(h_ref, g_ref, wr_ref, tri_ref, meta_ref, cnt_ref, run_ref):
    @pl.when(pl.program_id(0) == 0)
    def _():
        run_ref[...] = jnp.zeros_like(run_ref)

    hn = _rmsnorm(h_ref[...], g_ref[...])
    hn_hi = hn.astype(BF16)
    hn_lo = (hn - hn_hi.astype(F32)).astype(BF16)
    both = jnp.dot(hn_hi, wr_ref[...], preferred_element_type=F32)
    logits = (both[:, 0:LANES] + both[:, LANES:2 * LANES]
              + jnp.dot(hn_lo, wr_ref[:, 0:LANES], preferred_element_type=F32))
    lane = lax.broadcasted_iota(jnp.int32, logits.shape, 1).astype(F32)
    neg_inf = jnp.float32(-jnp.inf)
    l1 = jnp.where(lane < N_EXPERTS, logits, neg_inf)
    m1 = jnp.max(l1, axis=-1, keepdims=True)
    i1 = jnp.min(jnp.where(l1 == m1, lane, float(LANES)), axis=-1, keepdims=True)
    l2 = jnp.where(lane == i1, neg_inf, l1)
    m2 = jnp.max(l2, axis=-1, keepdims=True)
    i2 = jnp.min(jnp.where(l2 == m2, lane, float(LANES)), axis=-1, keepdims=True)
    e2 = jnp.exp(m2 - m1)
    w1 = 1.0 / (1.0 + e2)
    w2 = e2 / (1.0 + e2)
    sel1 = lane == i1
    sel2 = lane == i2
    onehot = jnp.where(sel1 | sel2, 1.0, 0.0)
    before = jnp.dot(tri_ref[...], onehot.astype(BF16), preferred_element_type=F32)
    rank = before + run_ref[0:1, :]
    r1 = jnp.sum(jnp.where(sel1, rank, 0.0), axis=-1, keepdims=True)
    r2 = jnp.sum(jnp.where(sel2, rank, 0.0), axis=-1, keepdims=True)
    total = run_ref[0:1, :] + jnp.sum(onehot, axis=0, keepdims=True)
    run_ref[0:1, :] = total
    cnt_ref[...] = jnp.broadcast_to(total, cnt_ref.shape)
    meta = jnp.zeros(logits.shape, F32)
    for col, val in ((META_IDX, i1), (META_IDX + 1, i2), (META_RANK, r1), (META_RANK + 1, r2),
                     (META_W, w1), (META_W + 1, w2)):
        meta = jnp.where(lane == col, val, meta)
    meta_ref[...] = meta


def _router(h, g, wr_pad, tri):
    n_tok = h.shape[0]
    ts = TILE_ROUTE
    return pl.pallas_call(
        _router_kernel,
        grid=(n_tok // ts,),
        in_specs=[pl.BlockSpec((ts, D_MODEL), lambda i: (i, 0)),
                  _const_spec(g.shape), _const_spec(wr_pad.shape), _const_spec(tri.shape)],
        out_specs=[pl.BlockSpec((ts, LANES), lambda i: (i, 0)),
                   pl.BlockSpec((SUBLANES, LANES), lambda i: (0, 0))],
        out_shape=[jax.ShapeDtypeStruct((n_tok, LANES), F32),
                   jax.ShapeDtypeStruct((SUBLANES, LANES), F32)],
        scratch_shapes=[pltpu.VMEM((SUBLANES, LANES), F32)],
        compiler_params=_params("arbitrary"),
        name="router",
    )(h, g, wr_pad, tri)


def _rows_to_tiles(x, dst):
    n = x.shape[0]
    for g in range(n // SUBLANES):
        for j in range(ROW_TILE):
            dst[pl.ds(g * SUBLANES * ROW_TILE + j, SUBLANES, stride=ROW_TILE), :] = (
                x[g * SUBLANES:(g + 1) * SUBLANES, j * LANES:(j + 1) * LANES])


def _tiles_to_rows(src, n):
    groups = []
    for g in range(n // SUBLANES):
        groups.append(jnp.concatenate(
            [src[pl.ds(g * SUBLANES * ROW_TILE + j, SUBLANES, stride=ROW_TILE), :]
             for j in range(ROW_TILE)], axis=1))
    return jnp.concatenate(groups, axis=0)


def _tile_rows(ref, row):
    return ref.at[pl.ds(pl.multiple_of(row * ROW_TILE, ROW_TILE), ROW_TILE), :]


def _wait_rows(buf, sem):
    pltpu.make_async_copy(buf, buf, sem).wait()


def _dispatch_kernel(pos_ref, fill_ref, h_ref, g_ref, xs_ref, hn_buf, zero_buf, sems, fill_sem,
                     *, ts, tm):
    step = pl.program_id(0)
    slot = lax.rem(step, 2)

    @pl.when(step == 0)
    def _():
        zero_buf[...] = jnp.zeros_like(zero_buf)
        for k in range(2 * N_EXPERTS):
            @pl.when(fill_ref[k] >= 0)
            def _():
                cp = pltpu.make_async_copy(
                    zero_buf, xs_ref.at[pl.ds(fill_ref[k] * (tm * ROW_TILE), tm * ROW_TILE), :],
                    fill_sem)
                cp.start()
                cp.wait()

    rows = hn_buf.at[slot]
    _rows_to_tiles(_rmsnorm(h_ref[...], g_ref[...]), rows)

    def issue(q, carry):
        for l in range(LANES):
            k = l % 2
            src = _tile_rows(rows, q * (LANES // 2) + l // 2)
            pltpu.make_async_copy(src, _tile_rows(xs_ref, pos_ref[q, l]),
                                  sems.at[slot, k]).start(priority=k)
        return carry

    lax.fori_loop(0, 2 * ts // LANES, issue, 0)

    @pl.when(step > 0)
    def _():
        for k in range(2):
            _wait_rows(hn_buf.at[1 - slot], sems.at[1 - slot, k])

    @pl.when(step == pl.num_programs(0) - 1)
    def _():
        for k in range(2):
            _wait_rows(rows, sems.at[slot, k])


def _dispatch(h, g, pos_tiles, fill_tiles, n_rows, tm):
    n_tok = h.shape[0]
    n_steps = pos_tiles.shape[0]
    ts = n_tok // n_steps
    kern = functools.partial(_dispatch_kernel, ts=ts, tm=tm)
    return pl.pallas_call(
        kern,
        grid=(n_steps,),
        in_specs=[pl.BlockSpec((None,) + pos_tiles.shape[1:], lambda i: (i, 0, 0),
                               memory_space=pltpu.SMEM),
                  pl.BlockSpec(memory_space=pltpu.SMEM),
                  pl.BlockSpec((ts, D_MODEL), lambda i: (i, 0)),
                  _const_spec(g.shape)],
        out_specs=pl.BlockSpec(memory_space=pl.ANY),
        out_shape=jax.ShapeDtypeStruct((n_rows * ROW_TILE, LANES), F32),
        scratch_shapes=[pltpu.VMEM((2, ts * ROW_TILE, LANES), F32),
                        pltpu.VMEM((tm * ROW_TILE, LANES), F32),
                        pltpu.SemaphoreType.DMA((2, 2)),
                        pltpu.SemaphoreType.DMA(())],
        compiler_params=_params("arbitrary"),
        name="dispatch",
    )(pos_tiles, fill_tiles, h, g)


def _expert_kernel(te_ref, na_ref, x_ref, wg_ref, wu_ref, wd_ref, o_ref, acc_ref, *, tm, n_f):
    del te_ref
    f = pl.program_id(1)
    active = pl.program_id(0) < na_ref[0]

    @pl.when(jnp.logical_and(jnp.logical_not(active), f == 0))
    def _():
        o_ref[...] = jnp.zeros_like(o_ref)

    @pl.when(active)
    def _():
        x = _tiles_to_rows(x_ref, tm).astype(BF16)
        part = None
        for c0 in range(0, FF_CHUNK, FF_SUBCHUNK):
            c1 = min(c0 + FF_SUBCHUNK, FF_CHUNK)
            gate = jnp.dot(x, wg_ref[:, c0:c1], preferred_element_type=F32)
            up = jnp.dot(x, wu_ref[:, c0:c1], preferred_element_type=F32)
            act = (jax.nn.silu(gate) * up).astype(BF16)
            down = jnp.dot(act, wd_ref[c0:c1, :], preferred_element_type=F32)
            part = down if part is None else part + down

        @pl.when(f == 0)
        def _():
            acc_ref[...] = part

        @pl.when(jnp.logical_and(f > 0, f < n_f - 1))
        def _():
            acc_ref[...] += part

        @pl.when(f == n_f - 1)
        def _():
            _rows_to_tiles(acc_ref[...] + part, o_ref)


def _experts(xs, wg, wu, wd, tile_expert, n_active, tm):
    d_exp = wg.shape[2]
    n_f = d_exp // FF_CHUNK
    assert n_f >= 2
    n_tiles = xs.shape[0] // (tm * ROW_TILE)

    def x_tile(i, na):
        return jnp.minimum(i, na[0] - 1)

    def chunk(i, f, na):
        return jnp.where(i < na[0], f, n_f - 1)

    grid_spec = pltpu.PrefetchScalarGridSpec(
        num_scalar_prefetch=2,
        grid=(n_tiles, n_f),
        in_specs=[
            pl.BlockSpec((tm * ROW_TILE, LANES), lambda i, f, te, na: (x_tile(i, na), 0)),
            pl.BlockSpec((None, D_MODEL, FF_CHUNK), lambda i, f, te, na: (te[i], 0, chunk(i, f, na))),
            pl.BlockSpec((None, D_MODEL, FF_CHUNK), lambda i, f, te, na: (te[i], 0, chunk(i, f, na))),
            pl.BlockSpec((None, FF_CHUNK, D_MODEL), lambda i, f, te, na: (te[i], chunk(i, f, na), 0)),
        ],
        out_specs=pl.BlockSpec((tm * ROW_TILE, LANES), lambda i, f, te, na: (i, 0)),
        scratch_shapes=[pltpu.VMEM((tm, D_MODEL), F32)],
    )
    return pl.pallas_call(
        functools.partial(_expert_kernel, tm=tm, n_f=n_f),
        grid_spec=grid_spec,
        out_shape=jax.ShapeDtypeStruct(xs.shape, F32),
        compiler_params=_params("arbitrary", "arbitrary"),
        name="experts",
    )(tile_expert, n_active, xs, wg, wu, wd)


def _combine_kernel(pos_ref, next_pos_ref, h_ref, meta_ref, g_ref, ys_ref, o_ref, bufs, sems, *, ts):
    step = pl.program_id(0)
    slot = lax.rem(step, 2)

    def gather(table, dst_slot):
        def issue(q, carry):
            for l in range(LANES):
                k = l % 2
                dst = _tile_rows(bufs.at[dst_slot, k], q * (LANES // 2) + l // 2)
                pltpu.make_async_copy(_tile_rows(ys_ref, table[q, l]), dst,
                                      sems.at[dst_slot, k]).start(priority=k)
            return carry

        lax.fori_loop(0, 2 * ts // LANES, issue, 0)

    @pl.when(step == 0)
    def _():
        gather(pos_ref, slot)

    @pl.when(step + 1 < pl.num_programs(0))
    def _():
        gather(next_pos_ref, 1 - slot)

    for k in range(2):
        _wait_rows(bufs.at[slot, k], sems.at[slot, k])
    meta = meta_ref[...]
    w1 = meta[:, META_W:META_W + 1]
    w2 = meta[:, META_W + 1:META_W + 2]
    y = h_ref[...] + (w1 * _tiles_to_rows(bufs.at[slot, 0], ts)
                      + w2 * _tiles_to_rows(bufs.at[slot, 1], ts))
    o_ref[...] = _rmsnorm(y, g_ref[...])


def _combine(h, meta, g, ys, pos_tiles):
    n_tok = h.shape[0]
    n_steps = pos_tiles.shape[0]
    ts = n_tok // n_steps
    kern = functools.partial(_combine_kernel, ts=ts)
    pos_block = (None,) + pos_tiles.shape[1:]
    return pl.pallas_call(
        kern,
        grid=(n_steps,),
        in_specs=[pl.BlockSpec(pos_block, lambda i: (i, 0, 0), memory_space=pltpu.SMEM),
                  pl.BlockSpec(pos_block, lambda i: (jnp.minimum(i + 1, n_steps - 1), 0, 0),
                               memory_space=pltpu.SMEM),
                  pl.BlockSpec((ts, D_MODEL), lambda i: (i, 0)),
                  pl.BlockSpec((ts, LANES), lambda i: (i, 0)),
                  _const_spec(g.shape),
                  pl.BlockSpec(memory_space=pl.ANY)],
        out_specs=pl.BlockSpec((ts, D_MODEL), lambda i: (i, 0)),
        out_shape=jax.ShapeDtypeStruct((n_tok, D_MODEL), F32),
        scratch_shapes=[pltpu.VMEM((2, 2, ts * ROW_TILE, LANES), F32),
                        pltpu.SemaphoreType.DMA((2, 2))],
        compiler_params=_params("arbitrary"),
        name="combine",
    )(pos_tiles, pos_tiles, h, meta, g, ys)


def _pos_tiles(pos, ts):
    n_tok = pos.shape[0]
    return pos.reshape(n_tok // ts, 2 * ts // LANES, LANES)


def _moe(h, norm_g, wr_pad, tri, wg, wu, wd, final_g):
    n_tok = h.shape[0]
    tm = TILE_EXPERT
    meta, cnt = _router(h, norm_g, wr_pad, tri)

    counts = cnt[0, :N_EXPERTS].astype(jnp.int32)
    tiles = (counts + (tm - 1)) // tm
    tile_end = jnp.cumsum(tiles)
    tile_start = tile_end - tiles
    n_tiles = (2 * n_tok) // tm + N_EXPERTS
    n_rows = n_tiles * tm
    idx = meta[:, META_IDX:META_IDX + 2].astype(jnp.int32)
    rank = meta[:, META_RANK:META_RANK + 2].astype(jnp.int32)
    pos = tile_start[idx] * tm + rank
    n_active = tile_end[-1:].astype(jnp.int32)
    tail_tiles = jnp.where(tiles > 0, tile_end - 1, -1)
    slack_tiles = n_active[0] + jnp.arange(N_EXPERTS)
    slack_tiles = jnp.where(slack_tiles < n_tiles, slack_tiles, -1)
    fill_tiles = jnp.concatenate([tail_tiles, slack_tiles]).astype(jnp.int32)
    t = jnp.minimum(jnp.arange(n_tiles, dtype=jnp.int32), n_active[0] - 1)
    tile_expert = jnp.sum(t[:, None] >= tile_end[None, :], axis=1).astype(jnp.int32)

    xs = _dispatch(h, norm_g, _pos_tiles(pos, TILE_ROUTE), fill_tiles, n_rows, tm)
    ys = _experts(xs, wg, wu, wd, tile_expert, n_active, tm)
    return _combine(h, meta, final_g, ys, _pos_tiles(pos, TILE_COMBINE))


def _pad_last(a, width):
    return jnp.pad(a, [(0, 0)] * (a.ndim - 1) + [(0, width - a.shape[-1])])


def _row(a):
    return a.reshape(1, -1)


def _router_weight(w):
    w_hi = w.astype(BF16)
    w_lo = (w - w_hi.astype(F32)).astype(BF16)
    return jnp.concatenate([_pad_last(w_hi, LANES), _pad_last(w_lo, LANES)], axis=1)


def _mixer_params(norm_g, w_in, conv_a_w, conv_a_b, ln_a_g, ln_a_b, conv_b_w, ln_c_g, ln_c_b,
                  gmlp_ws, gmlp_b, mix_out_g, w_mix_out):
    widths = [A_WIDTH, A_WIDTH, B_WIDTH, B_WIDTH, B_WIDTH, C_WIDTH, C_WIDTH]
    bounds = [0]
    for w in widths:
        bounds.append(bounds[-1] + w)
    w_in_p = jnp.concatenate(
        [_pad_last(w_in[:, bounds[s]:bounds[s + 1]], SEG) for s in range(N_IN_SEG)], axis=1)
    out_bounds = [0, A_WIDTH, A_WIDTH + B_WIDTH, A_WIDTH + B_WIDTH + C_WIDTH]
    out_g = jnp.concatenate(
        [_pad_last(mix_out_g[out_bounds[s]:out_bounds[s + 1]], SEG) for s in range(3)])
    w_out_p = jnp.concatenate(
        [jnp.pad(w_mix_out[out_bounds[s]:out_bounds[s + 1]],
                 [(0, SEG - (out_bounds[s + 1] - out_bounds[s])), (0, 0)]) for s in range(3)], axis=0)
    gbias = _pad_last(jnp.repeat(gmlp_b.T, HEAD_GROUP, axis=1), SEG)
    group = jnp.arange(MXU_DIM) // HEAD_GROUP
    gsum = (group[:, None] == group[None, :]).astype(BF16)
    return {
        "norm_g": _row(norm_g), "w_in": w_in_p.astype(BF16),
        "conv_a_w": jnp.repeat(conv_a_w, SUBLANES, axis=0),
        "conv_a_b": jnp.broadcast_to(conv_a_b, (SUBLANES, A_WIDTH)),
        "ln_a_g": _row(ln_a_g), "ln_a_b": _row(ln_a_b),
        "conv_b_w": _pad_last(conv_b_w, SEG),
        "ln_c_g": _row(_pad_last(ln_c_g, SEG)), "ln_c_b": _row(_pad_last(ln_c_b, SEG)),
        "gmlp_ws": gmlp_ws.astype(BF16), "gmlp_bias": gbias,
        "out_g": _row(out_g), "w_out": w_out_p.astype(BF16), "gsum": gsum,
    }


def kernel(x, mem, norm_mix_g, w_in, conv_a_w, conv_a_b, ln_a_g, ln_a_b, conv_b_w, ln_c_g, ln_c_b,
           gmlp_ws, gmlp_b, mix_out_g, w_mix_out, norm_x_g, norm_mem_g, w_xq, w_xkv, w_xo,
           norm_ffn_g, ffn_w_gate, ffn_w_up, ffn_w_down, moe_router, moe_w_gate, moe_w_up,
           moe_w_down, norm_final_g):
    bsz, seq_len, _ = x.shape
    mem_len = mem.shape[1]
    depth = w_in.shape[0]
    assert depth == 2 and ffn_w_gate.shape[0] == 1 and moe_router.shape[0] == 1
    assert seq_len % TILE_MIX == 0 and seq_len % TILE_ATT == 0

    h = x.reshape(bsz * seq_len, D_MODEL)
    mem2d = mem.reshape(bsz * mem_len, D_MODEL)
    tri = (jnp.arange(TILE_ROUTE)[:, None] > jnp.arange(TILE_ROUTE)[None, :]).astype(BF16)
    out = None
    for layer in range(depth):
        mp = _mixer_params(norm_mix_g[layer], w_in[layer], conv_a_w[layer], conv_a_b[layer],
                           ln_a_g[layer], ln_a_b[layer], conv_b_w[layer], ln_c_g[layer],
                           ln_c_b[layer], gmlp_ws[layer], gmlp_b[layer], mix_out_g[layer],
                           w_mix_out[layer])
        h = _mixer(h, mp, seq_len)
        k, v = _kv_proj(mem2d, _row(norm_mem_g[layer]), w_xkv[layer].astype(BF16))
        h = _xattn(h, k, v, _row(norm_x_g[layer]), w_xq[layer].astype(BF16),
                   w_xo[layer].astype(BF16), seq_len, mem_len)
        if layer % 2 == 0:
            i = layer // 2
            h = _ffn(h, _row(norm_ffn_g[layer]), ffn_w_gate[i].astype(BF16),
                     ffn_w_up[i].astype(BF16), ffn_w_down[i].astype(BF16))
        else:
            i = layer // 2
            out = _moe(h, _row(norm_ffn_g[layer]), _router_weight(moe_router[i]), tri,
                       moe_w_gate[i].astype(BF16), moe_w_up[i].astype(BF16),
                       moe_w_down[i].astype(BF16), _row(norm_final_g))
    return out.reshape(bsz, seq_len, D_MODEL)
```

```python
import functools

import jax
import jax.numpy as jnp
from jax import lax
from jax.experimental import pallas as pl
from jax.experimental.pallas import tpu as pltpu

F32 = jnp.float32
BF16 = jnp.bfloat16

D_MODEL = 1024
EPS = 1e-6
CHUNK = 64
HEAD_GROUP = 64
A_WIDTH, B_WIDTH, C_WIDTH = 384, 320, 320
A_KERNEL, B_KERNEL = 31, 3
GMLP_BLOCK = 128
C_GROUPS = 5
X_HEADS = 4
X_HEAD_DIM = D_MODEL // X_HEADS
N_EXPERTS = 8

LANES = 128
SUBLANES = 8
MXU_DIM = 256
VMEM_LIMIT_BYTES = 56 * 1024 * 1024

SEG = 384
N_IN_SEG = 7
MIX_PAD = 3 * SEG
HALO_A = 32
HALO_B = SUBLANES
CONV_ROWS = 32
EDGE_PARTS = 2
SHIFT_EXTRA = HALO_A - SUBLANES

TILE_MIX = 512
TILE_ATT = 512
TILE_FFN = 512
TILE_ROUTE = 512
TILE_COMBINE = 512
TILE_EXPERT = 512
FF_CHUNK = 1792
FF_SUBCHUNK = 512
KV_ROWS = 1024
ROW_TILE = 8


def _rmsnorm(x, g):
    ms = jnp.mean(x * x, axis=-1, keepdims=True)
    return x * lax.rsqrt(ms + EPS) * g


def _const_spec(shape):
    zeros = (0,) * len(shape)
    return pl.BlockSpec(shape, lambda *_: zeros, pipeline_mode=pl.Buffered(1))


def _params(*semantics):
    return pltpu.CompilerParams(dimension_semantics=semantics,
                                vmem_limit_bytes=VMEM_LIMIT_BYTES)


def _mixer_kernel(h_ref, ng_ref, win_ref, caw_ref, cab_ref, lag_ref, lab_ref, cbw_ref,
                  lcg_ref, lcb_ref, ws_ref, gbias_ref, og_ref, wout_ref, gsum_ref,
                  o_ref, zbuf, abuf, bbuf, shifted, ybuf, *, ts, tiles_per_seq):
    seq_tile = lax.rem(pl.program_id(0), tiles_per_seq)

    @pl.when(seq_tile == 0)
    def _():
        abuf[0:HALO_A, :] = jnp.zeros((HALO_A, SEG), F32)
        bbuf[0:HALO_B, :] = jnp.zeros((HALO_B, SEG), F32)

    @pl.when(seq_tile > 0)
    def _():
        abuf[0:HALO_A, :] = abuf[ts:ts + HALO_A, :]
        bbuf[0:HALO_B, :] = bbuf[ts:ts + HALO_B, :]

    h = h_ref[...]
    xn = _rmsnorm(h, ng_ref[...]).astype(BF16)

    part = ts // EDGE_PARTS
    for r0 in range(0, ts, part):
        z_a = jnp.dot(xn[r0:r0 + part, :], win_ref[:, 0:2 * SEG], preferred_element_type=F32)
        abuf[HALO_A + r0:HALO_A + r0 + part, :] = z_a[:, 0:SEG] * jax.nn.sigmoid(z_a[:, SEG:2 * SEG])
        lo = 0 if r0 == 0 else r0 + SHIFT_EXTRA
        hi = r0 + part + SHIFT_EXTRA
        for phase in range(1, SUBLANES):
            shifted[phase - 1, lo:hi, :] = abuf[lo + phase:hi + phase, :]

    zbuf[...] = jnp.dot(xn, win_ref[:, 2 * SEG:N_IN_SEG * SEG], preferred_element_type=F32)

    for r in range(0, ts, CONV_ROWS):
        acc = jnp.concatenate([cab_ref[...]] * (CONV_ROWS // SUBLANES), axis=0)
        for k in range(A_KERNEL):
            groups, phase = divmod(HALO_A - A_KERNEL + 1 + k, SUBLANES)
            start = r + groups * SUBLANES
            if phase == 0:
                window = abuf[start:start + CONV_ROWS, :]
            else:
                window = shifted[phase - 1, start:start + CONV_ROWS, :]
            tap = caw_ref[k * SUBLANES:(k + 1) * SUBLANES, :]
            acc = acc + jnp.concatenate([tap] * (CONV_ROWS // SUBLANES), axis=0) * window
        mu = jnp.mean(acc, axis=-1, keepdims=True)
        xc = acc - mu
        var = jnp.mean(xc * xc, axis=-1, keepdims=True)
        ybuf[r:r + CONV_ROWS, 0:SEG] = jax.nn.silu(
            xc * lax.rsqrt(var + EPS) * lag_ref[...] + lab_ref[...])

    bbuf[HALO_B:HALO_B + ts, :] = zbuf[:, 1 * SEG:2 * SEG] * zbuf[:, 2 * SEG:3 * SEG]
    accb = jnp.zeros((ts, SEG), F32)
    for k in range(B_KERNEL):
        first = HALO_B - B_KERNEL + 1 + k
        accb = accb + cbw_ref[k:k + 1, :] * bbuf[first:first + ts, :]
    ybuf[:, 1 * SEG:2 * SEG] = zbuf[:, 0:SEG] * accb

    c_valid = lax.broadcasted_iota(jnp.int32, (1, SEG), 1) < C_WIDTH
    low_group = lax.broadcasted_iota(jnp.int32, (GMLP_BLOCK, LANES), 1) < HEAD_GROUP
    row_chunk = lax.broadcasted_iota(jnp.int32, (GMLP_BLOCK, GMLP_BLOCK), 0) // CHUNK
    col_chunk = lax.broadcasted_iota(jnp.int32, (GMLP_BLOCK, GMLP_BLOCK), 1) // CHUNK
    chunk_causal = row_chunk >= col_chunk
    ws = [jnp.where(chunk_causal, ws_ref[g], jnp.zeros((), BF16)) for g in range(C_GROUPS)]
    for r in range(0, ts, GMLP_BLOCK):
        rows = slice(r, r + GMLP_BLOCK)
        c_u = jax.nn.gelu(zbuf[rows, 3 * SEG:4 * SEG])
        c_v = jax.nn.gelu(zbuf[rows, 4 * SEG:5 * SEG])
        mu = jnp.sum(c_v, axis=-1, keepdims=True) * (1.0 / C_WIDTH)
        xc = jnp.where(c_valid, c_v - mu, 0.0)
        var = jnp.sum(xc * xc, axis=-1, keepdims=True) * (1.0 / C_WIDTH)
        v = (xc * lax.rsqrt(var + EPS) * lcg_ref[...] + lcb_ref[...]).astype(BF16)
        cols = []
        for j in range(SEG // LANES):
            vj = v[:, j * LANES:(j + 1) * LANES]
            col = jnp.dot(ws[2 * j], vj, preferred_element_type=F32)
            if 2 * j + 1 < C_GROUPS:
                col = jnp.where(low_group, col, jnp.dot(ws[2 * j + 1], vj, preferred_element_type=F32))
            cols.append(col)
        mixed = jnp.concatenate(cols, axis=1) + gbias_ref[...]
        ybuf[rows, 2 * SEG:3 * SEG] = c_u * mixed

    gsum = gsum_ref[...]
    for r0 in range(0, ts, part):
        y = ybuf[r0:r0 + part, :]
        y2 = (y * y).astype(BF16)
        sums = []
        for c0 in range(0, MIX_PAD, MXU_DIM):
            width = min(MXU_DIM, MIX_PAD - c0)
            sums.append(jnp.dot(y2[:, c0:c0 + width], gsum[0:width, 0:width],
                                preferred_element_type=F32))
        ms = jnp.concatenate(sums, axis=1)
        yn = (y * lax.rsqrt(ms + EPS) * og_ref[...]).astype(BF16)
        o_ref[r0:r0 + part, :] = h[r0:r0 + part, :] + jnp.dot(yn, wout_ref[...], preferred_element_type=F32)


def _mixer(h, p, seq_len):
    n_tok = h.shape[0]
    ts = TILE_MIX
    tiles_per_seq = seq_len // ts
    kern = functools.partial(_mixer_kernel, ts=ts, tiles_per_seq=tiles_per_seq)
    consts = [p["norm_g"], p["w_in"], p["conv_a_w"], p["conv_a_b"], p["ln_a_g"], p["ln_a_b"],
              p["conv_b_w"], p["ln_c_g"], p["ln_c_b"], p["gmlp_ws"], p["gmlp_bias"],
              p["out_g"], p["w_out"], p["gsum"]]
    return pl.pallas_call(
        kern,
        grid=(n_tok // ts,),
        in_specs=[pl.BlockSpec((ts, D_MODEL), lambda i: (i, 0))]
                 + [_const_spec(c.shape) for c in consts],
        out_specs=pl.BlockSpec((ts, D_MODEL), lambda i: (i, 0)),
        out_shape=jax.ShapeDtypeStruct((n_tok, D_MODEL), F32),
        scratch_shapes=[
            pltpu.VMEM((ts, (N_IN_SEG - 2) * SEG), F32),
            pltpu.VMEM((HALO_A + ts, SEG), F32),
            pltpu.VMEM((HALO_B + ts, SEG), F32),
            pltpu.VMEM((SUBLANES - 1, ts + SHIFT_EXTRA, SEG), F32),
            pltpu.VMEM((ts, MIX_PAD), F32),
        ],
        compiler_params=_params("arbitrary"),
        name="mixer",
    )(h, *consts)


def _kv_kernel(mem_ref, g_ref, wkv_ref, k_ref, v_ref):
    mn = _rmsnorm(mem_ref[...], g_ref[...]).astype(BF16)
    kv = jnp.dot(mn, wkv_ref[...], preferred_element_type=F32)
    k_ref[...] = kv[:, 0:D_MODEL].astype(BF16)
    v_ref[...] = kv[:, D_MODEL:2 * D_MODEL].astype(BF16)


def _kv_proj(mem2d, g, wkv):
    n = mem2d.shape[0]
    rows = min(KV_ROWS, n)
    return pl.pallas_call(
        _kv_kernel,
        grid=(n // rows,),
        in_specs=[pl.BlockSpec((rows, D_MODEL), lambda i: (i, 0)),
                  _const_spec(g.shape), _const_spec(wkv.shape)],
        out_specs=[pl.BlockSpec((rows, D_MODEL), lambda i: (i, 0))] * 2,
        out_shape=[jax.ShapeDtypeStruct((n, D_MODEL), BF16)] * 2,
        compiler_params=_params("arbitrary"),
        name="kv_proj",
    )(mem2d, g, wkv)


def _xattn_kernel(h_ref, g_ref, wq_ref, k_ref, v_ref, wo_ref, o_ref):
    h = h_ref[...]
    xn = _rmsnorm(h, g_ref[...]).astype(BF16)
    q = (jnp.dot(xn, wq_ref[...], preferred_element_type=F32) * (X_HEAD_DIM ** -0.5)).astype(BF16)
    heads = []
    for hd in range(X_HEADS):
        cols = slice(hd * X_HEAD_DIM, (hd + 1) * X_HEAD_DIM)
        s = lax.dot_general(q[:, cols], k_ref[:, cols], (((1,), (1,)), ((), ())),
                            preferred_element_type=F32)
        e = jnp.exp(s - jnp.max(s, axis=-1, keepdims=True))
        pv = jnp.dot(e.astype(BF16), v_ref[:, cols], preferred_element_type=F32)
        heads.append(pv / jnp.sum(e, axis=-1, keepdims=True))
    o = jnp.concatenate(heads, axis=1).astype(BF16)
    o_ref[...] = h + jnp.dot(o, wo_ref[...], preferred_element_type=F32)


def _xattn(h, k, v, g, wq, wo, seq_len, mem_len):
    n_tok = h.shape[0]
    ts = TILE_ATT
    tiles_per_seq = seq_len // ts
    return pl.pallas_call(
        _xattn_kernel,
        grid=(n_tok // ts,),
        in_specs=[pl.BlockSpec((ts, D_MODEL), lambda i: (i, 0)),
                  _const_spec(g.shape), _const_spec(wq.shape),
                  pl.BlockSpec((mem_len, D_MODEL), lambda i: (i // tiles_per_seq, 0)),
                  pl.BlockSpec((mem_len, D_MODEL), lambda i: (i // tiles_per_seq, 0)),
                  _const_spec(wo.shape)],
        out_specs=pl.BlockSpec((ts, D_MODEL), lambda i: (i, 0)),
        out_shape=jax.ShapeDtypeStruct((n_tok, D_MODEL), F32),
        compiler_params=_params("arbitrary"),
        name="xattn",
    )(h, g, wq, k, v, wo)


def _ffn_chunks(d_ff):
    step = -(-d_ff // (3 * MXU_DIM)) * MXU_DIM
    return [(c0, min(c0 + step, d_ff)) for c0 in range(0, d_ff, step)]


def _ffn_kernel(h_ref, g_ref, wg_ref, wu_ref, wd_ref, o_ref, *, chunks):
    h = h_ref[...]
    xn = _rmsnorm(h, g_ref[...]).astype(BF16)
    acc = h
    for c0, c1 in chunks:
        gate = jnp.dot(xn, wg_ref[:, c0:c1], preferred_element_type=F32)
        up = jnp.dot(xn, wu_ref[:, c0:c1], preferred_element_type=F32)
        act = (jax.nn.silu(gate) * up).astype(BF16)
        acc = acc + jnp.dot(act, wd_ref[c0:c1, :], preferred_element_type=F32)
    o_ref[...] = acc


def _ffn(h, g, wg, wu, wd):
    n_tok = h.shape[0]
    ts = TILE_FFN
    kern = functools.partial(_ffn_kernel, chunks=_ffn_chunks(wg.shape[1]))
    return pl.pallas_call(
        kern,
        grid=(n_tok // ts,),
        in_specs=[pl.BlockSpec((ts, D_MODEL), lambda i: (i, 0)),
                  _const_spec(g.shape), _const_spec(wg.shape),
                  _const_spec(wu.shape), _const_spec(wd.shape)],
        out_specs=pl.BlockSpec((ts, D_MODEL), lambda i: (i, 0)),
        out_shape=jax.ShapeDtypeStruct((n_tok, D_MODEL), F32),
        compiler_params=_params("arbitrary"),
        name="ffn",
    )(h, g, wg, wu, wd)


META_IDX, META_RANK, META_W = 0, 2, 4


def _router_kernel(h_ref, g_ref, wr_ref, tri_ref, meta_ref, cnt_ref, run_ref):
    @pl.when(pl.program_id(0) == 0)
    def _():
        run_ref[...] = jnp.zeros_like(run_ref)

    hn = _rmsnorm(h_ref[...], g_ref[...])
    hn_hi = hn.astype(BF16)
    hn_lo = (hn - hn_hi.astype(F32)).astype(BF16)
    both = jnp.dot(hn_hi, wr_ref[...], preferred_element_type=F32)
    logits = (both[:, 0:LANES] + both[:, LANES:2 * LANES]
              + jnp.dot(hn_lo, wr_ref[:, 0:LANES], preferred_element_type=F32))
    lane = lax.broadcasted_iota(jnp.int32, logits.shape, 1).astype(F32)
    neg_inf = jnp.float32(-jnp.inf)
    l1 = jnp.where(lane < N_EXPERTS, logits, neg_inf)
    m1 = jnp.max(l1, axis=-1, keepdims=True)
    i1 = jnp.min(jnp.where(l1 == m1, lane, float(LANES)), axis=-1, keepdims=True)
    l2 = jnp.where(lane == i1, neg_inf, l1)
    m2 = jnp.max(l2, axis=-1, keepdims=True)
    i2 = jnp.min(jnp.where(l2 == m2, lane, float(LANES)), axis=-1, keepdims=True)
    e2 = jnp.exp(m2 - m1)
    w1 = 1.0 / (1.0 + e2)
    w2 = e2 / (1.0 + e2)
    sel1 = lane == i1
    sel2 = lane == i2
    onehot = jnp.where(sel1 | sel2, 1.0, 0.0)
    before = jnp.dot(tri_ref[...], onehot.astype(BF16), preferred_element_type=F32)
    rank = before + run_ref[0:1, :]
    r1 = jnp.sum(jnp.where(sel1, rank, 0.0), axis=-1, keepdims=True)
    r2 = jnp.sum(jnp.where(sel2, rank, 0.0), axis=-1, keepdims=True)
    total = run_ref[0:1, :] + jnp.sum(onehot, axis=0, keepdims=True)
    run_ref[0:1, :] = total
    cnt_ref[...] = jnp.broadcast_to(total, cnt_ref.shape)
    meta = jnp.zeros(logits.shape, F32)
    for col, val in ((META_IDX, i1), (META_IDX + 1, i2), (META_RANK, r1), (META_RANK + 1, r2),
                     (META_W, w1), (META_W + 1, w2)):
        meta = jnp.where(lane == col, val, meta)
    meta_ref[...] = meta


def _router(h, g, wr_pad, tri):
    n_tok = h.shape[0]
    ts = TILE_ROUTE
    return pl.pallas_call(
        _router_kernel,
        grid=(n_tok // ts,),
        in_specs=[pl.BlockSpec((ts, D_MODEL), lambda i: (i, 0)),
                  _const_spec(g.shape), _const_spec(wr_pad.shape), _const_spec(tri.shape)],
        out_specs=[pl.BlockSpec((ts, LANES), lambda i: (i, 0)),
                   pl.BlockSpec((SUBLANES, LANES), lambda i: (0, 0))],
        out_shape=[jax.ShapeDtypeStruct((n_tok, LANES), F32),
                   jax.ShapeDtypeStruct((SUBLANES, LANES), F32)],
        scratch_shapes=[pltpu.VMEM((SUBLANES, LANES), F32)],
        compiler_params=_params("arbitrary"),
        name="router",
    )(h, g, wr_pad, tri)


def _rows_to_tiles(x, dst):
    n = x.shape[0]
    for g in range(n // SUBLANES):
        for j in range(ROW_TILE):
            dst[pl.ds(g * SUBLANES * ROW_TILE + j, SUBLANES, stride=ROW_TILE), :] = (
                x[g * SUBLANES:(g + 1) * SUBLANES, j * LANES:(j + 1) * LANES])


def _tiles_to_rows(src, n):
    groups = []
    for g in range(n // SUBLANES):
        groups.append(jnp.concatenate(
            [src[pl.ds(g * SUBLANES * ROW_TILE + j, SUBLANES, stride=ROW_TILE), :]
             for j in range(ROW_TILE)], axis=1))
    return jnp.concatenate(groups, axis=0)


def _tile_rows(ref, row):
    return ref.at[pl.ds(pl.multiple_of(row * ROW_TILE, ROW_TILE), ROW_TILE), :]


def _wait_rows(buf, sem):
    pltpu.make_async_copy(buf, buf, sem).wait()


def _dispatch_kernel(pos_ref, fill_ref, h_ref, g_ref, xs_ref, hn_buf, zero_buf, sems, fill_sem,
                     *, ts, tm):
    step = pl.program_id(0)
    slot = lax.rem(step, 2)

    @pl.when(step == 0)
    def _():
        zero_buf[...] = jnp.zeros_like(zero_buf)
        for k in range(2 * N_EXPERTS):
            @pl.when(fill_ref[k] >= 0)
            def _():
                cp = pltpu.make_async_copy(
                    zero_buf, xs_ref.at[pl.ds(fill_ref[k] * (tm * ROW_TILE), tm * ROW_TILE), :],
                    fill_sem)
                cp.start()
                cp.wait()

    rows = hn_buf.at[slot]
    _rows_to_tiles(_rmsnorm(h_ref[...], g_ref[...]), rows)

    def issue(q, carry):
        for l in range(LANES):
            k = l % 2
            src = _tile_rows(rows, q * (LANES // 2) + l // 2)
            pltpu.make_async_copy(src, _tile_rows(xs_ref, pos_ref[q, l]),
                                  sems.at[slot, k]).start(priority=k)
        return carry

    lax.fori_loop(0, 2 * ts // LANES, issue, 0)

    @pl.when(step > 0)
    def _():
        for k in range(2):
            _wait_rows(hn_buf.at[1 - slot], sems.at[1 - slot, k])

    @pl.when(step == pl.num_programs(0) - 1)
    def _():
        for k in range(2):
            _wait_rows(rows, sems.at[slot, k])


def _dispatch(h, g, pos_tiles, fill_tiles, n_rows, tm):
    n_tok = h.shape[0]
    n_steps = pos_tiles.shape[0]
    ts = n_tok // n_steps
    kern = functools.partial(_dispatch_kernel, ts=ts, tm=tm)
    return pl.pallas_call(
        kern,
        grid=(n_steps,),
        in_specs=[pl.BlockSpec((None,) + pos_tiles.shape[1:], lambda i: (i, 0, 0),
                               memory_space=pltpu.SMEM),
                  pl.BlockSpec(memory_space=pltpu.SMEM),
                  pl.BlockSpec((ts, D_MODEL), lambda i: (i, 0)),
                  _const_spec(g.shape)],
        out_specs=pl.BlockSpec(memory_space=pl.ANY),
        out_shape=jax.ShapeDtypeStruct((n_rows * ROW_TILE, LANES), F32),
        scratch_shapes=[pltpu.VMEM((2, ts * ROW_TILE, LANES), F32),
                        pltpu.VMEM((tm * ROW_TILE, LANES), F32),
                        pltpu.SemaphoreType.DMA((2, 2)),
                        pltpu.SemaphoreType.DMA(())],
        compiler_params=_params("arbitrary"),
        name="dispatch",
    )(pos_tiles, fill_tiles, h, g)


def _expert_kernel(te_ref, na_ref, x_ref, wg_ref, wu_ref, wd_ref, o_ref, acc_ref, x_bf, *, tm, n_f):
    del te_ref
    f = pl.program_id(1)
    active = pl.program_id(0) < na_ref[0]

    @pl.when(jnp.logical_and(jnp.logical_not(active), f == 0))
    def _():
        o_ref[...] = jnp.zeros_like(o_ref)

    @pl.when(jnp.logical_and(active, f == 0))
    def _():
        x_bf[...] = _tiles_to_rows(x_ref, tm).astype(BF16)

    @pl.when(active)
    def _():
        x = x_bf[...]
        part = None
        for c0 in range(0, FF_CHUNK, FF_SUBCHUNK):
            c1 = min(c0 + FF_SUBCHUNK, FF_CHUNK)
            gate = jnp.dot(x, wg_ref[:, c0:c1], preferred_element_type=F32)
            up = jnp.dot(x, wu_ref[:, c0:c1], preferred_element_type=F32)
            act = (jax.nn.silu(gate) * up).astype(BF16)
            down = jnp.dot(act, wd_ref[c0:c1, :], preferred_element_type=F32)
            part = down if part is None else part + down

        @pl.when(f == 0)
        def _():
            acc_ref[...] = part

        @pl.when(jnp.logical_and(f > 0, f < n_f - 1))
        def _():
            acc_ref[...] += part

        @pl.when(f == n_f - 1)
        def _():
            _rows_to_tiles(acc_ref[...] + part, o_ref)


def _experts(xs, wg, wu, wd, tile_expert, n_active, tm):
    d_exp = wg.shape[2]
    n_f = d_exp // FF_CHUNK
    assert n_f >= 2
    n_tiles = xs.shape[0] // (tm * ROW_TILE)

    def x_tile(i, na):
        return jnp.minimum(i, na[0] - 1)

    def chunk(i, f, na):
        return jnp.where(i < na[0], f, n_f - 1)

    grid_spec = pltpu.PrefetchScalarGridSpec(
        num_scalar_prefetch=2,
        grid=(n_tiles, n_f),
        in_specs=[
            pl.BlockSpec((tm * ROW_TILE, LANES), lambda i, f, te, na: (x_tile(i, na), 0)),
            pl.BlockSpec((None, D_MODEL, FF_CHUNK), lambda i, f, te, na: (te[i], 0, chunk(i, f, na))),
            pl.BlockSpec((None, D_MODEL, FF_CHUNK), lambda i, f, te, na: (te[i], 0, chunk(i, f, na))),
            pl.BlockSpec((None, FF_CHUNK, D_MODEL), lambda i, f, te, na: (te[i], chunk(i, f, na), 0)),
        ],
        out_specs=pl.BlockSpec((tm * ROW_TILE, LANES), lambda i, f, te, na: (i, 0)),
        scratch_shapes=[pltpu.VMEM((tm, D_MODEL), F32), pltpu.VMEM((tm, D_MODEL), BF16)],
    )
    return pl.pallas_call(
        functools.partial(_expert_kernel, tm=tm, n_f=n_f),
        grid_spec=grid_spec,
        out_shape=jax.ShapeDtypeStruct(xs.shape, F32),
        compiler_params=_params("arbitrary", "arbitrary"),
        name="experts",
    )(tile_expert, n_active, xs, wg, wu, wd)


def _combine_kernel(pos_ref, next_pos_ref, h_ref, meta_ref, g_ref, ys_ref, o_ref, bufs, sems, *, ts):
    step = pl.program_id(0)
    slot = lax.rem(step, 2)

    def gather(table, dst_slot):
        def issue(q, carry):
            for l in range(LANES):
                k = l % 2
                dst = _tile_rows(bufs.at[dst_slot, k], q * (LANES // 2) + l // 2)
                pltpu.make_async_copy(_tile_rows(ys_ref, table[q, l]), dst,
                                      sems.at[dst_slot, k]).start(priority=k)
            return carry

        lax.fori_loop(0, 2 * ts // LANES, issue, 0)

    @pl.when(step == 0)
    def _():
        gather(pos_ref, slot)

    @pl.when(step + 1 < pl.num_programs(0))
    def _():
        gather(next_pos_ref, 1 - slot)

    for k in range(2):
        _wait_rows(bufs.at[slot, k], sems.at[slot, k])
    meta = meta_ref[...]
    w1 = meta[:, META_W:META_W + 1]
    w2 = meta[:, META_W + 1:META_W + 2]
    y = h_ref[...] + (w1 * _tiles_to_rows(bufs.at[slot, 0], ts)
                      + w2 * _tiles_to_rows(bufs.at[slot, 1], ts))
    o_ref[...] = _rmsnorm(y, g_ref[...])


def _combine(h, meta, g, ys, pos_tiles):
    n_tok = h.shape[0]
    n_steps = pos_tiles.shape[0]
    ts = n_tok // n_steps
    kern = functools.partial(_combine_kernel, ts=ts)
    pos_block = (None,) + pos_tiles.shape[1:]
    return pl.pallas_call(
        kern,
        grid=(n_steps,),
        in_specs=[pl.BlockSpec(pos_block, lambda i: (i, 0, 0), memory_space=pltpu.SMEM),
                  pl.BlockSpec(pos_block, lambda i: (jnp.minimum(i + 1, n_steps - 1), 0, 0),
                               memory_space=pltpu.SMEM),
                  pl.BlockSpec((ts, D_MODEL), lambda i: (i, 0)),
                  pl.BlockSpec((ts, LANES), lambda i: (i, 0)),
                  _const_spec(g.shape),
                  pl.BlockSpec(memory_space=pl.ANY)],
        out_specs=pl.BlockSpec((ts, D_MODEL), lambda i: (i, 0)),
        out_shape=jax.ShapeDtypeStruct((n_tok, D_MODEL), F32),
        scratch_shapes=[pltpu.VMEM((2, 2, ts * ROW_TILE, LANES), F32),
                        pltpu.SemaphoreType.DMA((2, 2))],
        compiler_params=_params("arbitrary"),
        name="combine",
    )(pos_tiles, pos_tiles, h, meta, g, ys)


def _pos_tiles(pos, ts):
    n_tok = pos.shape[0]
    return pos.reshape(n_tok // ts, 2 * ts // LANES, LANES)


def _moe(h, norm_g, wr_pad, tri, wg, wu, wd, final_g):
    n_tok = h.shape[0]
    tm = TILE_EXPERT
    meta, cnt = _router(h, norm_g, wr_pad, tri)

    counts = cnt[0, :N_EXPERTS].astype(jnp.int32)
    tiles = (counts + (tm - 1)) // tm
    tile_end = jnp.cumsum(tiles)
    tile_start = tile_end - tiles
    n_tiles = (2 * n_tok) // tm + N_EXPERTS
    n_rows = n_tiles * tm
    idx = meta[:, META_IDX:META_IDX + 2].astype(jnp.int32)
    rank = meta[:, META_RANK:META_RANK + 2].astype(jnp.int32)
    pos = tile_start[idx] * tm + rank
    n_active = tile_end[-1:].astype(jnp.int32)
    tail_tiles = jnp.where(tiles > 0, tile_end - 1, -1)
    slack_tiles = n_active[0] + jnp.arange(N_EXPERTS)
    slack_tiles = jnp.where(slack_tiles < n_tiles, slack_tiles, -1)
    fill_tiles = jnp.concatenate([tail_tiles, slack_tiles]).astype(jnp.int32)
    t = jnp.minimum(jnp.arange(n_tiles, dtype=jnp.int32), n_active[0] - 1)
    tile_expert = jnp.sum(t[:, None] >= tile_end[None, :], axis=1).astype(jnp.int32)

    xs = _dispatch(h, norm_g, _pos_tiles(pos, TILE_ROUTE), fill_tiles, n_rows, tm)
    ys = _experts(xs, wg, wu, wd, tile_expert, n_active, tm)
    return _combine(h, meta, final_g, ys, _pos_tiles(pos, TILE_COMBINE))


def _pad_last(a, width):
    return jnp.pad(a, [(0, 0)] * (a.ndim - 1) + [(0, width - a.shape[-1])])


def _row(a):
    return a.reshape(1, -1)


def _router_weight(w):
    w_hi = w.astype(BF16)
    w_lo = (w - w_hi.astype(F32)).astype(BF16)
    return jnp.concatenate([_pad_last(w_hi, LANES), _pad_last(w_lo, LANES)], axis=1)


def _mixer_params(norm_g, w_in, conv_a_w, conv_a_b, ln_a_g, ln_a_b, conv_b_w, ln_c_g, ln_c_b,
                  gmlp_ws, gmlp_b, mix_out_g, w_mix_out):
    widths = [A_WIDTH, A_WIDTH, B_WIDTH, B_WIDTH, B_WIDTH, C_WIDTH, C_WIDTH]
    bounds = [0]
    for w in widths:
        bounds.append(bounds[-1] + w)
    w_in_p = jnp.concatenate(
        [_pad_last(w_in[:, bounds[s]:bounds[s + 1]], SEG) for s in range(N_IN_SEG)], axis=1)
    out_bounds = [0, A_WIDTH, A_WIDTH + B_WIDTH, A_WIDTH + B_WIDTH + C_WIDTH]
    out_g = jnp.concatenate(
        [_pad_last(mix_out_g[out_bounds[s]:out_bounds[s + 1]], SEG) for s in range(3)])
    w_out_p = jnp.concatenate(
        [jnp.pad(w_mix_out[out_bounds[s]:out_bounds[s + 1]],
                 [(0, SEG - (out_bounds[s + 1] - out_bounds[s])), (0, 0)]) for s in range(3)], axis=0)
    gbias = _pad_last(jnp.repeat(gmlp_b.T, HEAD_GROUP, axis=1), SEG)
    group = jnp.arange(MXU_DIM) // HEAD_GROUP
    gsum = ((group[:, None] == group[None, :]) * (1.0 / HEAD_GROUP)).astype(BF16)
    return {
        "norm_g": _row(norm_g), "w_in": w_in_p.astype(BF16),
        "conv_a_w": jnp.repeat(conv_a_w, SUBLANES, axis=0),
        "conv_a_b": jnp.broadcast_to(conv_a_b, (SUBLANES, A_WIDTH)),
        "ln_a_g": _row(ln_a_g), "ln_a_b": _row(ln_a_b),
        "conv_b_w": _pad_last(conv_b_w, SEG),
        "ln_c_g": _row(_pad_last(ln_c_g, SEG)), "ln_c_b": _row(_pad_last(ln_c_b, SEG)),
        "gmlp_ws": gmlp_ws.astype(BF16), "gmlp_bias": gbias,
        "out_g": _row(out_g), "w_out": w_out_p.astype(BF16), "gsum": gsum,
    }


def kernel(x, mem, norm_mix_g, w_in, conv_a_w, conv_a_b, ln_a_g, ln_a_b, conv_b_w, ln_c_g, ln_c_b,
           gmlp_ws, gmlp_b, mix_out_g, w_mix_out, norm_x_g, norm_mem_g, w_xq, w_xkv, w_xo,
           norm_ffn_g, ffn_w_gate, ffn_w_up, ffn_w_down, moe_router, moe_w_gate, moe_w_up,
           moe_w_down, norm_final_g):
    bsz, seq_len, _ = x.shape
    mem_len = mem.shape[1]
    depth = w_in.shape[0]
    assert depth == 2 and ffn_w_gate.shape[0] == 1 and moe_router.shape[0] == 1
    assert seq_len % TILE_MIX == 0 and seq_len % TILE_ATT == 0

    h = x.reshape(bsz * seq_len, D_MODEL)
    mem2d = mem.reshape(bsz * mem_len, D_MODEL)
    tri = (jnp.arange(TILE_ROUTE)[:, None] > jnp.arange(TILE_ROUTE)[None, :]).astype(BF16)
    out = None
    for layer in range(depth):
        mp = _mixer_params(norm_mix_g[layer], w_in[layer], conv_a_w[layer], conv_a_b[layer],
                           ln_a_g[layer], ln_a_b[layer], conv_b_w[layer], ln_c_g[layer],
                           ln_c_b[layer], gmlp_ws[layer], gmlp_b[layer], mix_out_g[layer],
                           w_mix_out[layer])
        h = _mixer(h, mp, seq_len)
        k, v = _kv_proj(mem2d, _row(norm_mem_g[layer]), w_xkv[layer].astype(BF16))
        h = _xattn(h, k, v, _row(norm_x_g[layer]), w_xq[layer].astype(BF16),
                   w_xo[layer].astype(BF16), seq_len, mem_len)
        if layer % 2 == 0:
            i = layer // 2
            h = _ffn(h, _row(norm_ffn_g[layer]), ffn_w_gate[i].astype(BF16),
                     ffn_w_up[i].astype(BF16), ffn_w_down[i].astype(BF16))
        else:
            i = layer // 2
            out = _moe(h, _row(norm_ffn_g[layer]), _router_weight(moe_router[i]), tri,
                       moe_w_gate[i].astype(BF16), moe_w_up[i].astype(BF16),
                       moe_w_down[i].astype(BF16), _row(norm_final_g))
    return out.reshape(bsz, seq_len, D_MODEL)
```

```python
import functools

import jax
import jax.numpy as jnp
from jax import lax
from jax.experimental import pallas as pl
from jax.experimental.pallas import tpu as pltpu

F32 = jnp.float32
BF16 = jnp.bfloat16

D_MODEL = 1024
EPS = 1e-6
CHUNK = 64
HEAD_GROUP = 64
A_WIDTH, B_WIDTH, C_WIDTH = 384, 320, 320
A_KERNEL, B_KERNEL = 31, 3
GMLP_BLOCK = 128
C_GROUPS = 5
X_HEADS = 4
X_HEAD_DIM = D_MODEL // X_HEADS
N_EXPERTS = 8

LANES = 128
SUBLANES = 8
MXU_DIM = 256
VMEM_LIMIT_BYTES = 56 * 1024 * 1024

SEG = 384
N_IN_SEG = 7
MIX_PAD = 3 * SEG
HALO_A = 32
HALO_B = SUBLANES
CONV_ROWS = 32
EDGE_PARTS = 2
SHIFT_EXTRA = HALO_A - SUBLANES

TILE_MIX = 512
TILE_ATT = 512
TILE_FFN = 512
TILE_ROUTE = 512
TILE_COMBINE = 256
TILE_EXPERT = 512
FF_CHUNK = 1792
FF_SUBCHUNK = 512
KV_ROWS = 1024
ROW_TILE = 8


def _rmsnorm(x, g):
    ms = jnp.mean(x * x, axis=-1, keepdims=True)
    return x * lax.rsqrt(ms + EPS) * g


def _const_spec(shape):
    zeros = (0,) * len(shape)
    return pl.BlockSpec(shape, lambda *_: zeros, pipeline_mode=pl.Buffered(1))


def _params(*semantics):
    return pltpu.CompilerParams(dimension_semantics=semantics,
                                vmem_limit_bytes=VMEM_LIMIT_BYTES)


def _mixer_kernel(h_ref, ng_ref, win_ref, caw_ref, cab_ref, lag_ref, lab_ref, cbw_ref,
                  lcg_ref, lcb_ref, ws_ref, gbias_ref, og_ref, wout_ref, gsum_ref,
                  o_ref, zbuf, abuf, bbuf, shifted, ybuf, *, ts, tiles_per_seq):
    seq_tile = lax.rem(pl.program_id(0), tiles_per_seq)

    @pl.when(seq_tile == 0)
    def _():
        abuf[0:HALO_A, :] = jnp.zeros((HALO_A, SEG), F32)
        bbuf[0:HALO_B, :] = jnp.zeros((HALO_B, SEG), F32)

    @pl.when(seq_tile > 0)
    def _():
        abuf[0:HALO_A, :] = abuf[ts:ts + HALO_A, :]
        bbuf[0:HALO_B, :] = bbuf[ts:ts + HALO_B, :]

    h = h_ref[...]
    xn = _rmsnorm(h, ng_ref[...]).astype(BF16)

    part = ts // EDGE_PARTS
    for r0 in range(0, ts, part):
        z_a = jnp.dot(xn[r0:r0 + part, :], win_ref[:, 0:2 * SEG], preferred_element_type=F32)
        abuf[HALO_A + r0:HALO_A + r0 + part, :] = z_a[:, 0:SEG] * jax.nn.sigmoid(z_a[:, SEG:2 * SEG])
        lo = 0 if r0 == 0 else r0 + SHIFT_EXTRA
        hi = r0 + part + SHIFT_EXTRA
        for phase in range(1, SUBLANES):
            shifted[phase - 1, lo:hi, :] = abuf[lo + phase:hi + phase, :]

    zbuf[...] = jnp.dot(xn, win_ref[:, 2 * SEG:N_IN_SEG * SEG], preferred_element_type=F32)

    for r in range(0, ts, CONV_ROWS):
        acc = jnp.concatenate([cab_ref[...]] * (CONV_ROWS // SUBLANES), axis=0)
        for k in range(A_KERNEL):
            groups, phase = divmod(HALO_A - A_KERNEL + 1 + k, SUBLANES)
            start = r + groups * SUBLANES
            if phase == 0:
                window = abuf[start:start + CONV_ROWS, :]
            else:
                window = shifted[phase - 1, start:start + CONV_ROWS, :]
            tap = caw_ref[k * SUBLANES:(k + 1) * SUBLANES, :]
            acc = acc + jnp.concatenate([tap] * (CONV_ROWS // SUBLANES), axis=0) * window
        mu = jnp.mean(acc, axis=-1, keepdims=True)
        xc = acc - mu
        var = jnp.mean(xc * xc, axis=-1, keepdims=True)
        ybuf[r:r + CONV_ROWS, 0:SEG] = jax.nn.silu(
            xc * lax.rsqrt(var + EPS) * lag_ref[...] + lab_ref[...])

    bbuf[HALO_B:HALO_B + ts, :] = zbuf[:, 1 * SEG:2 * SEG] * zbuf[:, 2 * SEG:3 * SEG]
    accb = jnp.zeros((ts, SEG), F32)
    for k in range(B_KERNEL):
        first = HALO_B - B_KERNEL + 1 + k
        accb = accb + cbw_ref[k:k + 1, :] * bbuf[first:first + ts, :]
    ybuf[:, 1 * SEG:2 * SEG] = zbuf[:, 0:SEG] * accb

    c_valid = lax.broadcasted_iota(jnp.int32, (1, SEG), 1) < C_WIDTH
    low_group = lax.broadcasted_iota(jnp.int32, (GMLP_BLOCK, LANES), 1) < HEAD_GROUP
    row_chunk = lax.broadcasted_iota(jnp.int32, (GMLP_BLOCK, GMLP_BLOCK), 0) // CHUNK
    col_chunk = lax.broadcasted_iota(jnp.int32, (GMLP_BLOCK, GMLP_BLOCK), 1) // CHUNK
    chunk_causal = row_chunk >= col_chunk
    ws = [jnp.where(chunk_causal, ws_ref[g], jnp.zeros((), BF16)) for g in range(C_GROUPS)]
    for r in range(0, ts, GMLP_BLOCK):
        rows = slice(r, r + GMLP_BLOCK)
        c_u = jax.nn.gelu(zbuf[rows, 3 * SEG:4 * SEG])
        c_v = jax.nn.gelu(zbuf[rows, 4 * SEG:5 * SEG])
        mu = jnp.sum(c_v, axis=-1, keepdims=True) * (1.0 / C_WIDTH)
        xc = jnp.where(c_valid, c_v - mu, 0.0)
        var = jnp.sum(xc * xc, axis=-1, keepdims=True) * (1.0 / C_WIDTH)
        v = (xc * lax.rsqrt(var + EPS) * lcg_ref[...] + lcb_ref[...]).astype(BF16)
        cols = []
        for j in range(SEG // LANES):
            vj = v[:, j * LANES:(j + 1) * LANES]
            col = jnp.dot(ws[2 * j], vj, preferred_element_type=F32)
            if 2 * j + 1 < C_GROUPS:
                col = jnp.where(low_group, col, jnp.dot(ws[2 * j + 1], vj, preferred_element_type=F32))
            cols.append(col)
        mixed = jnp.concatenate(cols, axis=1) + gbias_ref[...]
        ybuf[rows, 2 * SEG:3 * SEG] = c_u * mixed

    gsum = gsum_ref[...]
    for r0 in range(0, ts, part):
        y = ybuf[r0:r0 + part, :]
        y2 = (y * y).astype(BF16)
        sums = []
        for c0 in range(0, MIX_PAD, MXU_DIM):
            width = min(MXU_DIM, MIX_PAD - c0)
            sums.append(jnp.dot(y2[:, c0:c0 + width], gsum[0:width, 0:width],
                                preferred_element_type=F32))
        ms = jnp.concatenate(sums, axis=1)
        yn = (y * lax.rsqrt(ms + EPS) * og_ref[...]).astype(BF16)
        o_ref[r0:r0 + part, :] = h[r0:r0 + part, :] + jnp.dot(yn, wout_ref[...], preferred_element_type=F32)


def _mixer(h, p, seq_len):
    n_tok = h.shape[0]
    ts = TILE_MIX
    tiles_per_seq = seq_len // ts
    kern = functools.partial(_mixer_kernel, ts=ts, tiles_per_seq=tiles_per_seq)
    consts = [p["norm_g"], p["w_in"], p["conv_a_w"], p["conv_a_b"], p["ln_a_g"], p["ln_a_b"],
              p["conv_b_w"], p["ln_c_g"], p["ln_c_b"], p["gmlp_ws"], p["gmlp_bias"],
              p["out_g"], p["w_out"], p["gsum"]]
    return pl.pallas_call(
        kern,
        grid=(n_tok // ts,),
        in_specs=[pl.BlockSpec((ts, D_MODEL), lambda i: (i, 0))]
                 + [_const_spec(c.shape) for c in consts],
        out_specs=pl.BlockSpec((ts, D_MODEL), lambda i: (i, 0)),
        out_shape=jax.ShapeDtypeStruct((n_tok, D_MODEL), F32),
        scratch_shapes=[
            pltpu.VMEM((ts, (N_IN_SEG - 2) * SEG), F32),
            pltpu.VMEM((HALO_A + ts, SEG), F32),
            pltpu.VMEM((HALO_B + ts, SEG), F32),
            pltpu.VMEM((SUBLANES - 1, ts + SHIFT_EXTRA, SEG), F32),
            pltpu.VMEM((ts, MIX_PAD), F32),
        ],
        compiler_params=_params("arbitrary"),
        name="mixer",
    )(h, *consts)


def _kv_kernel(mem_ref, g_ref, wkv_ref, k_ref, v_ref):
    mn = _rmsnorm(mem_ref[...], g_ref[...]).astype(BF16)
    kv = jnp.dot(mn, wkv_ref[...], preferred_element_type=F32)
    k_ref[...] = kv[:, 0:D_MODEL].astype(BF16)
    v_ref[...] = kv[:, D_MODEL:2 * D_MODEL].astype(BF16)


def _kv_proj(mem2d, g, wkv):
    n = mem2d.shape[0]
    rows = min(KV_ROWS, n)
    return pl.pallas_call(
        _kv_kernel,
        grid=(n // rows,),
        in_specs=[pl.BlockSpec((rows, D_MODEL), lambda i: (i, 0)),
                  _const_spec(g.shape), _const_spec(wkv.shape)],
        out_specs=[pl.BlockSpec((rows, D_MODEL), lambda i: (i, 0))] * 2,
        out_shape=[jax.ShapeDtypeStruct((n, D_MODEL), BF16)] * 2,
        compiler_params=_params("arbitrary"),
        name="kv_proj",
    )(mem2d, g, wkv)


def _side_cast_specs(side, n_steps):
    rows = side.shape[0] // n_steps
    assert side.shape[0] % n_steps == 0 and rows % (2 * SUBLANES) == 0
    spec = pl.BlockSpec((rows, side.shape[1]), lambda i: (i, 0))
    return spec, spec, jax.ShapeDtypeStruct(side.shape, BF16)


def _xattn_kernel(h_ref, g_ref, wq_ref, k_ref, v_ref, wo_ref, side_ref, o_ref, side_out_ref):
    side_out_ref[...] = side_ref[...].astype(BF16)
    h = h_ref[...]
    xn = _rmsnorm(h, g_ref[...]).astype(BF16)
    q = (jnp.dot(xn, wq_ref[...], preferred_element_type=F32) * (X_HEAD_DIM ** -0.5)).astype(BF16)
    heads = []
    for hd in range(X_HEADS):
        cols = slice(hd * X_HEAD_DIM, (hd + 1) * X_HEAD_DIM)
        s = lax.dot_general(q[:, cols], k_ref[:, cols], (((1,), (1,)), ((), ())),
                            preferred_element_type=F32)
        e = jnp.exp(s - jnp.max(s, axis=-1, keepdims=True))
        pv = jnp.dot(e.astype(BF16), v_ref[:, cols], preferred_element_type=F32)
        heads.append(pv / jnp.sum(e, axis=-1, keepdims=True))
    o = jnp.concatenate(heads, axis=1).astype(BF16)
    o_ref[...] = h + jnp.dot(o, wo_ref[...], preferred_element_type=F32)


def _xattn(h, k, v, g, wq, wo, seq_len, mem_len, side):
    n_tok = h.shape[0]
    ts = TILE_ATT
    tiles_per_seq = seq_len // ts
    side_in, side_out, side_shape = _side_cast_specs(side, n_tok // ts)
    return pl.pallas_call(
        _xattn_kernel,
        grid=(n_tok // ts,),
        in_specs=[pl.BlockSpec((ts, D_MODEL), lambda i: (i, 0)),
                  _const_spec(g.shape), _const_spec(wq.shape),
                  pl.BlockSpec((mem_len, D_MODEL), lambda i: (i // tiles_per_seq, 0)),
                  pl.BlockSpec((mem_len, D_MODEL), lambda i: (i // tiles_per_seq, 0)),
                  _const_spec(wo.shape), side_in],
        out_specs=[pl.BlockSpec((ts, D_MODEL), lambda i: (i, 0)), side_out],
        out_shape=[jax.ShapeDtypeStruct((n_tok, D_MODEL), F32), side_shape],
        compiler_params=_params("arbitrary"),
        name="xattn",
    )(h, g, wq, k, v, wo, side)


def _ffn_chunks(d_ff):
    step = -(-d_ff // (3 * MXU_DIM)) * MXU_DIM
    return [(c0, min(c0 + step, d_ff)) for c0 in range(0, d_ff, step)]


def _ffn_kernel(h_ref, g_ref, wg_ref, wu_ref, wd_ref, side_ref, o_ref, side_out_ref, *, chunks):
    side_out_ref[...] = side_ref[...].astype(BF16)
    h = h_ref[...]
    xn = _rmsnorm(h, g_ref[...]).astype(BF16)
    acc = h
    for c0, c1 in chunks:
        gate = jnp.dot(xn, wg_ref[:, c0:c1], preferred_element_type=F32)
        up = jnp.dot(xn, wu_ref[:, c0:c1], preferred_element_type=F32)
        act = (jax.nn.silu(gate) * up).astype(BF16)
        acc = acc + jnp.dot(act, wd_ref[c0:c1, :], preferred_element_type=F32)
    o_ref[...] = acc


def _ffn(h, g, wg, wu, wd, side):
    n_tok = h.shape[0]
    ts = TILE_FFN
    kern = functools.partial(_ffn_kernel, chunks=_ffn_chunks(wg.shape[1]))
    side_in, side_out, side_shape = _side_cast_specs(side, n_tok // ts)
    return pl.pallas_call(
        kern,
        grid=(n_tok // ts,),
        in_specs=[pl.BlockSpec((ts, D_MODEL), lambda i: (i, 0)),
                  _const_spec(g.shape), _const_spec(wg.shape),
                  _const_spec(wu.shape), _const_spec(wd.shape), side_in],
        out_specs=[pl.BlockSpec((ts, D_MODEL), lambda i: (i, 0)), side_out],
        out_shape=[jax.ShapeDtypeStruct((n_tok, D_MODEL), F32), side_shape],
        compiler_params=_params("arbitrary"),
        name="ffn",
    )(h, g, wg, wu, wd, side)


META_IDX, META_RANK, META_W = 0, 2, 4


def _router_kernel(h_ref, g_ref, wr_ref, tri_ref, meta_ref, cnt_ref, run_ref):
    @pl.when(pl.program_id(0) == 0)
    def _():
        run_ref[...] = jnp.zeros_like(run_ref)

    hn = _rmsnorm(h_ref[...], g_ref[...])
    hn_hi = hn.astype(BF16)
    hn_lo = (hn - hn_hi.astype(F32)).astype(BF16)
    both = jnp.dot(hn_hi, wr_ref[...], preferred_element_type=F32)
    logits = (both[:, 0:LANES] + both[:, LANES:2 * LANES]
              + jnp.dot(hn_lo, wr_ref[:, 0:LANES], preferred_element_type=F32))
    lane = lax.broadcasted_iota(jnp.int32, logits.shape, 1).astype(F32)
    neg_inf = jnp.float32(-jnp.inf)
    l1 = jnp.where(lane < N_EXPERTS, logits, neg_inf)
    m1 = jnp.max(l1, axis=-1, keepdims=True)
    i1 = jnp.min(jnp.where(l1 == m1, lane, float(LANES)), axis=-1, keepdims=True)
    l2 = jnp.where(lane == i1, neg_inf, l1)
    m2 = jnp.max(l2, axis=-1, keepdims=True)
    i2 = jnp.min(jnp.where(l2 == m2, lane, float(LANES)), axis=-1, keepdims=True)
    e2 = jnp.exp(m2 - m1)
    w1 = 1.0 / (1.0 + e2)
    w2 = e2 / (1.0 + e2)
    sel1 = lane == i1
    sel2 = lane == i2
    onehot = jnp.where(sel1 | sel2, 1.0, 0.0)
    before = jnp.dot(tri_ref[...], onehot.astype(BF16), preferred_element_type=F32)
    rank = before + run_ref[0:1, :]
    r1 = jnp.sum(jnp.where(sel1, rank, 0.0), axis=-1, keepdims=True)
    r2 = jnp.sum(jnp.where(sel2, rank, 0.0), axis=-1, keepdims=True)
    total = run_ref[0:1, :] + jnp.sum(onehot, axis=0, keepdims=True)
    run_ref[0:1, :] = total
    cnt_ref[...] = jnp.broadcast_to(total, cnt_ref.shape)
    meta = jnp.zeros(logits.shape, F32)
    for col, val in ((META_IDX, i1), (META_IDX + 1, i2), (META_RANK, r1), (META_RANK + 1, r2),
                     (META_W, w1), (META_W + 1, w2)):
        meta = jnp.where(lane == col, val, meta)
    meta_ref[...] = meta


def _router(h, g, wr_pad, tri):
    n_tok = h.shape[0]
    ts = TILE_ROUTE
    return pl.pallas_call(
        _router_kernel,
        grid=(n_tok // ts,),
        in_specs=[pl.BlockSpec((ts, D_MODEL), lambda i: (i, 0)),
                  _const_spec(g.shape), _const_spec(wr_pad.shape), _const_spec(tri.shape)],
        out_specs=[pl.BlockSpec((ts, LANES), lambda i: (i, 0)),
                   pl.BlockSpec((SUBLANES, LANES), lambda i: (0, 0))],
        out_shape=[jax.ShapeDtypeStruct((n_tok, LANES), F32),
                   jax.ShapeDtypeStruct((SUBLANES, LANES), F32)],
        scratch_shapes=[pltpu.VMEM((SUBLANES, LANES), F32)],
        compiler_params=_params("arbitrary"),
        name="router",
    )(h, g, wr_pad, tri)


def _rows_to_tiles(x, dst):
    n = x.shape[0]
    for g in range(n // SUBLANES):
        for j in range(ROW_TILE):
            dst[pl.ds(g * SUBLANES * ROW_TILE + j, SUBLANES, stride=ROW_TILE), :] = (
                x[g * SUBLANES:(g + 1) * SUBLANES, j * LANES:(j + 1) * LANES])


def _tiles_to_rows(src, n):
    groups = []
    for g in range(n // SUBLANES):
        groups.append(jnp.concatenate(
            [src[pl.ds(g * SUBLANES * ROW_TILE + j, SUBLANES, stride=ROW_TILE), :]
             for j in range(ROW_TILE)], axis=1))
    return jnp.concatenate(groups, axis=0)


def _tile_rows(ref, row):
    return ref.at[pl.ds(pl.multiple_of(row * ROW_TILE, ROW_TILE), ROW_TILE), :]


def _wait_rows(buf, sem):
    pltpu.make_async_copy(buf, buf, sem).wait()


def _dispatch_kernel(pos0_ref, pos1_ref, fill_ref, h_ref, g_ref, xs_ref, hn_buf, zero_buf, sems,
                     fill_sem, *, ts, tm):
    step = pl.program_id(0)
    slot = lax.rem(step, 2)

    @pl.when(step == 0)
    def _():
        zero_buf[...] = jnp.zeros_like(zero_buf)
        for k in range(2 * N_EXPERTS):
            @pl.when(fill_ref[k] >= 0)
            def _():
                cp = pltpu.make_async_copy(
                    zero_buf, xs_ref.at[pl.ds(fill_ref[k] * (tm * ROW_TILE), tm * ROW_TILE), :],
                    fill_sem)
                cp.start()
                cp.wait()

    rows = hn_buf.at[slot]
    _rows_to_tiles(_rmsnorm(h_ref[...], g_ref[...]), rows)

    def issue(q, carry):
        for l in range(LANES):
            src = _tile_rows(rows, q * LANES + l)
            for k, table in enumerate((pos0_ref, pos1_ref)):
                pltpu.make_async_copy(src, _tile_rows(xs_ref, table[q, l]),
                                      sems.at[slot, k]).start(priority=k)
        return carry

    lax.fori_loop(0, ts // LANES, issue, 0)

    @pl.when(step > 0)
    def _():
        for k in range(2):
            _wait_rows(hn_buf.at[1 - slot], sems.at[1 - slot, k])

    @pl.when(step == pl.num_programs(0) - 1)
    def _():
        for k in range(2):
            _wait_rows(rows, sems.at[slot, k])


def _dispatch(h, g, pos_tiles, fill_tiles, n_rows, tm):
    n_tok = h.shape[0]
    n_steps = pos_tiles[0].shape[0]
    ts = n_tok // n_steps
    kern = functools.partial(_dispatch_kernel, ts=ts, tm=tm)
    pos_spec = pl.BlockSpec((None,) + pos_tiles[0].shape[1:], lambda i: (i, 0, 0),
                            memory_space=pltpu.SMEM)
    return pl.pallas_call(
        kern,
        grid=(n_steps,),
        in_specs=[pos_spec, pos_spec,
                  pl.BlockSpec(memory_space=pltpu.SMEM),
                  pl.BlockSpec((ts, D_MODEL), lambda i: (i, 0)),
                  _const_spec(g.shape)],
        out_specs=pl.BlockSpec(memory_space=pl.ANY),
        out_shape=jax.ShapeDtypeStruct((n_rows * ROW_TILE, LANES), F32),
        scratch_shapes=[pltpu.VMEM((2, ts * ROW_TILE, LANES), F32),
                        pltpu.VMEM((tm * ROW_TILE, LANES), F32),
                        pltpu.SemaphoreType.DMA((2, 2)),
                        pltpu.SemaphoreType.DMA(())],
        compiler_params=_params("arbitrary"),
        name="dispatch",
    )(*pos_tiles, fill_tiles, h, g)


def _expert_kernel(te_ref, na_ref, x_ref, wg_ref, wu_ref, wd_ref, o_ref, acc_ref, *, tm, n_f):
    del te_ref
    f = pl.program_id(1)
    active = pl.program_id(0) < na_ref[0]

    @pl.when(jnp.logical_and(jnp.logical_not(active), f == 0))
    def _():
        o_ref[...] = jnp.zeros_like(o_ref)

    @pl.when(active)
    def _():
        x = _tiles_to_rows(x_ref, tm).astype(BF16)
        part = None
        for c0 in range(0, FF_CHUNK, FF_SUBCHUNK):
            c1 = min(c0 + FF_SUBCHUNK, FF_CHUNK)
            gate = jnp.dot(x, wg_ref[:, c0:c1], preferred_element_type=F32)
            up = jnp.dot(x, wu_ref[:, c0:c1], preferred_element_type=F32)
            act = (jax.nn.silu(gate) * up).astype(BF16)
            down = jnp.dot(act, wd_ref[c0:c1, :], preferred_element_type=F32)
            part = down if part is None else part + down

        @pl.when(f == 0)
        def _():
            acc_ref[...] = part

        @pl.when(jnp.logical_and(f > 0, f < n_f - 1))
        def _():
            acc_ref[...] += part

        @pl.when(f == n_f - 1)
        def _():
            _rows_to_tiles(acc_ref[...] + part, o_ref)


def _experts(xs, wg, wu, wd, tile_expert, n_active, tm):
    d_exp = wg.shape[2]
    n_f = d_exp // FF_CHUNK
    assert n_f >= 2
    n_tiles = xs.shape[0] // (tm * ROW_TILE)

    def x_tile(i, na):
        return jnp.minimum(i, na[0] - 1)

    def chunk(i, f, na):
        return jnp.where(i < na[0], f, n_f - 1)

    grid_spec = pltpu.PrefetchScalarGridSpec(
        num_scalar_prefetch=2,
        grid=(n_tiles, n_f),
        in_specs=[
            pl.BlockSpec((tm * ROW_TILE, LANES), lambda i, f, te, na: (x_tile(i, na), 0)),
            pl.BlockSpec((None, D_MODEL, FF_CHUNK), lambda i, f, te, na: (te[i], 0, chunk(i, f, na))),
            pl.BlockSpec((None, D_MODEL, FF_CHUNK), lambda i, f, te, na: (te[i], 0, chunk(i, f, na))),
            pl.BlockSpec((None, FF_CHUNK, D_MODEL), lambda i, f, te, na: (te[i], chunk(i, f, na), 0)),
        ],
        out_specs=pl.BlockSpec((tm * ROW_TILE, LANES), lambda i, f, te, na: (i, 0)),
        scratch_shapes=[pltpu.VMEM((tm, D_MODEL), F32)],
    )
    return pl.pallas_call(
        functools.partial(_expert_kernel, tm=tm, n_f=n_f),
        grid_spec=grid_spec,
        out_shape=jax.ShapeDtypeStruct(xs.shape, F32),
        compiler_params=_params("arbitrary", "arbitrary"),
        name="experts",
    )(tile_expert, n_active, xs, wg, wu, wd)


def _combine_kernel(pos0_ref, pos1_ref, next_pos0_ref, next_pos1_ref, h_ref, meta_ref, g_ref, ys_ref,
                    o_ref, bufs, sems, *, ts):
    step = pl.program_id(0)
    slot = lax.rem(step, 2)

    def gather(tables, dst_slot):
        def issue(q, carry):
            for l in range(LANES):
                for k, table in enumerate(tables):
                    dst = _tile_rows(bufs.at[dst_slot, k], q * LANES + l)
                    pltpu.make_async_copy(_tile_rows(ys_ref, table[q, l]), dst,
                                          sems.at[dst_slot, k]).start(priority=k)
            return carry

        lax.fori_loop(0, ts // LANES, issue, 0)

    @pl.when(step == 0)
    def _():
        gather((pos0_ref, pos1_ref), slot)

    @pl.when(step + 1 < pl.num_programs(0))
    def _():
        gather((next_pos0_ref, next_pos1_ref), 1 - slot)

    for k in range(2):
        _wait_rows(bufs.at[slot, k], sems.at[slot, k])
    meta = meta_ref[...]
    w1 = meta[:, META_W:META_W + 1]
    w2 = meta[:, META_W + 1:META_W + 2]
    y = h_ref[...] + (w1 * _tiles_to_rows(bufs.at[slot, 0], ts)
                      + w2 * _tiles_to_rows(bufs.at[slot, 1], ts))
    o_ref[...] = _rmsnorm(y, g_ref[...])


def _combine(h, meta, g, ys, pos_tiles):
    n_tok = h.shape[0]
    n_steps = pos_tiles[0].shape[0]
    ts = n_tok // n_steps
    kern = functools.partial(_combine_kernel, ts=ts)
    pos_block = (None,) + pos_tiles[0].shape[1:]
    pos_spec = pl.BlockSpec(pos_block, lambda i: (i, 0, 0), memory_space=pltpu.SMEM)
    next_spec = pl.BlockSpec(pos_block, lambda i: (jnp.minimum(i + 1, n_steps - 1), 0, 0),
                             memory_space=pltpu.SMEM)
    return pl.pallas_call(
        kern,
        grid=(n_steps,),
        in_specs=[pos_spec, pos_spec, next_spec, next_spec,
                  pl.BlockSpec((ts, D_MODEL), lambda i: (i, 0)),
                  pl.BlockSpec((ts, LANES), lambda i: (i, 0)),
                  _const_spec(g.shape),
                  pl.BlockSpec(memory_space=pl.ANY)],
        out_specs=pl.BlockSpec((ts, D_MODEL), lambda i: (i, 0)),
        out_shape=jax.ShapeDtypeStruct((n_tok, D_MODEL), F32),
        scratch_shapes=[pltpu.VMEM((2, 2, ts * ROW_TILE, LANES), F32),
                        pltpu.SemaphoreType.DMA((2, 2))],
        compiler_params=_params("arbitrary"),
        name="combine",
    )(*pos_tiles, *pos_tiles, h, meta, g, ys)


def _pos_tiles(pos, ts):
    return [p.reshape(p.shape[0] // ts, ts // LANES, LANES) for p in pos]


def _moe(h, norm_g, wr_pad, tri, wg, wu, wd, final_g):
    n_tok = h.shape[0]
    tm = TILE_EXPERT
    meta, cnt = _router(h, norm_g, wr_pad, tri)

    counts = cnt[0, :N_EXPERTS].astype(jnp.int32)
    tiles = (counts + (tm - 1)) // tm
    tile_end = jnp.cumsum(tiles)
    tile_start = tile_end - tiles
    n_tiles = (2 * n_tok) // tm + N_EXPERTS
    n_rows = n_tiles * tm

    def slots(k):
        idx = meta[:, META_IDX + k].astype(jnp.int32)
        rank = meta[:, META_RANK + k].astype(jnp.int32)
        first_tile = jnp.sum(jnp.where(idx[:, None] == jnp.arange(N_EXPERTS)[None, :],
                                       tile_start[None, :], 0), axis=1)
        return first_tile * tm + rank

    pos = [slots(0), slots(1)]
    n_active = tile_end[-1:].astype(jnp.int32)
    tail_tiles = jnp.where(tiles > 0, tile_end - 1, -1)
    slack_tiles = n_active[0] + jnp.arange(N_EXPERTS)
    slack_tiles = jnp.where(slack_tiles < n_tiles, slack_tiles, -1)
    fill_tiles = jnp.concatenate([tail_tiles, slack_tiles]).astype(jnp.int32)
    t = jnp.minimum(jnp.arange(n_tiles, dtype=jnp.int32), n_active[0] - 1)
    tile_expert = jnp.sum(t[:, None] >= tile_end[None, :], axis=1).astype(jnp.int32)

    xs = _dispatch(h, norm_g, _pos_tiles(pos, TILE_ROUTE), fill_tiles, n_rows, tm)
    ys = _experts(xs, wg, wu, wd, tile_expert, n_active, tm)
    return _combine(h, meta, final_g, ys, _pos_tiles(pos, TILE_COMBINE))


def _pad_last(a, width):
    return jnp.pad(a, [(0, 0)] * (a.ndim - 1) + [(0, width - a.shape[-1])])


def _row(a):
    return a.reshape(1, -1)


def _router_weight(w):
    w_hi = w.astype(BF16)
    w_lo = (w - w_hi.astype(F32)).astype(BF16)
    return jnp.concatenate([_pad_last(w_hi, LANES), _pad_last(w_lo, LANES)], axis=1)


def _mixer_params(norm_g, w_in, conv_a_w, conv_a_b, ln_a_g, ln_a_b, conv_b_w, ln_c_g, ln_c_b,
                  gmlp_ws, gmlp_b, mix_out_g, w_mix_out):
    widths = [A_WIDTH, A_WIDTH, B_WIDTH, B_WIDTH, B_WIDTH, C_WIDTH, C_WIDTH]
    bounds = [0]
    for w in widths:
        bounds.append(bounds[-1] + w)
    w_in_p = jnp.concatenate(
        [_pad_last(w_in[:, bounds[s]:bounds[s + 1]], SEG) for s in range(N_IN_SEG)], axis=1)
    out_bounds = [0, A_WIDTH, A_WIDTH + B_WIDTH, A_WIDTH + B_WIDTH + C_WIDTH]
    out_g = jnp.concatenate(
        [_pad_last(mix_out_g[out_bounds[s]:out_bounds[s + 1]], SEG) for s in range(3)])
    w_out_p = jnp.concatenate(
        [jnp.pad(w_mix_out[out_bounds[s]:out_bounds[s + 1]],
                 [(0, SEG - (out_bounds[s + 1] - out_bounds[s])), (0, 0)]) for s in range(3)], axis=0)
    gbias = _pad_last(jnp.repeat(gmlp_b.T, HEAD_GROUP, axis=1), SEG)
    group = jnp.arange(MXU_DIM) // HEAD_GROUP
    gsum = ((group[:, None] == group[None, :]) * (1.0 / HEAD_GROUP)).astype(BF16)
    return {
        "norm_g": _row(norm_g), "w_in": w_in_p.astype(BF16),
        "conv_a_w": jnp.repeat(conv_a_w, SUBLANES, axis=0),
        "conv_a_b": jnp.broadcast_to(conv_a_b, (SUBLANES, A_WIDTH)),
        "ln_a_g": _row(ln_a_g), "ln_a_b": _row(ln_a_b),
        "conv_b_w": _pad_last(conv_b_w, SEG),
        "ln_c_g": _row(_pad_last(ln_c_g, SEG)), "ln_c_b": _row(_pad_last(ln_c_b, SEG)),
        "gmlp_ws": gmlp_ws.astype(BF16), "gmlp_bias": gbias,
        "out_g": _row(out_g), "w_out": w_out_p.astype(BF16), "gsum": gsum,
    }


def kernel(x, mem, norm_mix_g, w_in, conv_a_w, conv_a_b, ln_a_g, ln_a_b, conv_b_w, ln_c_g, ln_c_b,
           gmlp_ws, gmlp_b, mix_out_g, w_mix_out, norm_x_g, norm_mem_g, w_xq, w_xkv, w_xo,
           norm_ffn_g, ffn_w_gate, ffn_w_up, ffn_w_down, moe_router, moe_w_gate, moe_w_up,
           moe_w_down, norm_final_g):
    bsz, seq_len, _ = x.shape
    mem_len = mem.shape[1]
    depth = w_in.shape[0]
    assert depth == 2 and ffn_w_gate.shape[0] == 1 and moe_router.shape[0] == 1
    assert seq_len % TILE_MIX == 0 and seq_len % TILE_ATT == 0

    h = x.reshape(bsz * seq_len, D_MODEL)
    mem2d = mem.reshape(bsz * mem_len, D_MODEL)
    tri = (jnp.arange(TILE_ROUTE)[:, None] > jnp.arange(TILE_ROUTE)[None, :]).astype(BF16)
    n_exp, _, d_exp = moe_w_gate.shape[1:]
    sides = [moe_w_gate[0].reshape(n_exp * D_MODEL, d_exp), moe_w_up[0].reshape(n_exp * D_MODEL, d_exp),
             moe_w_down[0].reshape(n_exp * d_exp, D_MODEL)]
    casted = []

    def attend(h, layer):
        k, v = _kv_proj(mem2d, _row(norm_mem_g[layer]), w_xkv[layer].astype(BF16))
        h, done = _xattn(h, k, v, _row(norm_x_g[layer]), w_xq[layer].astype(BF16),
                         w_xo[layer].astype(BF16), seq_len, mem_len, sides[len(casted)])
        casted.append(done)
        return h

    def mix(h, layer):
        mp = _mixer_params(norm_mix_g[layer], w_in[layer], conv_a_w[layer], conv_a_b[layer],
                           ln_a_g[layer], ln_a_b[layer], conv_b_w[layer], ln_c_g[layer],
                           ln_c_b[layer], gmlp_ws[layer], gmlp_b[layer], mix_out_g[layer],
                           w_mix_out[layer])
        return _mixer(h, mp, seq_len)

    h = attend(mix(h, 0), 0)
    h, done = _ffn(h, _row(norm_ffn_g[0]), ffn_w_gate[0].astype(BF16), ffn_w_up[0].astype(BF16),
                   ffn_w_down[0].astype(BF16), sides[len(casted)])
    casted.append(done)
    h = attend(mix(h, 1), 1)
    out = _moe(h, _row(norm_ffn_g[1]), _router_weight(moe_router[0]), tri,
               casted[0].reshape(n_exp, D_MODEL, d_exp), casted[1].reshape(n_exp, D_MODEL, d_exp),
               casted[2].reshape(n_exp, d_exp, D_MODEL), _row(norm_final_g))
    return out.reshape(bsz, seq_len, D_MODEL)
```

```python
import functools

import jax
import jax.numpy as jnp
from jax import lax
from jax.experimental import pallas as pl
from jax.experimental.pallas import tpu as pltpu

F32 = jnp.float32
BF16 = jnp.bfloat16

D_MODEL = 1024
EPS = 1e-6
CHUNK = 64
HEAD_GROUP = 64
A_WIDTH, B_WIDTH, C_WIDTH = 384, 320, 320
A_KERNEL, B_KERNEL = 31, 3
GMLP_BLOCK = 128
C_GROUPS = 5
X_HEADS = 4
X_HEAD_DIM = D_MODEL // X_HEADS
N_EXPERTS = 8

LANES = 128
SUBLANES = 8
MXU_DIM = 256
VMEM_LIMIT_BYTES = 56 * 1024 * 1024

SEG = 384
N_IN_SEG = 7
MIX_PAD = 3 * SEG
HALO_A = 32
HALO_B = SUBLANES
CONV_ROWS = 32
EDGE_PARTS = 2
SHIFT_EXTRA = HALO_A - SUBLANES

TILE_MIX = 512
TILE_ATT = 512
TILE_FFN = 512
TILE_ROUTE = 512
TILE_COMBINE = 256
TILE_EXPERT = 512
FF_CHUNK = 1792
FF_SUBCHUNK = 512
KV_ROWS = 1024
ROW_TILE = 8


def _rmsnorm(x, g):
    ms = jnp.mean(x * x, axis=-1, keepdims=True)
    return x * lax.rsqrt(ms + EPS) * g


def _const_spec(shape):
    zeros = (0,) * len(shape)
    return pl.BlockSpec(shape, lambda *_: zeros, pipeline_mode=pl.Buffered(1))


def _params(*semantics):
    return pltpu.CompilerParams(dimension_semantics=semantics,
                                vmem_limit_bytes=VMEM_LIMIT_BYTES)


def _mixer_kernel(h_ref, ng_ref, win_ref, caw_ref, cab_ref, lag_ref, lab_ref, cbw_ref,
                  lcg_ref, lcb_ref, ws_ref, gbias_ref, og_ref, wout_ref, gsum_ref,
                  o_ref, zbuf, abuf, bbuf, shifted, ybuf, *, ts, tiles_per_seq):
    seq_tile = lax.rem(pl.program_id(0), tiles_per_seq)

    @pl.when(seq_tile == 0)
    def _():
        abuf[0:HALO_A, :] = jnp.zeros((HALO_A, SEG), F32)
        bbuf[0:HALO_B, :] = jnp.zeros((HALO_B, SEG), F32)

    @pl.when(seq_tile > 0)
    def _():
        abuf[0:HALO_A, :] = abuf[ts:ts + HALO_A, :]
        bbuf[0:HALO_B, :] = bbuf[ts:ts + HALO_B, :]

    h = h_ref[...]
    xn = _rmsnorm(h, ng_ref[...]).astype(BF16)

    part = ts // EDGE_PARTS
    for r0 in range(0, ts, part):
        z_a = jnp.dot(xn[r0:r0 + part, :], win_ref[:, 0:2 * SEG], preferred_element_type=F32)
        abuf[HALO_A + r0:HALO_A + r0 + part, :] = z_a[:, 0:SEG] * jax.nn.sigmoid(z_a[:, SEG:2 * SEG])
        lo = 0 if r0 == 0 else r0 + SHIFT_EXTRA
        hi = r0 + part + SHIFT_EXTRA
        for phase in range(1, SUBLANES):
            shifted[phase - 1, lo:hi, :] = abuf[lo + phase:hi + phase, :]

    zbuf[...] = jnp.dot(xn, win_ref[:, 2 * SEG:N_IN_SEG * SEG], preferred_element_type=F32)

    for r in range(0, ts, CONV_ROWS):
        acc = jnp.concatenate([cab_ref[...]] * (CONV_ROWS // SUBLANES), axis=0)
        for k in range(A_KERNEL):
            groups, phase = divmod(HALO_A - A_KERNEL + 1 + k, SUBLANES)
            start = r + groups * SUBLANES
            if phase == 0:
                window = abuf[start:start + CONV_ROWS, :]
            else:
                window = shifted[phase - 1, start:start + CONV_ROWS, :]
            tap = caw_ref[k * SUBLANES:(k + 1) * SUBLANES, :]
            acc = acc + jnp.concatenate([tap] * (CONV_ROWS // SUBLANES), axis=0) * window
        mu = jnp.mean(acc, axis=-1, keepdims=True)
        xc = acc - mu
        var = jnp.mean(xc * xc, axis=-1, keepdims=True)
        ybuf[r:r + CONV_ROWS, 0:SEG] = jax.nn.silu(
            xc * lax.rsqrt(var + EPS) * lag_ref[...] + lab_ref[...])

    bbuf[HALO_B:HALO_B + ts, :] = zbuf[:, 1 * SEG:2 * SEG] * zbuf[:, 2 * SEG:3 * SEG]
    accb = jnp.zeros((ts, SEG), F32)
    for k in range(B_KERNEL):
        first = HALO_B - B_KERNEL + 1 + k
        accb = accb + cbw_ref[k:k + 1, :] * bbuf[first:first + ts, :]
    ybuf[:, 1 * SEG:2 * SEG] = zbuf[:, 0:SEG] * accb

    c_valid = lax.broadcasted_iota(jnp.int32, (1, SEG), 1) < C_WIDTH
    low_group = lax.broadcasted_iota(jnp.int32, (GMLP_BLOCK, LANES), 1) < HEAD_GROUP
    row_chunk = lax.broadcasted_iota(jnp.int32, (GMLP_BLOCK, GMLP_BLOCK), 0) // CHUNK
    col_chunk = lax.broadcasted_iota(jnp.int32, (GMLP_BLOCK, GMLP_BLOCK), 1) // CHUNK
    chunk_causal = row_chunk >= col_chunk
    ws = [jnp.where(chunk_causal, ws_ref[g], jnp.zeros((), BF16)) for g in range(C_GROUPS)]
    for r in range(0, ts, GMLP_BLOCK):
        rows = slice(r, r + GMLP_BLOCK)
        c_u = jax.nn.gelu(zbuf[rows, 3 * SEG:4 * SEG])
        c_v = jax.nn.gelu(zbuf[rows, 4 * SEG:5 * SEG])
        mu = jnp.sum(c_v, axis=-1, keepdims=True) * (1.0 / C_WIDTH)
        xc = jnp.where(c_valid, c_v - mu, 0.0)
        var = jnp.sum(xc * xc, axis=-1, keepdims=True) * (1.0 / C_WIDTH)
        v = (xc * lax.rsqrt(var + EPS) * lcg_ref[...] + lcb_ref[...]).astype(BF16)
        cols = []
        for j in range(SEG // LANES):
            vj = v[:, j * LANES:(j + 1) * LANES]
            col = jnp.dot(ws[2 * j], vj, preferred_element_type=F32)
            if 2 * j + 1 < C_GROUPS:
                col = jnp.where(low_group, col, jnp.dot(ws[2 * j + 1], vj, preferred_element_type=F32))
            cols.append(col)
        mixed = jnp.concatenate(cols, axis=1) + gbias_ref[...]
        ybuf[rows, 2 * SEG:3 * SEG] = c_u * mixed

    gsum = gsum_ref[...]
    for r0 in range(0, ts, part):
        y = ybuf[r0:r0 + part, :]
        y2 = (y * y).astype(BF16)
        sums = []
        for c0 in range(0, MIX_PAD, MXU_DIM):
            width = min(MXU_DIM, MIX_PAD - c0)
            sums.append(jnp.dot(y2[:, c0:c0 + width], gsum[0:width, 0:width],
                                preferred_element_type=F32))
        ms = jnp.concatenate(sums, axis=1)
        yn = (y * lax.rsqrt(ms + EPS) * og_ref[...]).astype(BF16)
        o_ref[r0:r0 + part, :] = h[r0:r0 + part, :] + jnp.dot(yn, wout_ref[...], preferred_element_type=F32)


def _mixer(h, p, seq_len):
    n_tok = h.shape[0]
    ts = TILE_MIX
    tiles_per_seq = seq_len // ts
    kern = functools.partial(_mixer_kernel, ts=ts, tiles_per_seq=tiles_per_seq)
    consts = [p["norm_g"], p["w_in"], p["conv_a_w"], p["conv_a_b"], p["ln_a_g"], p["ln_a_b"],
              p["conv_b_w"], p["ln_c_g"], p["ln_c_b"], p["gmlp_ws"], p["gmlp_bias"],
              p["out_g"], p["w_out"], p["gsum"]]
    return pl.pallas_call(
        kern,
        grid=(n_tok // ts,),
        in_specs=[pl.BlockSpec((ts, D_MODEL), lambda i: (i, 0))]
                 + [_const_spec(c.shape) for c in consts],
        out_specs=pl.BlockSpec((ts, D_MODEL), lambda i: (i, 0)),
        out_shape=jax.ShapeDtypeStruct((n_tok, D_MODEL), F32),
        scratch_shapes=[
            pltpu.VMEM((ts, (N_IN_SEG - 2) * SEG), F32),
            pltpu.VMEM((HALO_A + ts, SEG), F32),
            pltpu.VMEM((HALO_B + ts, SEG), F32),
            pltpu.VMEM((SUBLANES - 1, ts + SHIFT_EXTRA, SEG), F32),
            pltpu.VMEM((ts, MIX_PAD), F32),
        ],
        compiler_params=_params("arbitrary"),
        name="mixer",
    )(h, *consts)


def _kv_kernel(mem_ref, g_ref, wkv_ref, k_ref, v_ref):
    mn = _rmsnorm(mem_ref[...], g_ref[...]).astype(BF16)
    kv = jnp.dot(mn, wkv_ref[...], preferred_element_type=F32)
    k_ref[...] = kv[:, 0:D_MODEL].astype(BF16)
    v_ref[...] = kv[:, D_MODEL:2 * D_MODEL].astype(BF16)


def _kv_proj(mem2d, g, wkv):
    n = mem2d.shape[0]
    rows = min(KV_ROWS, n)
    return pl.pallas_call(
        _kv_kernel,
        grid=(n // rows,),
        in_specs=[pl.BlockSpec((rows, D_MODEL), lambda i: (i, 0)),
                  _const_spec(g.shape), _const_spec(wkv.shape)],
        out_specs=[pl.BlockSpec((rows, D_MODEL), lambda i: (i, 0))] * 2,
        out_shape=[jax.ShapeDtypeStruct((n, D_MODEL), BF16)] * 2,
        compiler_params=_params("arbitrary"),
        name="kv_proj",
    )(mem2d, g, wkv)


def _side_cast_specs(side, n_steps):
    rows = side.shape[0] // n_steps
    assert side.shape[0] % n_steps == 0 and rows % (2 * SUBLANES) == 0
    spec = pl.BlockSpec((rows, side.shape[1]), lambda i: (i, 0))
    return spec, spec, jax.ShapeDtypeStruct(side.shape, BF16)


def _xattn_kernel(h_ref, g_ref, wq_ref, k_ref, v_ref, wo_ref, side_ref, o_ref, side_out_ref):
    side_out_ref[...] = side_ref[...].astype(BF16)
    h = h_ref[...]
    xn = _rmsnorm(h, g_ref[...]).astype(BF16)
    q = (jnp.dot(xn, wq_ref[...], preferred_element_type=F32) * (X_HEAD_DIM ** -0.5)).astype(BF16)
    heads = []
    for hd in range(X_HEADS):
        cols = slice(hd * X_HEAD_DIM, (hd + 1) * X_HEAD_DIM)
        s = lax.dot_general(q[:, cols], k_ref[:, cols], (((1,), (1,)), ((), ())),
                            preferred_element_type=F32)
        e = jnp.exp(s - jnp.max(s, axis=-1, keepdims=True))
        pv = jnp.dot(e.astype(BF16), v_ref[:, cols], preferred_element_type=F32)
        heads.append(pv / jnp.sum(e, axis=-1, keepdims=True))
    o = jnp.concatenate(heads, axis=1).astype(BF16)
    o_ref[...] = h + jnp.dot(o, wo_ref[...], preferred_element_type=F32)


def _xattn(h, k, v, g, wq, wo, seq_len, mem_len, side):
    n_tok = h.shape[0]
    ts = TILE_ATT
    tiles_per_seq = seq_len // ts
    side_in, side_out, side_shape = _side_cast_specs(side, n_tok // ts)
    return pl.pallas_call(
        _xattn_kernel,
        grid=(n_tok // ts,),
        in_specs=[pl.BlockSpec((ts, D_MODEL), lambda i: (i, 0)),
                  _const_spec(g.shape), _const_spec(wq.shape),
                  pl.BlockSpec((mem_len, D_MODEL), lambda i: (i // tiles_per_seq, 0)),
                  pl.BlockSpec((mem_len, D_MODEL), lambda i: (i // tiles_per_seq, 0)),
                  _const_spec(wo.shape), side_in],
        out_specs=[pl.BlockSpec((ts, D_MODEL), lambda i: (i, 0)), side_out],
        out_shape=[jax.ShapeDtypeStruct((n_tok, D_MODEL), F32), side_shape],
        compiler_params=_params("arbitrary"),
        name="xattn",
    )(h, g, wq, k, v, wo, side)


def _ffn_chunks(d_ff):
    step = -(-d_ff // (3 * MXU_DIM)) * MXU_DIM
    return [(c0, min(c0 + step, d_ff)) for c0 in range(0, d_ff, step)]


def _ffn_kernel(h_ref, g_ref, wg_ref, wu_ref, wd_ref, side_ref, o_ref, side_out_ref, *, chunks):
    side_out_ref[...] = side_ref[...].astype(BF16)
    h = h_ref[...]
    xn = _rmsnorm(h, g_ref[...]).astype(BF16)
    acc = h
    for c0, c1 in chunks:
        gate = jnp.dot(xn, wg_ref[:, c0:c1], preferred_element_type=F32)
        up = jnp.dot(xn, wu_ref[:, c0:c1], preferred_element_type=F32)
        act = (jax.nn.silu(gate) * up).astype(BF16)
        acc = acc + jnp.dot(act, wd_ref[c0:c1, :], preferred_element_type=F32)
    o_ref[...] = acc


def _ffn(h, g, wg, wu, wd, side):
    n_tok = h.shape[0]
    ts = TILE_FFN
    kern = functools.partial(_ffn_kernel, chunks=_ffn_chunks(wg.shape[1]))
    side_in, side_out, side_shape = _side_cast_specs(side, n_tok // ts)
    return pl.pallas_call(
        kern,
        grid=(n_tok // ts,),
        in_specs=[pl.BlockSpec((ts, D_MODEL), lambda i: (i, 0)),
                  _const_spec(g.shape), _const_spec(wg.shape),
                  _const_spec(wu.shape), _const_spec(wd.shape), side_in],
        out_specs=[pl.BlockSpec((ts, D_MODEL), lambda i: (i, 0)), side_out],
        out_shape=[jax.ShapeDtypeStruct((n_tok, D_MODEL), F32), side_shape],
        compiler_params=_params("arbitrary"),
        name="ffn",
    )(h, g, wg, wu, wd, side)


META_IDX, META_RANK, META_W = 0, 2, 4


def _router_kernel(h_ref, g_ref, wr_ref, tri_ref, meta_ref, fields_ref, cnt_ref, run_ref):
    @pl.when(pl.program_id(0) == 0)
    def _():
        run_ref[...] = jnp.zeros_like(run_ref)

    hn = _rmsnorm(h_ref[...], g_ref[...])
    hn_hi = hn.astype(BF16)
    hn_lo = (hn - hn_hi.astype(F32)).astype(BF16)
    both = jnp.dot(hn_hi, wr_ref[...], preferred_element_type=F32)
    logits = (both[:, 0:LANES] + both[:, LANES:2 * LANES]
              + jnp.dot(hn_lo, wr_ref[:, 0:LANES], preferred_element_type=F32))
    lane = lax.broadcasted_iota(jnp.int32, logits.shape, 1).astype(F32)
    neg_inf = jnp.float32(-jnp.inf)
    l1 = jnp.where(lane < N_EXPERTS, logits, neg_inf)
    m1 = jnp.max(l1, axis=-1, keepdims=True)
    i1 = jnp.min(jnp.where(l1 == m1, lane, float(LANES)), axis=-1, keepdims=True)
    l2 = jnp.where(lane == i1, neg_inf, l1)
    m2 = jnp.max(l2, axis=-1, keepdims=True)
    i2 = jnp.min(jnp.where(l2 == m2, lane, float(LANES)), axis=-1, keepdims=True)
    e2 = jnp.exp(m2 - m1)
    w1 = 1.0 / (1.0 + e2)
    w2 = e2 / (1.0 + e2)
    sel1 = lane == i1
    sel2 = lane == i2
    onehot = jnp.where(sel1 | sel2, 1.0, 0.0)
    before = jnp.dot(tri_ref[...], onehot.astype(BF16), preferred_element_type=F32)
    rank = before + run_ref[0:1, :]
    r1 = jnp.sum(jnp.where(sel1, rank, 0.0), axis=-1, keepdims=True)
    r2 = jnp.sum(jnp.where(sel2, rank, 0.0), axis=-1, keepdims=True)
    total = run_ref[0:1, :] + jnp.sum(onehot, axis=0, keepdims=True)
    run_ref[0:1, :] = total
    cnt_ref[...] = jnp.broadcast_to(total, cnt_ref.shape)
    meta = jnp.zeros(logits.shape, F32)
    for col, val in ((META_IDX, i1), (META_IDX + 1, i2), (META_RANK, r1), (META_RANK + 1, r2),
                     (META_W, w1), (META_W + 1, w2)):
        meta = jnp.where(lane == col, val, meta)
    meta_ref[...] = meta
    fields_ref[...] = meta.T[0:SUBLANES, :]


def _router(h, g, wr_pad, tri):
    n_tok = h.shape[0]
    ts = TILE_ROUTE
    return pl.pallas_call(
        _router_kernel,
        grid=(n_tok // ts,),
        in_specs=[pl.BlockSpec((ts, D_MODEL), lambda i: (i, 0)),
                  _const_spec(g.shape), _const_spec(wr_pad.shape), _const_spec(tri.shape)],
        out_specs=[pl.BlockSpec((ts, LANES), lambda i: (i, 0)),
                   pl.BlockSpec((SUBLANES, ts), lambda i: (0, i)),
                   pl.BlockSpec((SUBLANES, LANES), lambda i: (0, 0))],
        out_shape=[jax.ShapeDtypeStruct((n_tok, LANES), F32),
                   jax.ShapeDtypeStruct((SUBLANES, n_tok), F32),
                   jax.ShapeDtypeStruct((SUBLANES, LANES), F32)],
        scratch_shapes=[pltpu.VMEM((SUBLANES, LANES), F32)],
        compiler_params=_params("arbitrary"),
        name="router",
    )(h, g, wr_pad, tri)


def _rows_to_tiles(x, dst):
    n = x.shape[0]
    for g in range(n // SUBLANES):
        for j in range(ROW_TILE):
            dst[pl.ds(g * SUBLANES * ROW_TILE + j, SUBLANES, stride=ROW_TILE), :] = (
                x[g * SUBLANES:(g + 1) * SUBLANES, j * LANES:(j + 1) * LANES])


def _tiles_to_rows(src, n):
    groups = []
    for g in range(n // SUBLANES):
        groups.append(jnp.concatenate(
            [src[pl.ds(g * SUBLANES * ROW_TILE + j, SUBLANES, stride=ROW_TILE), :]
             for j in range(ROW_TILE)], axis=1))
    return jnp.concatenate(groups, axis=0)


def _tile_rows(ref, row):
    return ref.at[pl.ds(pl.multiple_of(row * ROW_TILE, ROW_TILE), ROW_TILE), :]


def _wait_rows(buf, sem):
    pltpu.make_async_copy(buf, buf, sem).wait()


def _dispatch_kernel(pos0_ref, pos1_ref, fill_ref, h_ref, g_ref, xs_ref, hn_buf, zero_buf, sems,
                     fill_sem, *, ts, tm):
    step = pl.program_id(0)
    slot = lax.rem(step, 2)

    @pl.when(step == 0)
    def _():
        zero_buf[...] = jnp.zeros_like(zero_buf)
        for k in range(2 * N_EXPERTS):
            @pl.when(fill_ref[k] >= 0)
            def _():
                cp = pltpu.make_async_copy(
                    zero_buf, xs_ref.at[pl.ds(fill_ref[k] * (tm * ROW_TILE), tm * ROW_TILE), :],
                    fill_sem)
                cp.start()
                cp.wait()

    rows = hn_buf.at[slot]
    _rows_to_tiles(_rmsnorm(h_ref[...], g_ref[...]), rows)

    def issue(q, carry):
        for l in range(LANES):
            src = _tile_rows(rows, q * LANES + l)
            for k, table in enumerate((pos0_ref, pos1_ref)):
                pltpu.make_async_copy(src, _tile_rows(xs_ref, table[q, l]),
                                      sems.at[slot, k]).start(priority=k)
        return carry

    lax.fori_loop(0, ts // LANES, issue, 0)

    @pl.when(step > 0)
    def _():
        for k in range(2):
            _wait_rows(hn_buf.at[1 - slot], sems.at[1 - slot, k])

    @pl.when(step == pl.num_programs(0) - 1)
    def _():
        for k in range(2):
            _wait_rows(rows, sems.at[slot, k])


def _dispatch(h, g, pos_tiles, fill_tiles, n_rows, tm):
    n_tok = h.shape[0]
    n_steps = pos_tiles[0].shape[0]
    ts = n_tok // n_steps
    kern = functools.partial(_dispatch_kernel, ts=ts, tm=tm)
    pos_spec = pl.BlockSpec((None,) + pos_tiles[0].shape[1:], lambda i: (i, 0, 0),
                            memory_space=pltpu.SMEM)
    return pl.pallas_call(
        kern,
        grid=(n_steps,),
        in_specs=[pos_spec, pos_spec,
                  pl.BlockSpec(memory_space=pltpu.SMEM),
                  pl.BlockSpec((ts, D_MODEL), lambda i: (i, 0)),
                  _const_spec(g.shape)],
        out_specs=pl.BlockSpec(memory_space=pl.ANY),
        out_shape=jax.ShapeDtypeStruct((n_rows * ROW_TILE, LANES), F32),
        scratch_shapes=[pltpu.VMEM((2, ts * ROW_TILE, LANES), F32),
                        pltpu.VMEM((tm * ROW_TILE, LANES), F32),
                        pltpu.SemaphoreType.DMA((2, 2)),
                        pltpu.SemaphoreType.DMA(())],
        compiler_params=_params("arbitrary"),
        name="dispatch",
    )(*pos_tiles, fill_tiles, h, g)


def _expert_kernel(te_ref, na_ref, x_ref, wg_ref, wu_ref, wd_ref, o_ref, acc_ref, *, tm, n_f):
    del te_ref
    f = pl.program_id(1)
    active = pl.program_id(0) < na_ref[0]

    @pl.when(jnp.logical_and(jnp.logical_not(active), f == 0))
    def _():
        o_ref[...] = jnp.zeros_like(o_ref)

    def swiglu_chunk(first, last):
        x = _tiles_to_rows(x_ref, tm).astype(BF16)
        part = None if first else acc_ref[...]
        for c0 in range(0, FF_CHUNK, FF_SUBCHUNK):
            c1 = min(c0 + FF_SUBCHUNK, FF_CHUNK)
            gate = jnp.dot(x, wg_ref[:, c0:c1], preferred_element_type=F32)
            up = jnp.dot(x, wu_ref[:, c0:c1], preferred_element_type=F32)
            act = (jax.nn.silu(gate) * up).astype(BF16)
            down = jnp.dot(act, wd_ref[c0:c1, :], preferred_element_type=F32)
            part = down if part is None else part + down
        if last:
            _rows_to_tiles(part, o_ref)
        else:
            acc_ref[...] = part

    pl.when(jnp.logical_and(active, f == 0))(functools.partial(swiglu_chunk, True, False))
    pl.when(jnp.logical_and(active, f == n_f - 1))(functools.partial(swiglu_chunk, False, True))
    if n_f > 2:
        pl.when(jnp.logical_and(active, jnp.logical_and(f > 0, f < n_f - 1)))(
            functools.partial(swiglu_chunk, False, False))


def _experts(xs, wg, wu, wd, tile_expert, n_active, tm):
    d_exp = wg.shape[2]
    n_f = d_exp // FF_CHUNK
    assert n_f >= 2
    n_tiles = xs.shape[0] // (tm * ROW_TILE)

    def x_tile(i, na):
        return jnp.minimum(i, na[0] - 1)

    def chunk(i, f, na):
        return jnp.where(i < na[0], f, n_f - 1)

    grid_spec = pltpu.PrefetchScalarGridSpec(
        num_scalar_prefetch=2,
        grid=(n_tiles, n_f),
        in_specs=[
            pl.BlockSpec((tm * ROW_TILE, LANES), lambda i, f, te, na: (x_tile(i, na), 0)),
            pl.BlockSpec((None, D_MODEL, FF_CHUNK), lambda i, f, te, na: (te[i], 0, chunk(i, f, na))),
            pl.BlockSpec((None, D_MODEL, FF_CHUNK), lambda i, f, te, na: (te[i], 0, chunk(i, f, na))),
            pl.BlockSpec((None, FF_CHUNK, D_MODEL), lambda i, f, te, na: (te[i], chunk(i, f, na), 0)),
        ],
        out_specs=pl.BlockSpec((tm * ROW_TILE, LANES), lambda i, f, te, na: (i, 0)),
        scratch_shapes=[pltpu.VMEM((tm, D_MODEL), F32)],
    )
    return pl.pallas_call(
        functools.partial(_expert_kernel, tm=tm, n_f=n_f),
        grid_spec=grid_spec,
        out_shape=jax.ShapeDtypeStruct(xs.shape, F32),
        compiler_params=_params("arbitrary", "arbitrary"),
        name="experts",
    )(tile_expert, n_active, xs, wg, wu, wd)


def _combine_kernel(pos0_ref, pos1_ref, next_pos0_ref, next_pos1_ref, h_ref, meta_ref, g_ref, ys_ref,
                    o_ref, bufs, sems, *, ts):
    step = pl.program_id(0)
    slot = lax.rem(step, 2)

    def gather(tables, dst_slot):
        def issue(q, carry):
            for l in range(LANES):
                for k, table in enumerate(tables):
                    dst = _tile_rows(bufs.at[dst_slot, k], q * LANES + l)
                    pltpu.make_async_copy(_tile_rows(ys_ref, table[q, l]), dst,
                                          sems.at[dst_slot, k]).start(priority=k)
            return carry

        lax.fori_loop(0, ts // LANES, issue, 0)

    @pl.when(step == 0)
    def _():
        gather((pos0_ref, pos1_ref), slot)

    @pl.when(step + 1 < pl.num_programs(0))
    def _():
        gather((next_pos0_ref, next_pos1_ref), 1 - slot)

    for k in range(2):
        _wait_rows(bufs.at[slot, k], sems.at[slot, k])
    meta = meta_ref[...]
    w1 = meta[:, META_W:META_W + 1]
    w2 = meta[:, META_W + 1:META_W + 2]
    y = h_ref[...] + (w1 * _tiles_to_rows(bufs.at[slot, 0], ts)
                      + w2 * _tiles_to_rows(bufs.at[slot, 1], ts))
    o_ref[...] = _rmsnorm(y, g_ref[...])


def _combine(h, meta, g, ys, pos_tiles):
    n_tok = h.shape[0]
    n_steps = pos_tiles[0].shape[0]
    ts = n_tok // n_steps
    kern = functools.partial(_combine_kernel, ts=ts)
    pos_block = (None,) + pos_tiles[0].shape[1:]
    pos_spec = pl.BlockSpec(pos_block, lambda i: (i, 0, 0), memory_space=pltpu.SMEM)
    next_spec = pl.BlockSpec(pos_block, lambda i: (jnp.minimum(i + 1, n_steps - 1), 0, 0),
                             memory_space=pltpu.SMEM)
    return pl.pallas_call(
        kern,
        grid=(n_steps,),
        in_specs=[pos_spec, pos_spec, next_spec, next_spec,
                  pl.BlockSpec((ts, D_MODEL), lambda i: (i, 0)),
                  pl.BlockSpec((ts, LANES), lambda i: (i, 0)),
                  _const_spec(g.shape),
                  pl.BlockSpec(memory_space=pl.ANY)],
        out_specs=pl.BlockSpec((ts, D_MODEL), lambda i: (i, 0)),
        out_shape=jax.ShapeDtypeStruct((n_tok, D_MODEL), F32),
        scratch_shapes=[pltpu.VMEM((2, 2, ts * ROW_TILE, LANES), F32),
                        pltpu.SemaphoreType.DMA((2, 2))],
        compiler_params=_params("arbitrary"),
        name="combine",
    )(*pos_tiles, *pos_tiles, h, meta, g, ys)


def _pos_tiles(pos, ts):
    return [p.reshape(p.shape[0] // ts, ts // LANES, LANES) for p in pos]


def _moe(h, norm_g, wr_pad, tri, wg, wu, wd, final_g):
    n_tok = h.shape[0]
    tm = TILE_EXPERT
    meta, fields, cnt = _router(h, norm_g, wr_pad, tri)

    counts = cnt[0, :N_EXPERTS].astype(jnp.int32)
    tiles = (counts + (tm - 1)) // tm
    tile_end = jnp.cumsum(tiles)
    tile_start = tile_end - tiles
    n_tiles = (2 * n_tok) // tm + N_EXPERTS
    n_rows = n_tiles * tm

    def slots(k):
        idx = fields[META_IDX + k].astype(jnp.int32)
        rank = fields[META_RANK + k].astype(jnp.int32)
        first_tile = jnp.sum(jnp.where(idx[:, None] == jnp.arange(N_EXPERTS)[None, :],
                                       tile_start[None, :], 0), axis=1)
        return first_tile * tm + rank

    pos = [slots(0), slots(1)]
    n_active = tile_end[-1:].astype(jnp.int32)
    tail_tiles = jnp.where(tiles > 0, tile_end - 1, -1)
    slack_tiles = n_active[0] + jnp.arange(N_EXPERTS)
    slack_tiles = jnp.where(slack_tiles < n_tiles, slack_tiles, -1)
    fill_tiles = jnp.concatenate([tail_tiles, slack_tiles]).astype(jnp.int32)
    t = jnp.minimum(jnp.arange(n_tiles, dtype=jnp.int32), n_active[0] - 1)
    tile_expert = jnp.sum(t[:, None] >= tile_end[None, :], axis=1).astype(jnp.int32)

    xs = _dispatch(h, norm_g, _pos_tiles(pos, TILE_ROUTE), fill_tiles, n_rows, tm)
    ys = _experts(xs, wg, wu, wd, tile_expert, n_active, tm)
    return _combine(h, meta, final_g, ys, _pos_tiles(pos, TILE_COMBINE))


def _pad_last(a, width):
    return jnp.pad(a, [(0, 0)] * (a.ndim - 1) + [(0, width - a.shape[-1])])


def _row(a):
    return a.reshape(1, -1)


def _router_weight(w):
    w_hi = w.astype(BF16)
    w_lo = (w - w_hi.astype(F32)).astype(BF16)
    return jnp.concatenate([_pad_last(w_hi, LANES), _pad_last(w_lo, LANES)], axis=1)


def _mixer_params(norm_g, w_in, conv_a_w, conv_a_b, ln_a_g, ln_a_b, conv_b_w, ln_c_g, ln_c_b,
                  gmlp_ws, gmlp_b, mix_out_g, w_mix_out):
    widths = [A_WIDTH, A_WIDTH, B_WIDTH, B_WIDTH, B_WIDTH, C_WIDTH, C_WIDTH]
    bounds = [0]
    for w in widths:
        bounds.append(bounds[-1] + w)
    w_in_p = jnp.concatenate(
        [_pad_last(w_in[:, bounds[s]:bounds[s + 1]], SEG) for s in range(N_IN_SEG)], axis=1)
    out_bounds = [0, A_WIDTH, A_WIDTH + B_WIDTH, A_WIDTH + B_WIDTH + C_WIDTH]
    out_g = jnp.concatenate(
        [_pad_last(mix_out_g[out_bounds[s]:out_bounds[s + 1]], SEG) for s in range(3)])
    w_out_p = jnp.concatenate(
        [jnp.pad(w_mix_out[out_bounds[s]:out_bounds[s + 1]],
                 [(0, SEG - (out_bounds[s + 1] - out_bounds[s])), (0, 0)]) for s in range(3)], axis=0)
    gbias = _pad_last(jnp.repeat(gmlp_b.T, HEAD_GROUP, axis=1), SEG)
    group = jnp.arange(MXU_DIM) // HEAD_GROUP
    gsum = ((group[:, None] == group[None, :]) * (1.0 / HEAD_GROUP)).astype(BF16)
    return {
        "norm_g": _row(norm_g), "w_in": w_in_p.astype(BF16),
        "conv_a_w": jnp.repeat(conv_a_w, SUBLANES, axis=0),
        "conv_a_b": jnp.broadcast_to(conv_a_b, (SUBLANES, A_WIDTH)),
        "ln_a_g": _row(ln_a_g), "ln_a_b": _row(ln_a_b),
        "conv_b_w": _pad_last(conv_b_w, SEG),
        "ln_c_g": _row(_pad_last(ln_c_g, SEG)), "ln_c_b": _row(_pad_last(ln_c_b, SEG)),
        "gmlp_ws": gmlp_ws.astype(BF16), "gmlp_bias": gbias,
        "out_g": _row(out_g), "w_out": w_out_p.astype(BF16), "gsum": gsum,
    }


def kernel(x, mem, norm_mix_g, w_in, conv_a_w, conv_a_b, ln_a_g, ln_a_b, conv_b_w, ln_c_g, ln_c_b,
           gmlp_ws, gmlp_b, mix_out_g, w_mix_out, norm_x_g, norm_mem_g, w_xq, w_xkv, w_xo,
           norm_ffn_g, ffn_w_gate, ffn_w_up, ffn_w_down, moe_router, moe_w_gate, moe_w_up,
           moe_w_down, norm_final_g):
    bsz, seq_len, _ = x.shape
    mem_len = mem.shape[1]
    depth = w_in.shape[0]
    assert depth == 2 and ffn_w_gate.shape[0] == 1 and moe_router.shape[0] == 1
    assert seq_len % TILE_MIX == 0 and seq_len % TILE_ATT == 0

    h = x.reshape(bsz * seq_len, D_MODEL)
    mem2d = mem.reshape(bsz * mem_len, D_MODEL)
    tri = (jnp.arange(TILE_ROUTE)[:, None] > jnp.arange(TILE_ROUTE)[None, :]).astype(BF16)
    n_exp, _, d_exp = moe_w_gate.shape[1:]
    sides = [moe_w_gate[0].reshape(n_exp * D_MODEL, d_exp), moe_w_up[0].reshape(n_exp * D_MODEL, d_exp),
             moe_w_down[0].reshape(n_exp * d_exp, D_MODEL)]
    casted = []

    def attend(h, layer):
        k, v = _kv_proj(mem2d, _row(norm_mem_g[layer]), w_xkv[layer].astype(BF16))
        h, done = _xattn(h, k, v, _row(norm_x_g[layer]), w_xq[layer].astype(BF16),
                         w_xo[layer].astype(BF16), seq_len, mem_len, sides[len(casted)])
        casted.append(done)
        return h

    def mix(h, layer):
        mp = _mixer_params(norm_mix_g[layer], w_in[layer], conv_a_w[layer], conv_a_b[layer],
                           ln_a_g[layer], ln_a_b[layer], conv_b_w[layer], ln_c_g[layer],
                           ln_c_b[layer], gmlp_ws[layer], gmlp_b[layer], mix_out_g[layer],
                           w_mix_out[layer])
        return _mixer(h, mp, seq_len)

    h = attend(mix(h, 0), 0)
    h, done = _ffn(h, _row(norm_ffn_g[0]), ffn_w_gate[0].astype(BF16), ffn_w_up[0].astype(BF16),
                   ffn_w_down[0].astype(BF16), sides[len(casted)])
    casted.append(done)
    h = attend(mix(h, 1), 1)
    out = _moe(h, _row(norm_ffn_g[1]), _router_weight(moe_router[0]), tri,
               casted[0].reshape(n_exp, D_MODEL, d_exp), casted[1].reshape(n_exp, D_MODEL, d_exp),
               casted[2].reshape(n_exp, d_exp, D_MODEL), _row(norm_final_g))
    return out.reshape(bsz, seq_len, D_MODEL)
```

```python
import functools

import jax
import jax.numpy as jnp
from jax import lax
from jax.experimental import pallas as pl
from jax.experimental.pallas import tpu as pltpu

F32 = jnp.float32
BF16 = jnp.bfloat16

D_MODEL = 1024
EPS = 1e-6
CHUNK = 64
HEAD_GROUP = 64
A_WIDTH, B_WIDTH, C_WIDTH = 384, 320, 320
A_KERNEL, B_KERNEL = 31, 3
GMLP_BLOCK = 128
C_GROUPS = 5
X_HEADS = 4
X_HEAD_DIM = D_MODEL // X_HEADS
N_EXPERTS = 8

LANES = 128
SUBLANES = 8
MXU_DIM = 256
VMEM_LIMIT_BYTES = 56 * 1024 * 1024

SEG = 384
N_IN_SEG = 7
MIX_PAD = 3 * SEG
HALO_A = 32
HALO_B = SUBLANES
CONV_ROWS = 32
EDGE_PARTS = 2
SHIFT_EXTRA = HALO_A - SUBLANES

TILE_MIX = 512
TILE_ATT = 1024
TILE_FFN = 1024
TILE_ROUTE = 512
TILE_COMBINE = 256
TILE_EXPERT = 512
FF_CHUNK = 1792
FF_SUBCHUNK = 512
KV_ROWS = 1024
ROW_TILE = 8


def _rmsnorm(x, g):
    ms = jnp.mean(x * x, axis=-1, keepdims=True)
    return x * lax.rsqrt(ms + EPS) * g


def _const_spec(shape):
    zeros = (0,) * len(shape)
    return pl.BlockSpec(shape, lambda *_: zeros, pipeline_mode=pl.Buffered(1))


def _params(*semantics):
    return pltpu.CompilerParams(dimension_semantics=semantics,
                                vmem_limit_bytes=VMEM_LIMIT_BYTES)


def _mixer_kernel(h_ref, ng_ref, win_ref, caw_ref, cab_ref, lag_ref, lab_ref, cbw_ref,
                  lcg_ref, lcb_ref, ws_ref, gbias_ref, og_ref, wout_ref, gsum_ref,
                  o_ref, zbuf, abuf, bbuf, shifted, ybuf, *, ts, tiles_per_seq):
    seq_tile = lax.rem(pl.program_id(0), tiles_per_seq)

    @pl.when(seq_tile == 0)
    def _():
        abuf[0:HALO_A, :] = jnp.zeros((HALO_A, SEG), F32)
        bbuf[0:HALO_B, :] = jnp.zeros((HALO_B, SEG), F32)

    @pl.when(seq_tile > 0)
    def _():
        abuf[0:HALO_A, :] = abuf[ts:ts + HALO_A, :]
        bbuf[0:HALO_B, :] = bbuf[ts:ts + HALO_B, :]

    h = h_ref[...]
    xn = _rmsnorm(h, ng_ref[...]).astype(BF16)

    part = ts // EDGE_PARTS
    for r0 in range(0, ts, part):
        z_a = jnp.dot(xn[r0:r0 + part, :], win_ref[:, 0:2 * SEG], preferred_element_type=F32)
        abuf[HALO_A + r0:HALO_A + r0 + part, :] = z_a[:, 0:SEG] * jax.nn.sigmoid(z_a[:, SEG:2 * SEG])
        lo = 0 if r0 == 0 else r0 + SHIFT_EXTRA
        hi = r0 + part + SHIFT_EXTRA
        for phase in range(1, SUBLANES):
            shifted[phase - 1, lo:hi, :] = abuf[lo + phase:hi + phase, :]

    zbuf[...] = jnp.dot(xn, win_ref[:, 2 * SEG:N_IN_SEG * SEG], preferred_element_type=F32)

    for r in range(0, ts, CONV_ROWS):
        acc = jnp.concatenate([cab_ref[...]] * (CONV_ROWS // SUBLANES), axis=0)
        for k in range(A_KERNEL):
            groups, phase = divmod(HALO_A - A_KERNEL + 1 + k, SUBLANES)
            start = r + groups * SUBLANES
            if phase == 0:
                window = abuf[start:start + CONV_ROWS, :]
            else:
                window = shifted[phase - 1, start:start + CONV_ROWS, :]
            tap = caw_ref[k * SUBLANES:(k + 1) * SUBLANES, :]
            acc = acc + jnp.concatenate([tap] * (CONV_ROWS // SUBLANES), axis=0) * window
        mu = jnp.mean(acc, axis=-1, keepdims=True)
        xc = acc - mu
        var = jnp.mean(xc * xc, axis=-1, keepdims=True)
        ybuf[r:r + CONV_ROWS, 0:SEG] = jax.nn.silu(
            xc * lax.rsqrt(var + EPS) * lag_ref[...] + lab_ref[...])

    bbuf[HALO_B:HALO_B + ts, :] = zbuf[:, 1 * SEG:2 * SEG] * zbuf[:, 2 * SEG:3 * SEG]
    accb = jnp.zeros((ts, SEG), F32)
    for k in range(B_KERNEL):
        first = HALO_B - B_KERNEL + 1 + k
        accb = accb + cbw_ref[k:k + 1, :] * bbuf[first:first + ts, :]
    ybuf[:, 1 * SEG:2 * SEG] = zbuf[:, 0:SEG] * accb

    c_valid = lax.broadcasted_iota(jnp.int32, (1, SEG), 1) < C_WIDTH
    low_group = lax.broadcasted_iota(jnp.int32, (GMLP_BLOCK, LANES), 1) < HEAD_GROUP
    row_chunk = lax.broadcasted_iota(jnp.int32, (GMLP_BLOCK, GMLP_BLOCK), 0) // CHUNK
    col_chunk = lax.broadcasted_iota(jnp.int32, (GMLP_BLOCK, GMLP_BLOCK), 1) // CHUNK
    chunk_causal = row_chunk >= col_chunk
    ws = [jnp.where(chunk_causal, ws_ref[g], jnp.zeros((), BF16)) for g in range(C_GROUPS)]
    for r in range(0, ts, GMLP_BLOCK):
        rows = slice(r, r + GMLP_BLOCK)
        c_u = jax.nn.gelu(zbuf[rows, 3 * SEG:4 * SEG])
        c_v = jax.nn.gelu(zbuf[rows, 4 * SEG:5 * SEG])
        mu = jnp.sum(c_v, axis=-1, keepdims=True) * (1.0 / C_WIDTH)
        xc = jnp.where(c_valid, c_v - mu, 0.0)
        var = jnp.sum(xc * xc, axis=-1, keepdims=True) * (1.0 / C_WIDTH)
        v = (xc * lax.rsqrt(var + EPS) * lcg_ref[...] + lcb_ref[...]).astype(BF16)
        cols = []
        for j in range(SEG // LANES):
            vj = v[:, j * LANES:(j + 1) * LANES]
            col = jnp.dot(ws[2 * j], vj, preferred_element_type=F32)
            if 2 * j + 1 < C_GROUPS:
                col = jnp.where(low_group, col, jnp.dot(ws[2 * j + 1], vj, preferred_element_type=F32))
            cols.append(col)
        mixed = jnp.concatenate(cols, axis=1) + gbias_ref[...]
        ybuf[rows, 2 * SEG:3 * SEG] = c_u * mixed

    gsum = gsum_ref[...]
    for r0 in range(0, ts, part):
        y = ybuf[r0:r0 + part, :]
        y2 = (y * y).astype(BF16)
        sums = []
        for c0 in range(0, MIX_PAD, MXU_DIM):
            width = min(MXU_DIM, MIX_PAD - c0)
            sums.append(jnp.dot(y2[:, c0:c0 + width], gsum[0:width, 0:width],
                                preferred_element_type=F32))
        ms = jnp.concatenate(sums, axis=1)
        yn = (y * lax.rsqrt(ms + EPS) * og_ref[...]).astype(BF16)
        o_ref[r0:r0 + part, :] = h[r0:r0 + part, :] + jnp.dot(yn, wout_ref[...], preferred_element_type=F32)


def _mixer(h, p, seq_len):
    n_tok = h.shape[0]
    ts = TILE_MIX
    tiles_per_seq = seq_len // ts
    kern = functools.partial(_mixer_kernel, ts=ts, tiles_per_seq=tiles_per_seq)
    consts = [p["norm_g"], p["w_in"], p["conv_a_w"], p["conv_a_b"], p["ln_a_g"], p["ln_a_b"],
              p["conv_b_w"], p["ln_c_g"], p["ln_c_b"], p["gmlp_ws"], p["gmlp_bias"],
              p["out_g"], p["w_out"], p["gsum"]]
    return pl.pallas_call(
        kern,
        grid=(n_tok // ts,),
        in_specs=[pl.BlockSpec((ts, D_MODEL), lambda i: (i, 0))]
                 + [_const_spec(c.shape) for c in consts],
        out_specs=pl.BlockSpec((ts, D_MODEL), lambda i: (i, 0)),
        out_shape=jax.ShapeDtypeStruct((n_tok, D_MODEL), F32),
        scratch_shapes=[
            pltpu.VMEM((ts, (N_IN_SEG - 2) * SEG), F32),
            pltpu.VMEM((HALO_A + ts, SEG), F32),
            pltpu.VMEM((HALO_B + ts, SEG), F32),
            pltpu.VMEM((SUBLANES - 1, ts + SHIFT_EXTRA, SEG), F32),
            pltpu.VMEM((ts, MIX_PAD), F32),
        ],
        compiler_params=_params("arbitrary"),
        name="mixer",
    )(h, *consts)


def _kv_kernel(mem_ref, g_ref, wkv_ref, k_ref, v_ref):
    mn = _rmsnorm(mem_ref[...], g_ref[...]).astype(BF16)
    kv = jnp.dot(mn, wkv_ref[...], preferred_element_type=F32)
    k_ref[...] = kv[:, 0:D_MODEL].astype(BF16)
    v_ref[...] = kv[:, D_MODEL:2 * D_MODEL].astype(BF16)


def _kv_proj(mem2d, g, wkv):
    n = mem2d.shape[0]
    rows = min(KV_ROWS, n)
    return pl.pallas_call(
        _kv_kernel,
        grid=(n // rows,),
        in_specs=[pl.BlockSpec((rows, D_MODEL), lambda i: (i, 0)),
                  _const_spec(g.shape), _const_spec(wkv.shape)],
        out_specs=[pl.BlockSpec((rows, D_MODEL), lambda i: (i, 0))] * 2,
        out_shape=[jax.ShapeDtypeStruct((n, D_MODEL), BF16)] * 2,
        compiler_params=_params("arbitrary"),
        name="kv_proj",
    )(mem2d, g, wkv)


def _side_cast_specs(side, n_steps):
    rows = side.shape[0] // n_steps
    assert side.shape[0] % n_steps == 0 and rows % (2 * SUBLANES) == 0
    spec = pl.BlockSpec((rows, side.shape[1]), lambda i: (i, 0))
    return spec, spec, jax.ShapeDtypeStruct(side.shape, BF16)


def _xattn_kernel(h_ref, g_ref, wq_ref, k_ref, v_ref, wo_ref, side_ref, o_ref, side_out_ref):
    side_out_ref[...] = side_ref[...].astype(BF16)
    h = h_ref[...]
    xn = _rmsnorm(h, g_ref[...]).astype(BF16)
    q = (jnp.dot(xn, wq_ref[...], preferred_element_type=F32) * (X_HEAD_DIM ** -0.5)).astype(BF16)
    heads = []
    for hd in range(X_HEADS):
        cols = slice(hd * X_HEAD_DIM, (hd + 1) * X_HEAD_DIM)
        s = lax.dot_general(q[:, cols], k_ref[:, cols], (((1,), (1,)), ((), ())),
                            preferred_element_type=F32)
        e = jnp.exp(s - jnp.max(s, axis=-1, keepdims=True))
        pv = jnp.dot(e.astype(BF16), v_ref[:, cols], preferred_element_type=F32)
        heads.append(pv / jnp.sum(e, axis=-1, keepdims=True))
    o = jnp.concatenate(heads, axis=1).astype(BF16)
    o_ref[...] = h + jnp.dot(o, wo_ref[...], preferred_element_type=F32)


def _xattn(h, k, v, g, wq, wo, seq_len, mem_len, side):
    n_tok = h.shape[0]
    ts = TILE_ATT
    tiles_per_seq = seq_len // ts
    side_in, side_out, side_shape = _side_cast_specs(side, n_tok // ts)
    return pl.pallas_call(
        _xattn_kernel,
        grid=(n_tok // ts,),
        in_specs=[pl.BlockSpec((ts, D_MODEL), lambda i: (i, 0)),
                  _const_spec(g.shape), _const_spec(wq.shape),
                  pl.BlockSpec((mem_len, D_MODEL), lambda i: (i // tiles_per_seq, 0)),
                  pl.BlockSpec((mem_len, D_MODEL), lambda i: (i // tiles_per_seq, 0)),
                  _const_spec(wo.shape), side_in],
        out_specs=[pl.BlockSpec((ts, D_MODEL), lambda i: (i, 0)), side_out],
        out_shape=[jax.ShapeDtypeStruct((n_tok, D_MODEL), F32), side_shape],
        compiler_params=_params("arbitrary"),
        name="xattn",
    )(h, g, wq, k, v, wo, side)


def _ffn_chunks(d_ff):
    step = -(-d_ff // (3 * MXU_DIM)) * MXU_DIM
    return [(c0, min(c0 + step, d_ff)) for c0 in range(0, d_ff, step)]


def _ffn_kernel(h_ref, g_ref, wg_ref, wu_ref, wd_ref, side_ref, o_ref, side_out_ref, *, chunks):
    side_out_ref[...] = side_ref[...].astype(BF16)
    h = h_ref[...]
    xn = _rmsnorm(h, g_ref[...]).astype(BF16)
    acc = h
    for c0, c1 in chunks:
        gate = jnp.dot(xn, wg_ref[:, c0:c1], preferred_element_type=F32)
        up = jnp.dot(xn, wu_ref[:, c0:c1], preferred_element_type=F32)
        act = (jax.nn.silu(gate) * up).astype(BF16)
        acc = acc + jnp.dot(act, wd_ref[c0:c1, :], preferred_element_type=F32)
    o_ref[...] = acc


def _ffn(h, g, wg, wu, wd, side):
    n_tok = h.shape[0]
    ts = TILE_FFN
    kern = functools.partial(_ffn_kernel, chunks=_ffn_chunks(wg.shape[1]))
    side_in, side_out, side_shape = _side_cast_specs(side, n_tok // ts)
    return pl.pallas_call(
        kern,
        grid=(n_tok // ts,),
        in_specs=[pl.BlockSpec((ts, D_MODEL), lambda i: (i, 0)),
                  _const_spec(g.shape), _const_spec(wg.shape),
                  _const_spec(wu.shape), _const_spec(wd.shape), side_in],
        out_specs=[pl.BlockSpec((ts, D_MODEL), lambda i: (i, 0)), side_out],
        out_shape=[jax.ShapeDtypeStruct((n_tok, D_MODEL), F32), side_shape],
        compiler_params=_params("arbitrary"),
        name="ffn",
    )(h, g, wg, wu, wd, side)


META_IDX, META_RANK, META_W = 0, 2, 4


def _router_kernel(h_ref, g_ref, wr_ref, tri_ref, meta_ref, fields_ref, cnt_ref, run_ref):
    @pl.when(pl.program_id(0) == 0)
    def _():
        run_ref[...] = jnp.zeros_like(run_ref)

    hn = _rmsnorm(h_ref[...], g_ref[...])
    hn_hi = hn.astype(BF16)
    hn_lo = (hn - hn_hi.astype(F32)).astype(BF16)
    both = jnp.dot(hn_hi, wr_ref[...], preferred_element_type=F32)
    logits = (both[:, 0:LANES] + both[:, LANES:2 * LANES]
              + jnp.dot(hn_lo, wr_ref[:, 0:LANES], preferred_element_type=F32))
    lane = lax.broadcasted_iota(jnp.int32, logits.shape, 1).astype(F32)
    neg_inf = jnp.float32(-jnp.inf)
    l1 = jnp.where(lane < N_EXPERTS, logits, neg_inf)
    m1 = jnp.max(l1, axis=-1, keepdims=True)
    i1 = jnp.min(jnp.where(l1 == m1, lane, float(LANES)), axis=-1, keepdims=True)
    l2 = jnp.where(lane == i1, neg_inf, l1)
    m2 = jnp.max(l2, axis=-1, keepdims=True)
    i2 = jnp.min(jnp.where(l2 == m2, lane, float(LANES)), axis=-1, keepdims=True)
    e2 = jnp.exp(m2 - m1)
    w1 = 1.0 / (1.0 + e2)
    w2 = e2 / (1.0 + e2)
    sel1 = lane == i1
    sel2 = lane == i2
    onehot = jnp.where(sel1 | sel2, 1.0, 0.0)
    before = jnp.dot(tri_ref[...], onehot.astype(BF16), preferred_element_type=F32)
    rank = before + run_ref[0:1, :]
    r1 = jnp.sum(jnp.where(sel1, rank, 0.0), axis=-1, keepdims=True)
    r2 = jnp.sum(jnp.where(sel2, rank, 0.0), axis=-1, keepdims=True)
    total = run_ref[0:1, :] + jnp.sum(onehot, axis=0, keepdims=True)
    run_ref[0:1, :] = total
    cnt_ref[...] = jnp.broadcast_to(total, cnt_ref.shape)
    meta = jnp.zeros(logits.shape, F32)
    for col, val in ((META_IDX, i1), (META_IDX + 1, i2), (META_RANK, r1), (META_RANK + 1, r2),
                     (META_W, w1), (META_W + 1, w2)):
        meta = jnp.where(lane == col, val, meta)
    meta_ref[...] = meta
    fields_ref[...] = meta.T[0:SUBLANES, :]


def _router(h, g, wr_pad, tri):
    n_tok = h.shape[0]
    ts = TILE_ROUTE
    return pl.pallas_call(
        _router_kernel,
        grid=(n_tok // ts,),
        in_specs=[pl.BlockSpec((ts, D_MODEL), lambda i: (i, 0)),
                  _const_spec(g.shape), _const_spec(wr_pad.shape), _const_spec(tri.shape)],
        out_specs=[pl.BlockSpec((ts, LANES), lambda i: (i, 0)),
                   pl.BlockSpec((SUBLANES, ts), lambda i: (0, i)),
                   pl.BlockSpec((SUBLANES, LANES), lambda i: (0, 0))],
        out_shape=[jax.ShapeDtypeStruct((n_tok, LANES), F32),
                   jax.ShapeDtypeStruct((SUBLANES, n_tok), F32),
                   jax.ShapeDtypeStruct((SUBLANES, LANES), F32)],
        scratch_shapes=[pltpu.VMEM((SUBLANES, LANES), F32)],
        compiler_params=_params("arbitrary"),
        name="router",
    )(h, g, wr_pad, tri)


def _rows_to_tiles(x, dst):
    n = x.shape[0]
    for g in range(n // SUBLANES):
        for j in range(ROW_TILE):
            dst[pl.ds(g * SUBLANES * ROW_TILE + j, SUBLANES, stride=ROW_TILE), :] = (
                x[g * SUBLANES:(g + 1) * SUBLANES, j * LANES:(j + 1) * LANES])


def _tiles_to_rows(src, n):
    groups = []
    for g in range(n // SUBLANES):
        groups.append(jnp.concatenate(
            [src[pl.ds(g * SUBLANES * ROW_TILE + j, SUBLANES, stride=ROW_TILE), :]
             for j in range(ROW_TILE)], axis=1))
    return jnp.concatenate(groups, axis=0)


def _tile_rows(ref, row):
    return ref.at[pl.ds(pl.multiple_of(row * ROW_TILE, ROW_TILE), ROW_TILE), :]


def _wait_rows(buf, sem):
    pltpu.make_async_copy(buf, buf, sem).wait()


def _dispatch_kernel(pos0_ref, pos1_ref, fill_ref, h_ref, g_ref, xs_ref, hn_buf, zero_buf, sems,
                     fill_sem, *, ts, tm):
    step = pl.program_id(0)
    slot = lax.rem(step, 2)

    @pl.when(step == 0)
    def _():
        zero_buf[...] = jnp.zeros_like(zero_buf)
        for k in range(2 * N_EXPERTS):
            @pl.when(fill_ref[k] >= 0)
            def _():
                cp = pltpu.make_async_copy(
                    zero_buf, xs_ref.at[pl.ds(fill_ref[k] * (tm * ROW_TILE), tm * ROW_TILE), :],
                    fill_sem)
                cp.start()
                cp.wait()

    rows = hn_buf.at[slot]
    _rows_to_tiles(_rmsnorm(h_ref[...], g_ref[...]), rows)

    def issue(q, carry):
        for l in range(LANES):
            src = _tile_rows(rows, q * LANES + l)
            for k, table in enumerate((pos0_ref, pos1_ref)):
                pltpu.make_async_copy(src, _tile_rows(xs_ref, table[q, l]),
                                      sems.at[slot, k]).start(priority=k)
        return carry

    lax.fori_loop(0, ts // LANES, issue, 0)

    @pl.when(step > 0)
    def _():
        for k in range(2):
            _wait_rows(hn_buf.at[1 - slot], sems.at[1 - slot, k])

    @pl.when(step == pl.num_programs(0) - 1)
    def _():
        for k in range(2):
            _wait_rows(rows, sems.at[slot, k])


def _dispatch(h, g, pos_tiles, fill_tiles, n_rows, tm):
    n_tok = h.shape[0]
    n_steps = pos_tiles[0].shape[0]
    ts = n_tok // n_steps
    kern = functools.partial(_dispatch_kernel, ts=ts, tm=tm)
    pos_spec = pl.BlockSpec((None,) + pos_tiles[0].shape[1:], lambda i: (i, 0, 0),
                            memory_space=pltpu.SMEM)
    return pl.pallas_call(
        kern,
        grid=(n_steps,),
        in_specs=[pos_spec, pos_spec,
                  pl.BlockSpec(memory_space=pltpu.SMEM),
                  pl.BlockSpec((ts, D_MODEL), lambda i: (i, 0)),
                  _const_spec(g.shape)],
        out_specs=pl.BlockSpec(memory_space=pl.ANY),
        out_shape=jax.ShapeDtypeStruct((n_rows * ROW_TILE, LANES), F32),
        scratch_shapes=[pltpu.VMEM((2, ts * ROW_TILE, LANES), F32),
                        pltpu.VMEM((tm * ROW_TILE, LANES), F32),
                        pltpu.SemaphoreType.DMA((2, 2)),
                        pltpu.SemaphoreType.DMA(())],
        compiler_params=_params("arbitrary"),
        name="dispatch",
    )(*pos_tiles, fill_tiles, h, g)


def _expert_kernel(te_ref, na_ref, x_ref, wg_ref, wu_ref, wd_ref, o_ref, acc_ref, *, tm, n_f):
    del te_ref
    f = pl.program_id(1)
    active = pl.program_id(0) < na_ref[0]

    @pl.when(jnp.logical_and(jnp.logical_not(active), f == 0))
    def _():
        o_ref[...] = jnp.zeros_like(o_ref)

    def swiglu_chunk(first, last):
        x = _tiles_to_rows(x_ref, tm).astype(BF16)
        part = None if first else acc_ref[...]
        for c0 in range(0, FF_CHUNK, FF_SUBCHUNK):
            c1 = min(c0 + FF_SUBCHUNK, FF_CHUNK)
            gate = jnp.dot(x, wg_ref[:, c0:c1], preferred_element_type=F32)
            up = jnp.dot(x, wu_ref[:, c0:c1], preferred_element_type=F32)
            act = (jax.nn.silu(gate) * up).astype(BF16)
            down = jnp.dot(act, wd_ref[c0:c1, :], preferred_element_type=F32)
            part = down if part is None else part + down
        if last:
            _rows_to_tiles(part, o_ref)
        else:
            acc_ref[...] = part

    pl.when(jnp.logical_and(active, f == 0))(functools.partial(swiglu_chunk, True, False))
    pl.when(jnp.logical_and(active, f == n_f - 1))(functools.partial(swiglu_chunk, False, True))
    if n_f > 2:
        pl.when(jnp.logical_and(active, jnp.logical_and(f > 0, f < n_f - 1)))(
            functools.partial(swiglu_chunk, False, False))


def _experts(xs, wg, wu, wd, tile_expert, n_active, tm):
    d_exp = wg.shape[2]
    n_f = d_exp // FF_CHUNK
    assert n_f >= 2
    n_tiles = xs.shape[0] // (tm * ROW_TILE)

    def x_tile(i, na):
        return jnp.minimum(i, na[0] - 1)

    def chunk(i, f, na):
        return jnp.where(i < na[0], f, n_f - 1)

    grid_spec = pltpu.PrefetchScalarGridSpec(
        num_scalar_prefetch=2,
        grid=(n_tiles, n_f),
        in_specs=[
            pl.BlockSpec((tm * ROW_TILE, LANES), lambda i, f, te, na: (x_tile(i, na), 0)),
            pl.BlockSpec((None, D_MODEL, FF_CHUNK), lambda i, f, te, na: (te[i], 0, chunk(i, f, na))),
            pl.BlockSpec((None, D_MODEL, FF_CHUNK), lambda i, f, te, na: (te[i], 0, chunk(i, f, na))),
            pl.BlockSpec((None, FF_CHUNK, D_MODEL), lambda i, f, te, na: (te[i], chunk(i, f, na), 0)),
        ],
        out_specs=pl.BlockSpec((tm * ROW_TILE, LANES), lambda i, f, te, na: (i, 0)),
        scratch_shapes=[pltpu.VMEM((tm, D_MODEL), F32)],
    )
    return pl.pallas_call(
        functools.partial(_expert_kernel, tm=tm, n_f=n_f),
        grid_spec=grid_spec,
        out_shape=jax.ShapeDtypeStruct(xs.shape, F32),
        compiler_params=_params("arbitrary", "arbitrary"),
        name="experts",
    )(tile_expert, n_active, xs, wg, wu, wd)


def _combine_kernel(pos0_ref, pos1_ref, next_pos0_ref, next_pos1_ref, h_ref, meta_ref, g_ref, ys_ref,
                    o_ref, bufs, sems, *, ts):
    step = pl.program_id(0)
    slot = lax.rem(step, 2)

    def gather(tables, dst_slot):
        def issue(q, carry):
            for l in range(LANES):
                for k, table in enumerate(tables):
                    dst = _tile_rows(bufs.at[dst_slot, k], q * LANES + l)
                    pltpu.make_async_copy(_tile_rows(ys_ref, table[q, l]), dst,
                                          sems.at[dst_slot, k]).start(priority=k)
            return carry

        lax.fori_loop(0, ts // LANES, issue, 0)

    @pl.when(step == 0)
    def _():
        gather((pos0_ref, pos1_ref), slot)

    @pl.when(step + 1 < pl.num_programs(0))
    def _():
        gather((next_pos0_ref, next_pos1_ref), 1 - slot)

    for k in range(2):
        _wait_rows(bufs.at[slot, k], sems.at[slot, k])
    meta = meta_ref[...]
    w1 = meta[:, META_W:META_W + 1]
    w2 = meta[:, META_W + 1:META_W + 2]
    y = h_ref[...] + (w1 * _tiles_to_rows(bufs.at[slot, 0], ts)
                      + w2 * _tiles_to_rows(bufs.at[slot, 1], ts))
    o_ref[...] = _rmsnorm(y, g_ref[...])


def _combine(h, meta, g, ys, pos_tiles):
    n_tok = h.shape[0]
    n_steps = pos_tiles[0].shape[0]
    ts = n_tok // n_steps
    kern = functools.partial(_combine_kernel, ts=ts)
    pos_block = (None,) + pos_tiles[0].shape[1:]
    pos_spec = pl.BlockSpec(pos_block, lambda i: (i, 0, 0), memory_space=pltpu.SMEM)
    next_spec = pl.BlockSpec(pos_block, lambda i: (jnp.minimum(i + 1, n_steps - 1), 0, 0),
                             memory_space=pltpu.SMEM)
    return pl.pallas_call(
        kern,
        grid=(n_steps,),
        in_specs=[pos_spec, pos_spec, next_spec, next_spec,
                  pl.BlockSpec((ts, D_MODEL), lambda i: (i, 0)),
                  pl.BlockSpec((ts, LANES), lambda i: (i, 0)),
                  _const_spec(g.shape),
                  pl.BlockSpec(memory_space=pl.ANY)],
        out_specs=pl.BlockSpec((ts, D_MODEL), lambda i: (i, 0)),
        out_shape=jax.ShapeDtypeStruct((n_tok, D_MODEL), F32),
        scratch_shapes=[pltpu.VMEM((2, 2, ts * ROW_TILE, LANES), F32),
                        pltpu.SemaphoreType.DMA((2, 2))],
        compiler_params=_params("arbitrary"),
        name="combine",
    )(*pos_tiles, *pos_tiles, h, meta, g, ys)


def _pos_tiles(pos, ts):
    return [p.reshape(p.shape[0] // ts, ts // LANES, LANES) for p in pos]


def _moe(h, norm_g, wr_pad, tri, wg, wu, wd, final_g):
    n_tok = h.shape[0]
    tm = TILE_EXPERT
    meta, fields, cnt = _router(h, norm_g, wr_pad, tri)

    counts = cnt[0, :N_EXPERTS].astype(jnp.int32)
    tiles = (counts + (tm - 1)) // tm
    tile_end = jnp.cumsum(tiles)
    tile_start = tile_end - tiles
    n_tiles = (2 * n_tok) // tm + N_EXPERTS
    n_rows = n_tiles * tm

    def slots(k):
        idx = fields[META_IDX + k].astype(jnp.int32)
        rank = fields[META_RANK + k].astype(jnp.int32)
        first_tile = jnp.sum(jnp.where(idx[:, None] == jnp.arange(N_EXPERTS)[None, :],
                                       tile_start[None, :], 0), axis=1)
        return first_tile * tm + rank

    pos = [slots(0), slots(1)]
    n_active = tile_end[-1:].astype(jnp.int32)
    tail_tiles = jnp.where(tiles > 0, tile_end - 1, -1)
    slack_tiles = n_active[0] + jnp.arange(N_EXPERTS)
    slack_tiles = jnp.where(slack_tiles < n_tiles, slack_tiles, -1)
    fill_tiles = jnp.concatenate([tail_tiles, slack_tiles]).astype(jnp.int32)
    t = jnp.minimum(jnp.arange(n_tiles, dtype=jnp.int32), n_active[0] - 1)
    tile_expert = jnp.sum(t[:, None] >= tile_end[None, :], axis=1).astype(jnp.int32)

    xs = _dispatch(h, norm_g, _pos_tiles(pos, TILE_ROUTE), fill_tiles, n_rows, tm)
    ys = _experts(xs, wg, wu, wd, tile_expert, n_active, tm)
    return _combine(h, meta, final_g, ys, _pos_tiles(pos, TILE_COMBINE))


def _pad_last(a, width):
    return jnp.pad(a, [(0, 0)] * (a.ndim - 1) + [(0, width - a.shape[-1])])


def _row(a):
    return a.reshape(1, -1)


def _router_weight(w):
    w_hi = w.astype(BF16)
    w_lo = (w - w_hi.astype(F32)).astype(BF16)
    return jnp.concatenate([_pad_last(w_hi, LANES), _pad_last(w_lo, LANES)], axis=1)


def _mixer_params(norm_g, w_in, conv_a_w, conv_a_b, ln_a_g, ln_a_b, conv_b_w, ln_c_g, ln_c_b,
                  gmlp_ws, gmlp_b, mix_out_g, w_mix_out):
    widths = [A_WIDTH, A_WIDTH, B_WIDTH, B_WIDTH, B_WIDTH, C_WIDTH, C_WIDTH]
    bounds = [0]
    for w in widths:
        bounds.append(bounds[-1] + w)
    w_in_p = jnp.concatenate(
        [_pad_last(w_in[:, bounds[s]:bounds[s + 1]], SEG) for s in range(N_IN_SEG)], axis=1)
    out_bounds = [0, A_WIDTH, A_WIDTH + B_WIDTH, A_WIDTH + B_WIDTH + C_WIDTH]
    out_g = jnp.concatenate(
        [_pad_last(mix_out_g[out_bounds[s]:out_bounds[s + 1]], SEG) for s in range(3)])
    w_out_p = jnp.concatenate(
        [jnp.pad(w_mix_out[out_bounds[s]:out_bounds[s + 1]],
                 [(0, SEG - (out_bounds[s + 1] - out_bounds[s])), (0, 0)]) for s in range(3)], axis=0)
    gbias = _pad_last(jnp.repeat(gmlp_b.T, HEAD_GROUP, axis=1), SEG)
    group = jnp.arange(MXU_DIM) // HEAD_GROUP
    gsum = ((group[:, None] == group[None, :]) * (1.0 / HEAD_GROUP)).astype(BF16)
    return {
        "norm_g": _row(norm_g), "w_in": w_in_p.astype(BF16),
        "conv_a_w": jnp.repeat(conv_a_w, SUBLANES, axis=0),
        "conv_a_b": jnp.broadcast_to(conv_a_b, (SUBLANES, A_WIDTH)),
        "ln_a_g": _row(ln_a_g), "ln_a_b": _row(ln_a_b),
        "conv_b_w": _pad_last(conv_b_w, SEG),
        "ln_c_g": _row(_pad_last(ln_c_g, SEG)), "ln_c_b": _row(_pad_last(ln_c_b, SEG)),
        "gmlp_ws": gmlp_ws.astype(BF16), "gmlp_bias": gbias,
        "out_g": _row(out_g), "w_out": w_out_p.astype(BF16), "gsum": gsum,
    }


def kernel(x, mem, norm_mix_g, w_in, conv_a_w, conv_a_b, ln_a_g, ln_a_b, conv_b_w, ln_c_g, ln_c_b,
           gmlp_ws, gmlp_b, mix_out_g, w_mix_out, norm_x_g, norm_mem_g, w_xq, w_xkv, w_xo,
           norm_ffn_g, ffn_w_gate, ffn_w_up, ffn_w_down, moe_router, moe_w_gate, moe_w_up,
           moe_w_down, norm_final_g):
    bsz, seq_len, _ = x.shape
    mem_len = mem.shape[1]
    depth = w_in.shape[0]
    assert depth == 2 and ffn_w_gate.shape[0] == 1 and moe_router.shape[0] == 1
    assert seq_len % TILE_MIX == 0 and seq_len % TILE_ATT == 0

    h = x.reshape(bsz * seq_len, D_MODEL)
    mem2d = mem.reshape(bsz * mem_len, D_MODEL)
    tri = (jnp.arange(TILE_ROUTE)[:, None] > jnp.arange(TILE_ROUTE)[None, :]).astype(BF16)
    n_exp, _, d_exp = moe_w_gate.shape[1:]
    sides = [moe_w_gate[0].reshape(n_exp * D_MODEL, d_exp), moe_w_up[0].reshape(n_exp * D_MODEL, d_exp),
             moe_w_down[0].reshape(n_exp * d_exp, D_MODEL)]
    casted = []

    def attend(h, layer):
        k, v = _kv_proj(mem2d, _row(norm_mem_g[layer]), w_xkv[layer].astype(BF16))
        h, done = _xattn(h, k, v, _row(norm_x_g[layer]), w_xq[layer].astype(BF16),
                         w_xo[layer].astype(BF16), seq_len, mem_len, sides[len(casted)])
        casted.append(done)
        return h

    def mix(h, layer):
        mp = _mixer_params(norm_mix_g[layer], w_in[layer], conv_a_w[layer], conv_a_b[layer],
                           ln_a_g[layer], ln_a_b[layer], conv_b_w[layer], ln_c_g[layer],
                           ln_c_b[layer], gmlp_ws[layer], gmlp_b[layer], mix_out_g[layer],
                           w_mix_out[layer])
        return _mixer(h, mp, seq_len)

    h = attend(mix(h, 0), 0)
    h, done = _ffn(h, _row(norm_ffn_g[0]), ffn_w_gate[0].astype(BF16), ffn_w_up[0].astype(BF16),
                   ffn_w_down[0].astype(BF16), sides[len(casted)])
    casted.append(done)
    h = attend(mix(h, 1), 1)
    out = _moe(h, _row(norm_ffn_g[1]), _router_weight(moe_router[0]), tri,
               casted[0].reshape(n_exp, D_MODEL, d_exp), casted[1].reshape(n_exp, D_MODEL, d_exp),
               casted[2].reshape(n_exp, d_exp, D_MODEL), _row(norm_final_g))
    return out.reshape(bsz, seq_len, D_MODEL)
```

```python
import functools

import jax
import jax.numpy as jnp
from jax import lax
from jax.experimental import pallas as pl
from jax.experimental.pallas import tpu as pltpu

F32 = jnp.float32
BF16 = jnp.bfloat16

D_MODEL = 1024
EPS = 1e-6
CHUNK = 64
HEAD_GROUP = 64
A_WIDTH, B_WIDTH, C_WIDTH = 384, 320, 320
A_KERNEL, B_KERNEL = 31, 3
GMLP_BLOCK = 128
C_GROUPS = 5
X_HEADS = 4
X_HEAD_DIM = D_MODEL // X_HEADS
N_EXPERTS = 8

LANES = 128
SUBLANES = 8
MXU_DIM = 256
VMEM_LIMIT_BYTES = 56 * 1024 * 1024

SEG = 384
N_IN_SEG = 7
MIX_PAD = 3 * SEG
HALO_A = 32
HALO_B = SUBLANES
CONV_ROWS = 32
EDGE_PARTS = 2
SHIFT_EXTRA = HALO_A - SUBLANES

TILE_MIX = 512
TILE_ATT = 1024
TILE_FFN = 1024
TILE_ROUTE = 512
TILE_COMBINE = 256
TILE_EXPERT = 512
FF_CHUNK = 3584
FF_SUBCHUNK = 512
KV_ROWS = 1024
ROW_TILE = 8


def _rmsnorm(x, g):
    ms = jnp.mean(x * x, axis=-1, keepdims=True)
    return x * lax.rsqrt(ms + EPS) * g


def _const_spec(shape):
    zeros = (0,) * len(shape)
    return pl.BlockSpec(shape, lambda *_: zeros, pipeline_mode=pl.Buffered(1))


def _params(*semantics):
    return pltpu.CompilerParams(dimension_semantics=semantics,
                                vmem_limit_bytes=VMEM_LIMIT_BYTES)


def _mixer_kernel(h_ref, ng_ref, win_ref, caw_ref, cab_ref, lag_ref, lab_ref, cbw_ref,
                  lcg_ref, lcb_ref, ws_ref, gbias_ref, og_ref, wout_ref, gsum_ref,
                  o_ref, zbuf, abuf, bbuf, shifted, ybuf, *, ts, tiles_per_seq):
    seq_tile = lax.rem(pl.program_id(0), tiles_per_seq)

    @pl.when(seq_tile == 0)
    def _():
        abuf[0:HALO_A, :] = jnp.zeros((HALO_A, SEG), F32)
        bbuf[0:HALO_B, :] = jnp.zeros((HALO_B, SEG), F32)

    @pl.when(seq_tile > 0)
    def _():
        abuf[0:HALO_A, :] = abuf[ts:ts + HALO_A, :]
        bbuf[0:HALO_B, :] = bbuf[ts:ts + HALO_B, :]

    h = h_ref[...]
    xn = _rmsnorm(h, ng_ref[...]).astype(BF16)

    part = ts // EDGE_PARTS
    for r0 in range(0, ts, part):
        z_a = jnp.dot(xn[r0:r0 + part, :], win_ref[:, 0:2 * SEG], preferred_element_type=F32)
        abuf[HALO_A + r0:HALO_A + r0 + part, :] = z_a[:, 0:SEG] * jax.nn.sigmoid(z_a[:, SEG:2 * SEG])
        lo = 0 if r0 == 0 else r0 + SHIFT_EXTRA
        hi = r0 + part + SHIFT_EXTRA
        for phase in range(1, SUBLANES):
            shifted[phase - 1, lo:hi, :] = abuf[lo + phase:hi + phase, :]

    zbuf[...] = jnp.dot(xn, win_ref[:, 2 * SEG:N_IN_SEG * SEG], preferred_element_type=F32)

    for r in range(0, ts, CONV_ROWS):
        acc = jnp.concatenate([cab_ref[...]] * (CONV_ROWS // SUBLANES), axis=0)
        for k in range(A_KERNEL):
            groups, phase = divmod(HALO_A - A_KERNEL + 1 + k, SUBLANES)
            start = r + groups * SUBLANES
            if phase == 0:
                window = abuf[start:start + CONV_ROWS, :]
            else:
                window = shifted[phase - 1, start:start + CONV_ROWS, :]
            tap = caw_ref[k * SUBLANES:(k + 1) * SUBLANES, :]
            acc = acc + jnp.concatenate([tap] * (CONV_ROWS // SUBLANES), axis=0) * window
        mu = jnp.mean(acc, axis=-1, keepdims=True)
        xc = acc - mu
        var = jnp.mean(xc * xc, axis=-1, keepdims=True)
        ybuf[r:r + CONV_ROWS, 0:SEG] = jax.nn.silu(
            xc * lax.rsqrt(var + EPS) * lag_ref[...] + lab_ref[...])

    bbuf[HALO_B:HALO_B + ts, :] = zbuf[:, 1 * SEG:2 * SEG] * zbuf[:, 2 * SEG:3 * SEG]
    accb = jnp.zeros((ts, SEG), F32)
    for k in range(B_KERNEL):
        first = HALO_B - B_KERNEL + 1 + k
        accb = accb + cbw_ref[k:k + 1, :] * bbuf[first:first + ts, :]
    ybuf[:, 1 * SEG:2 * SEG] = zbuf[:, 0:SEG] * accb

    c_valid = lax.broadcasted_iota(jnp.int32, (1, SEG), 1) < C_WIDTH
    low_group = lax.broadcasted_iota(jnp.int32, (GMLP_BLOCK, LANES), 1) < HEAD_GROUP
    row_chunk = lax.broadcasted_iota(jnp.int32, (GMLP_BLOCK, GMLP_BLOCK), 0) // CHUNK
    col_chunk = lax.broadcasted_iota(jnp.int32, (GMLP_BLOCK, GMLP_BLOCK), 1) // CHUNK
    chunk_causal = row_chunk >= col_chunk
    ws = [jnp.where(chunk_causal, ws_ref[g], jnp.zeros((), BF16)) for g in range(C_GROUPS)]
    for r in range(0, ts, GMLP_BLOCK):
        rows = slice(r, r + GMLP_BLOCK)
        c_u = jax.nn.gelu(zbuf[rows, 3 * SEG:4 * SEG])
        c_v = jax.nn.gelu(zbuf[rows, 4 * SEG:5 * SEG])
        mu = jnp.sum(c_v, axis=-1, keepdims=True) * (1.0 / C_WIDTH)
        xc = jnp.where(c_valid, c_v - mu, 0.0)
        var = jnp.sum(xc * xc, axis=-1, keepdims=True) * (1.0 / C_WIDTH)
        v = (xc * lax.rsqrt(var + EPS) * lcg_ref[...] + lcb_ref[...]).astype(BF16)
        cols = []
        for j in range(SEG // LANES):
            vj = v[:, j * LANES:(j + 1) * LANES]
            col = jnp.dot(ws[2 * j], vj, preferred_element_type=F32)
            if 2 * j + 1 < C_GROUPS:
                col = jnp.where(low_group, col, jnp.dot(ws[2 * j + 1], vj, preferred_element_type=F32))
            cols.append(col)
        mixed = jnp.concatenate(cols, axis=1) + gbias_ref[...]
        ybuf[rows, 2 * SEG:3 * SEG] = c_u * mixed

    gsum = gsum_ref[...]
    for r0 in range(0, ts, part):
        y = ybuf[r0:r0 + part, :]
        y2 = (y * y).astype(BF16)
        sums = []
        for c0 in range(0, MIX_PAD, MXU_DIM):
            width = min(MXU_DIM, MIX_PAD - c0)
            sums.append(jnp.dot(y2[:, c0:c0 + width], gsum[0:width, 0:width],
                                preferred_element_type=F32))
        ms = jnp.concatenate(sums, axis=1)
        yn = (y * lax.rsqrt(ms + EPS) * og_ref[...]).astype(BF16)
        o_ref[r0:r0 + part, :] = h[r0:r0 + part, :] + jnp.dot(yn, wout_ref[...], preferred_element_type=F32)


def _mixer(h, p, seq_len):
    n_tok = h.shape[0]
    ts = TILE_MIX
    tiles_per_seq = seq_len // ts
    kern = functools.partial(_mixer_kernel, ts=ts, tiles_per_seq=tiles_per_seq)
    consts = [p["norm_g"], p["w_in"], p["conv_a_w"], p["conv_a_b"], p["ln_a_g"], p["ln_a_b"],
              p["conv_b_w"], p["ln_c_g"], p["ln_c_b"], p["gmlp_ws"], p["gmlp_bias"],
              p["out_g"], p["w_out"], p["gsum"]]
    return pl.pallas_call(
        kern,
        grid=(n_tok // ts,),
        in_specs=[pl.BlockSpec((ts, D_MODEL), lambda i: (i, 0))]
                 + [_const_spec(c.shape) for c in consts],
        out_specs=pl.BlockSpec((ts, D_MODEL), lambda i: (i, 0)),
        out_shape=jax.ShapeDtypeStruct((n_tok, D_MODEL), F32),
        scratch_shapes=[
            pltpu.VMEM((ts, (N_IN_SEG - 2) * SEG), F32),
            pltpu.VMEM((HALO_A + ts, SEG), F32),
            pltpu.VMEM((HALO_B + ts, SEG), F32),
            pltpu.VMEM((SUBLANES - 1, ts + SHIFT_EXTRA, SEG), F32),
            pltpu.VMEM((ts, MIX_PAD), F32),
        ],
        compiler_params=_params("arbitrary"),
        name="mixer",
    )(h, *consts)


def _kv_kernel(mem_ref, g_ref, wkv_ref, k_ref, v_ref):
    mn = _rmsnorm(mem_ref[...], g_ref[...]).astype(BF16)
    kv = jnp.dot(mn, wkv_ref[...], preferred_element_type=F32)
    k_ref[...] = kv[:, 0:D_MODEL].astype(BF16)
    v_ref[...] = kv[:, D_MODEL:2 * D_MODEL].astype(BF16)


def _kv_proj(mem2d, g, wkv):
    n = mem2d.shape[0]
    rows = min(KV_ROWS, n)
    return pl.pallas_call(
        _kv_kernel,
        grid=(n // rows,),
        in_specs=[pl.BlockSpec((rows, D_MODEL), lambda i: (i, 0)),
                  _const_spec(g.shape), _const_spec(wkv.shape)],
        out_specs=[pl.BlockSpec((rows, D_MODEL), lambda i: (i, 0))] * 2,
        out_shape=[jax.ShapeDtypeStruct((n, D_MODEL), BF16)] * 2,
        compiler_params=_params("arbitrary"),
        name="kv_proj",
    )(mem2d, g, wkv)


def _side_cast_specs(side, n_steps):
    rows = side.shape[0] // n_steps
    assert side.shape[0] % n_steps == 0 and rows % (2 * SUBLANES) == 0
    spec = pl.BlockSpec((rows, side.shape[1]), lambda i: (i, 0))
    return spec, spec, jax.ShapeDtypeStruct(side.shape, BF16)


def _xattn_kernel(h_ref, g_ref, wq_ref, k_ref, v_ref, wo_ref, side_ref, o_ref, side_out_ref):
    side_out_ref[...] = side_ref[...].astype(BF16)
    h = h_ref[...]
    xn = _rmsnorm(h, g_ref[...]).astype(BF16)
    q = (jnp.dot(xn, wq_ref[...], preferred_element_type=F32) * (X_HEAD_DIM ** -0.5)).astype(BF16)
    heads = []
    for hd in range(X_HEADS):
        cols = slice(hd * X_HEAD_DIM, (hd + 1) * X_HEAD_DIM)
        s = lax.dot_general(q[:, cols], k_ref[:, cols], (((1,), (1,)), ((), ())),
                            preferred_element_type=F32)
        e = jnp.exp(s - jnp.max(s, axis=-1, keepdims=True))
        pv = jnp.dot(e.astype(BF16), v_ref[:, cols], preferred_element_type=F32)
        heads.append(pv / jnp.sum(e, axis=-1, keepdims=True))
    o = jnp.concatenate(heads, axis=1).astype(BF16)
    o_ref[...] = h + jnp.dot(o, wo_ref[...], preferred_element_type=F32)


def _xattn(h, k, v, g, wq, wo, seq_len, mem_len, side):
    n_tok = h.shape[0]
    ts = TILE_ATT
    tiles_per_seq = seq_len // ts
    side_in, side_out, side_shape = _side_cast_specs(side, n_tok // ts)
    return pl.pallas_call(
        _xattn_kernel,
        grid=(n_tok // ts,),
        in_specs=[pl.BlockSpec((ts, D_MODEL), lambda i: (i, 0)),
                  _const_spec(g.shape), _const_spec(wq.shape),
                  pl.BlockSpec((mem_len, D_MODEL), lambda i: (i // tiles_per_seq, 0)),
                  pl.BlockSpec((mem_len, D_MODEL), lambda i: (i // tiles_per_seq, 0)),
                  _const_spec(wo.shape), side_in],
        out_specs=[pl.BlockSpec((ts, D_MODEL), lambda i: (i, 0)), side_out],
        out_shape=[jax.ShapeDtypeStruct((n_tok, D_MODEL), F32), side_shape],
        compiler_params=_params("arbitrary"),
        name="xattn",
    )(h, g, wq, k, v, wo, side)


def _ffn_chunks(d_ff):
    step = -(-d_ff // (3 * MXU_DIM)) * MXU_DIM
    return [(c0, min(c0 + step, d_ff)) for c0 in range(0, d_ff, step)]


def _ffn_kernel(h_ref, g_ref, wg_ref, wu_ref, wd_ref, side_ref, o_ref, side_out_ref, *, chunks):
    side_out_ref[...] = side_ref[...].astype(BF16)
    h = h_ref[...]
    xn = _rmsnorm(h, g_ref[...]).astype(BF16)
    acc = h
    for c0, c1 in chunks:
        gate = jnp.dot(xn, wg_ref[:, c0:c1], preferred_element_type=F32)
        up = jnp.dot(xn, wu_ref[:, c0:c1], preferred_element_type=F32)
        act = (jax.nn.silu(gate) * up).astype(BF16)
        acc = acc + jnp.dot(act, wd_ref[c0:c1, :], preferred_element_type=F32)
    o_ref[...] = acc


def _ffn(h, g, wg, wu, wd, side):
    n_tok = h.shape[0]
    ts = TILE_FFN
    kern = functools.partial(_ffn_kernel, chunks=_ffn_chunks(wg.shape[1]))
    side_in, side_out, side_shape = _side_cast_specs(side, n_tok // ts)
    return pl.pallas_call(
        kern,
        grid=(n_tok // ts,),
        in_specs=[pl.BlockSpec((ts, D_MODEL), lambda i: (i, 0)),
                  _const_spec(g.shape), _const_spec(wg.shape),
                  _const_spec(wu.shape), _const_spec(wd.shape), side_in],
        out_specs=[pl.BlockSpec((ts, D_MODEL), lambda i: (i, 0)), side_out],
        out_shape=[jax.ShapeDtypeStruct((n_tok, D_MODEL), F32), side_shape],
        compiler_params=_params("arbitrary"),
        name="ffn",
    )(h, g, wg, wu, wd, side)


META_IDX, META_RANK, META_W = 0, 2, 4


def _router_kernel(h_ref, g_ref, wr_ref, tri_ref, meta_ref, fields_ref, cnt_ref, run_ref):
    @pl.when(pl.program_id(0) == 0)
    def _():
        run_ref[...] = jnp.zeros_like(run_ref)

    hn = _rmsnorm(h_ref[...], g_ref[...])
    hn_hi = hn.astype(BF16)
    hn_lo = (hn - hn_hi.astype(F32)).astype(BF16)
    both = jnp.dot(hn_hi, wr_ref[...], preferred_element_type=F32)
    logits = (both[:, 0:LANES] + both[:, LANES:2 * LANES]
              + jnp.dot(hn_lo, wr_ref[:, 0:LANES], preferred_element_type=F32))
    lane = lax.broadcasted_iota(jnp.int32, logits.shape, 1).astype(F32)
    neg_inf = jnp.float32(-jnp.inf)
    l1 = jnp.where(lane < N_EXPERTS, logits, neg_inf)
    m1 = jnp.max(l1, axis=-1, keepdims=True)
    i1 = jnp.min(jnp.where(l1 == m1, lane, float(LANES)), axis=-1, keepdims=True)
    l2 = jnp.where(lane == i1, neg_inf, l1)
    m2 = jnp.max(l2, axis=-1, keepdims=True)
    i2 = jnp.min(jnp.where(l2 == m2, lane, float(LANES)), axis=-1, keepdims=True)
    e2 = jnp.exp(m2 - m1)
    w1 = 1.0 / (1.0 + e2)
    w2 = e2 / (1.0 + e2)
    sel1 = lane == i1
    sel2 = lane == i2
    onehot = jnp.where(sel1 | sel2, 1.0, 0.0)
    before = jnp.dot(tri_ref[...], onehot.astype(BF16), preferred_element_type=F32)
    rank = before + run_ref[0:1, :]
    r1 = jnp.sum(jnp.where(sel1, rank, 0.0), axis=-1, keepdims=True)
    r2 = jnp.sum(jnp.where(sel2, rank, 0.0), axis=-1, keepdims=True)
    total = run_ref[0:1, :] + jnp.sum(onehot, axis=0, keepdims=True)
    run_ref[0:1, :] = total
    cnt_ref[...] = jnp.broadcast_to(total, cnt_ref.shape)
    meta = jnp.zeros(logits.shape, F32)
    for col, val in ((META_IDX, i1), (META_IDX + 1, i2), (META_RANK, r1), (META_RANK + 1, r2),
                     (META_W, w1), (META_W + 1, w2)):
        meta = jnp.where(lane == col, val, meta)
    meta_ref[...] = meta
    fields_ref[...] = meta.T[0:SUBLANES, :]


def _router(h, g, wr_pad, tri):
    n_tok = h.shape[0]
    ts = TILE_ROUTE
    return pl.pallas_call(
        _router_kernel,
        grid=(n_tok // ts,),
        in_specs=[pl.BlockSpec((ts, D_MODEL), lambda i: (i, 0)),
                  _const_spec(g.shape), _const_spec(wr_pad.shape), _const_spec(tri.shape)],
        out_specs=[pl.BlockSpec((ts, LANES), lambda i: (i, 0)),
                   pl.BlockSpec((SUBLANES, ts), lambda i: (0, i)),
                   pl.BlockSpec((SUBLANES, LANES), lambda i: (0, 0))],
        out_shape=[jax.ShapeDtypeStruct((n_tok, LANES), F32),
                   jax.ShapeDtypeStruct((SUBLANES, n_tok), F32),
                   jax.ShapeDtypeStruct((SUBLANES, LANES), F32)],
        scratch_shapes=[pltpu.VMEM((SUBLANES, LANES), F32)],
        compiler_params=_params("arbitrary"),
        name="router",
    )(h, g, wr_pad, tri)


def _rows_to_tiles(x, dst):
    n = x.shape[0]
    for g in range(n // SUBLANES):
        for j in range(ROW_TILE):
            dst[pl.ds(g * SUBLANES * ROW_TILE + j, SUBLANES, stride=ROW_TILE), :] = (
                x[g * SUBLANES:(g + 1) * SUBLANES, j * LANES:(j + 1) * LANES])


def _tiles_to_rows(src, n):
    groups = []
    for g in range(n // SUBLANES):
        groups.append(jnp.concatenate(
            [src[pl.ds(g * SUBLANES * ROW_TILE + j, SUBLANES, stride=ROW_TILE), :]
             for j in range(ROW_TILE)], axis=1))
    return jnp.concatenate(groups, axis=0)


def _tile_rows(ref, row):
    return ref.at[pl.ds(pl.multiple_of(row * ROW_TILE, ROW_TILE), ROW_TILE), :]


def _wait_rows(buf, sem):
    pltpu.make_async_copy(buf, buf, sem).wait()


def _dispatch_kernel(pos0_ref, pos1_ref, fill_ref, h_ref, g_ref, xs_ref, hn_buf, zero_buf, sems,
                     fill_sem, *, ts, tm):
    step = pl.program_id(0)
    slot = lax.rem(step, 2)

    @pl.when(step == 0)
    def _():
        zero_buf[...] = jnp.zeros_like(zero_buf)
        for k in range(2 * N_EXPERTS):
            @pl.when(fill_ref[k] >= 0)
            def _():
                cp = pltpu.make_async_copy(
                    zero_buf, xs_ref.at[pl.ds(fill_ref[k] * (tm * ROW_TILE), tm * ROW_TILE), :],
                    fill_sem)
                cp.start()
                cp.wait()

    rows = hn_buf.at[slot]
    _rows_to_tiles(_rmsnorm(h_ref[...], g_ref[...]), rows)

    def issue(q, carry):
        for l in range(LANES):
            src = _tile_rows(rows, q * LANES + l)
            for k, table in enumerate((pos0_ref, pos1_ref)):
                pltpu.make_async_copy(src, _tile_rows(xs_ref, table[q, l]),
                                      sems.at[slot, k]).start(priority=k)
        return carry

    lax.fori_loop(0, ts // LANES, issue, 0)

    @pl.when(step > 0)
    def _():
        for k in range(2):
            _wait_rows(hn_buf.at[1 - slot], sems.at[1 - slot, k])

    @pl.when(step == pl.num_programs(0) - 1)
    def _():
        for k in range(2):
            _wait_rows(rows, sems.at[slot, k])


def _dispatch(h, g, pos_tiles, fill_tiles, n_rows, tm):
    n_tok = h.shape[0]
    n_steps = pos_tiles[0].shape[0]
    ts = n_tok // n_steps
    kern = functools.partial(_dispatch_kernel, ts=ts, tm=tm)
    pos_spec = pl.BlockSpec((None,) + pos_tiles[0].shape[1:], lambda i: (i, 0, 0),
                            memory_space=pltpu.SMEM)
    return pl.pallas_call(
        kern,
        grid=(n_steps,),
        in_specs=[pos_spec, pos_spec,
                  pl.BlockSpec(memory_space=pltpu.SMEM),
                  pl.BlockSpec((ts, D_MODEL), lambda i: (i, 0)),
                  _const_spec(g.shape)],
        out_specs=pl.BlockSpec(memory_space=pl.ANY),
        out_shape=jax.ShapeDtypeStruct((n_rows * ROW_TILE, LANES), F32),
        scratch_shapes=[pltpu.VMEM((2, ts * ROW_TILE, LANES), F32),
                        pltpu.VMEM((tm * ROW_TILE, LANES), F32),
                        pltpu.SemaphoreType.DMA((2, 2)),
                        pltpu.SemaphoreType.DMA(())],
        compiler_params=_params("arbitrary"),
        name="dispatch",
    )(*pos_tiles, fill_tiles, h, g)


def _expert_kernel(te_ref, na_ref, x_ref, wg_ref, wu_ref, wd_ref, o_ref, *acc, tm, n_f):
    del te_ref
    acc_ref = acc[0] if acc else None
    f = pl.program_id(1)
    active = pl.program_id(0) < na_ref[0]

    @pl.when(jnp.logical_and(jnp.logical_not(active), f == 0))
    def _():
        o_ref[...] = jnp.zeros_like(o_ref)

    def swiglu_chunk(first, last):
        x = _tiles_to_rows(x_ref, tm).astype(BF16)
        part = None if first else acc_ref[...]
        for c0 in range(0, FF_CHUNK, FF_SUBCHUNK):
            c1 = min(c0 + FF_SUBCHUNK, FF_CHUNK)
            gate = jnp.dot(x, wg_ref[:, c0:c1], preferred_element_type=F32)
            up = jnp.dot(x, wu_ref[:, c0:c1], preferred_element_type=F32)
            act = (jax.nn.silu(gate) * up).astype(BF16)
            down = jnp.dot(act, wd_ref[c0:c1, :], preferred_element_type=F32)
            part = down if part is None else part + down
        if last:
            _rows_to_tiles(part, o_ref)
        else:
            acc_ref[...] = part

    if n_f == 1:
        pl.when(active)(functools.partial(swiglu_chunk, True, True))
        return
    pl.when(jnp.logical_and(active, f == 0))(functools.partial(swiglu_chunk, True, False))
    pl.when(jnp.logical_and(active, f == n_f - 1))(functools.partial(swiglu_chunk, False, True))
    if n_f > 2:
        pl.when(jnp.logical_and(active, jnp.logical_and(f > 0, f < n_f - 1)))(
            functools.partial(swiglu_chunk, False, False))


def _experts(xs, wg, wu, wd, tile_expert, n_active, tm):
    d_exp = wg.shape[2]
    n_f = d_exp // FF_CHUNK
    n_tiles = xs.shape[0] // (tm * ROW_TILE)

    def x_tile(i, na):
        return jnp.minimum(i, na[0] - 1)

    def chunk(i, f, na):
        return jnp.where(i < na[0], f, n_f - 1)

    grid_spec = pltpu.PrefetchScalarGridSpec(
        num_scalar_prefetch=2,
        grid=(n_tiles, n_f),
        in_specs=[
            pl.BlockSpec((tm * ROW_TILE, LANES), lambda i, f, te, na: (x_tile(i, na), 0)),
            pl.BlockSpec((None, D_MODEL, FF_CHUNK), lambda i, f, te, na: (te[i], 0, chunk(i, f, na))),
            pl.BlockSpec((None, D_MODEL, FF_CHUNK), lambda i, f, te, na: (te[i], 0, chunk(i, f, na))),
            pl.BlockSpec((None, FF_CHUNK, D_MODEL), lambda i, f, te, na: (te[i], chunk(i, f, na), 0)),
        ],
        out_specs=pl.BlockSpec((tm * ROW_TILE, LANES), lambda i, f, te, na: (i, 0)),
        scratch_shapes=[pltpu.VMEM((tm, D_MODEL), F32)] if n_f > 1 else [],
    )
    return pl.pallas_call(
        functools.partial(_expert_kernel, tm=tm, n_f=n_f),
        grid_spec=grid_spec,
        out_shape=jax.ShapeDtypeStruct(xs.shape, F32),
        compiler_params=_params("arbitrary", "arbitrary"),
        name="experts",
    )(tile_expert, n_active, xs, wg, wu, wd)


def _combine_kernel(pos0_ref, pos1_ref, next_pos0_ref, next_pos1_ref, h_ref, meta_ref, g_ref, ys_ref,
                    o_ref, bufs, sems, *, ts):
    step = pl.program_id(0)
    slot = lax.rem(step, 2)

    def gather(tables, dst_slot):
        def issue(q, carry):
            for l in range(LANES):
                for k, table in enumerate(tables):
                    dst = _tile_rows(bufs.at[dst_slot, k], q * LANES + l)
                    pltpu.make_async_copy(_tile_rows(ys_ref, table[q, l]), dst,
                                          sems.at[dst_slot, k]).start(priority=k)
            return carry

        lax.fori_loop(0, ts // LANES, issue, 0)

    @pl.when(step == 0)
    def _():
        gather((pos0_ref, pos1_ref), slot)

    @pl.when(step + 1 < pl.num_programs(0))
    def _():
        gather((next_pos0_ref, next_pos1_ref), 1 - slot)

    for k in range(2):
        _wait_rows(bufs.at[slot, k], sems.at[slot, k])
    meta = meta_ref[...]
    w1 = meta[:, META_W:META_W + 1]
    w2 = meta[:, META_W + 1:META_W + 2]
    y = h_ref[...] + (w1 * _tiles_to_rows(bufs.at[slot, 0], ts)
                      + w2 * _tiles_to_rows(bufs.at[slot, 1], ts))
    o_ref[...] = _rmsnorm(y, g_ref[...])


def _combine(h, meta, g, ys, pos_tiles):
    n_tok = h.shape[0]
    n_steps = pos_tiles[0].shape[0]
    ts = n_tok // n_steps
    kern = functools.partial(_combine_kernel, ts=ts)
    pos_block = (None,) + pos_tiles[0].shape[1:]
    pos_spec = pl.BlockSpec(pos_block, lambda i: (i, 0, 0), memory_space=pltpu.SMEM)
    next_spec = pl.BlockSpec(pos_block, lambda i: (jnp.minimum(i + 1, n_steps - 1), 0, 0),
                             memory_space=pltpu.SMEM)
    return pl.pallas_call(
        kern,
        grid=(n_steps,),
        in_specs=[pos_spec, pos_spec, next_spec, next_spec,
                  pl.BlockSpec((ts, D_MODEL), lambda i: (i, 0)),
                  pl.BlockSpec((ts, LANES), lambda i: (i, 0)),
                  _const_spec(g.shape),
                  pl.BlockSpec(memory_space=pl.ANY)],
        out_specs=pl.BlockSpec((ts, D_MODEL), lambda i: (i, 0)),
        out_shape=jax.ShapeDtypeStruct((n_tok, D_MODEL), F32),
        scratch_shapes=[pltpu.VMEM((2, 2, ts * ROW_TILE, LANES), F32),
                        pltpu.SemaphoreType.DMA((2, 2))],
        compiler_params=_params("arbitrary"),
        name="combine",
    )(*pos_tiles, *pos_tiles, h, meta, g, ys)


def _pos_tiles(pos, ts):
    return [p.reshape(p.shape[0] // ts, ts // LANES, LANES) for p in pos]


def _moe(h, norm_g, wr_pad, tri, wg, wu, wd, final_g):
    n_tok = h.shape[0]
    tm = TILE_EXPERT
    meta, fields, cnt = _router(h, norm_g, wr_pad, tri)

    counts = cnt[0, :N_EXPERTS].astype(jnp.int32)
    tiles = (counts + (tm - 1)) // tm
    tile_end = jnp.cumsum(tiles)
    tile_start = tile_end - tiles
    n_tiles = (2 * n_tok) // tm + N_EXPERTS
    n_rows = n_tiles * tm

    def slots(k):
        idx = fields[META_IDX + k].astype(jnp.int32)
        rank = fields[META_RANK + k].astype(jnp.int32)
        first_tile = jnp.sum(jnp.where(idx[:, None] == jnp.arange(N_EXPERTS)[None, :],
                                       tile_start[None, :], 0), axis=1)
        return first_tile * tm + rank

    pos = [slots(0), slots(1)]
    n_active = tile_end[-1:].astype(jnp.int32)
    tail_tiles = jnp.where(tiles > 0, tile_end - 1, -1)
    slack_tiles = n_active[0] + jnp.arange(N_EXPERTS)
    slack_tiles = jnp.where(slack_tiles < n_tiles, slack_tiles, -1)
    fill_tiles = jnp.concatenate([tail_tiles, slack_tiles]).astype(jnp.int32)
    t = jnp.minimum(jnp.arange(n_tiles, dtype=jnp.int32), n_active[0] - 1)
    tile_expert = jnp.sum(t[:, None] >= tile_end[None, :], axis=1).astype(jnp.int32)

    xs = _dispatch(h, norm_g, _pos_tiles(pos, TILE_ROUTE), fill_tiles, n_rows, tm)
    ys = _experts(xs, wg, wu, wd, tile_expert, n_active, tm)
    return _combine(h, meta, final_g, ys, _pos_tiles(pos, TILE_COMBINE))


def _pad_last(a, width):
    return jnp.pad(a, [(0, 0)] * (a.ndim - 1) + [(0, width - a.shape[-1])])


def _row(a):
    return a.reshape(1, -1)


def _router_weight(w):
    w_hi = w.astype(BF16)
    w_lo = (w - w_hi.astype(F32)).astype(BF16)
    return jnp.concatenate([_pad_last(w_hi, LANES), _pad_last(w_lo, LANES)], axis=1)


def _mixer_params(norm_g, w_in, conv_a_w, conv_a_b, ln_a_g, ln_a_b, conv_b_w, ln_c_g, ln_c_b,
                  gmlp_ws, gmlp_b, mix_out_g, w_mix_out):
    widths = [A_WIDTH, A_WIDTH, B_WIDTH, B_WIDTH, B_WIDTH, C_WIDTH, C_WIDTH]
    bounds = [0]
    for w in widths:
        bounds.append(bounds[-1] + w)
    w_in_p = jnp.concatenate(
        [_pad_last(w_in[:, bounds[s]:bounds[s + 1]], SEG) for s in range(N_IN_SEG)], axis=1)
    out_bounds = [0, A_WIDTH, A_WIDTH + B_WIDTH, A_WIDTH + B_WIDTH + C_WIDTH]
    out_g = jnp.concatenate(
        [_pad_last(mix_out_g[out_bounds[s]:out_bounds[s + 1]], SEG) for s in range(3)])
    w_out_p = jnp.concatenate(
        [jnp.pad(w_mix_out[out_bounds[s]:out_bounds[s + 1]],
                 [(0, SEG - (out_bounds[s + 1] - out_bounds[s])), (0, 0)]) for s in range(3)], axis=0)
    gbias = _pad_last(jnp.repeat(gmlp_b.T, HEAD_GROUP, axis=1), SEG)
    group = jnp.arange(MXU_DIM) // HEAD_GROUP
    gsum = ((group[:, None] == group[None, :]) * (1.0 / HEAD_GROUP)).astype(BF16)
    return {
        "norm_g": _row(norm_g), "w_in": w_in_p.astype(BF16),
        "conv_a_w": jnp.repeat(conv_a_w, SUBLANES, axis=0),
        "conv_a_b": jnp.broadcast_to(conv_a_b, (SUBLANES, A_WIDTH)),
        "ln_a_g": _row(ln_a_g), "ln_a_b": _row(ln_a_b),
        "conv_b_w": _pad_last(conv_b_w, SEG),
        "ln_c_g": _row(_pad_last(ln_c_g, SEG)), "ln_c_b": _row(_pad_last(ln_c_b, SEG)),
        "gmlp_ws": gmlp_ws.astype(BF16), "gmlp_bias": gbias,
        "out_g": _row(out_g), "w_out": w_out_p.astype(BF16), "gsum": gsum,
    }


def kernel(x, mem, norm_mix_g, w_in, conv_a_w, conv_a_b, ln_a_g, ln_a_b, conv_b_w, ln_c_g, ln_c_b,
           gmlp_ws, gmlp_b, mix_out_g, w_mix_out, norm_x_g, norm_mem_g, w_xq, w_xkv, w_xo,
           norm_ffn_g, ffn_w_gate, ffn_w_up, ffn_w_down, moe_router, moe_w_gate, moe_w_up,
           moe_w_down, norm_final_g):
    bsz, seq_len, _ = x.shape
    mem_len = mem.shape[1]
    depth = w_in.shape[0]
    assert depth == 2 and ffn_w_gate.shape[0] == 1 and moe_router.shape[0] == 1
    assert seq_len % TILE_MIX == 0 and seq_len % TILE_ATT == 0

    h = x.reshape(bsz * seq_len, D_MODEL)
    mem2d = mem.reshape(bsz * mem_len, D_MODEL)
    tri = (jnp.arange(TILE_ROUTE)[:, None] > jnp.arange(TILE_ROUTE)[None, :]).astype(BF16)
    n_exp, _, d_exp = moe_w_gate.shape[1:]
    sides = [moe_w_gate[0].reshape(n_exp * D_MODEL, d_exp), moe_w_up[0].reshape(n_exp * D_MODEL, d_exp),
             moe_w_down[0].reshape(n_exp * d_exp, D_MODEL)]
    casted = []

    def attend(h, layer):
        k, v = _kv_proj(mem2d, _row(norm_mem_g[layer]), w_xkv[layer].astype(BF16))
        h, done = _xattn(h, k, v, _row(norm_x_g[layer]), w_xq[layer].astype(BF16),
                         w_xo[layer].astype(BF16), seq_len, mem_len, sides[len(casted)])
        casted.append(done)
        return h

    def mix(h, layer):
        mp = _mixer_params(norm_mix_g[layer], w_in[layer], conv_a_w[layer], conv_a_b[layer],
                           ln_a_g[layer], ln_a_b[layer], conv_b_w[layer], ln_c_g[layer],
                           ln_c_b[layer], gmlp_ws[layer], gmlp_b[layer], mix_out_g[layer],
                           w_mix_out[layer])
        return _mixer(h, mp, seq_len)

    h = attend(mix(h, 0), 0)
    h, done = _ffn(h, _row(norm_ffn_g[0]), ffn_w_gate[0].astype(BF16), ffn_w_up[0].astype(BF16),
                   ffn_w_down[0].astype(BF16), sides[len(casted)])
    casted.append(done)
    h = attend(mix(h, 1), 1)
    out = _moe(h, _row(norm_ffn_g[1]), _router_weight(moe_router[0]), tri,
               casted[0].reshape(n_exp, D_MODEL, d_exp), casted[1].reshape(n_exp, D_MODEL, d_exp),
               casted[2].reshape(n_exp, d_exp, D_MODEL), _row(norm_final_g))
    return out.reshape(bsz, seq_len, D_MODEL)
```

```python
import functools

import jax
import jax.numpy as jnp
from jax import lax
from jax.experimental import pallas as pl
from jax.experimental.pallas import tpu as pltpu

F32 = jnp.float32
BF16 = jnp.bfloat16

D_MODEL = 1024
EPS = 1e-6
CHUNK = 64
HEAD_GROUP = 64
A_WIDTH, B_WIDTH, C_WIDTH = 384, 320, 320
A_KERNEL, B_KERNEL = 31, 3
GMLP_BLOCK = 128
C_GROUPS = 5
X_HEADS = 4
X_HEAD_DIM = D_MODEL // X_HEADS
N_EXPERTS = 8

LANES = 128
SUBLANES = 8
MXU_DIM = 256
VMEM_LIMIT_BYTES = 56 * 1024 * 1024

SEG = 384
N_IN_SEG = 7
MIX_PAD = 3 * SEG
HALO_A = 32
HALO_B = SUBLANES
CONV_ROWS = 32
EDGE_PARTS = 2
SHIFT_EXTRA = HALO_A - SUBLANES

TILE_MIX = 512
TILE_ATT = 1024
TILE_FFN = 1024
TILE_ROUTE = 512
TILE_DISPATCH = 1024
TILE_COMBINE = 256
TILE_EXPERT = 512
FF_CHUNK = 3584
FF_SUBCHUNK = 256
KV_ROWS = 1024
ROW_TILE = 8


def _rmsnorm(x, g):
    ms = jnp.mean(x * x, axis=-1, keepdims=True)
    return x * lax.rsqrt(ms + EPS) * g


def _const_spec(shape):
    zeros = (0,) * len(shape)
    return pl.BlockSpec(shape, lambda *_: zeros, pipeline_mode=pl.Buffered(1))


def _params(*semantics):
    return pltpu.CompilerParams(dimension_semantics=semantics,
                                vmem_limit_bytes=VMEM_LIMIT_BYTES)


def _mixer_kernel(h_ref, ng_ref, win_ref, caw_ref, cab_ref, lag_ref, lab_ref, cbw_ref,
                  lcg_ref, lcb_ref, ws_ref, gbias_ref, og_ref, wout_ref, gsum_ref,
                  o_ref, zbuf, abuf, bbuf, shifted, ybuf, *, ts, tiles_per_seq):
    seq_tile = lax.rem(pl.program_id(0), tiles_per_seq)

    @pl.when(seq_tile == 0)
    def _():
        abuf[0:HALO_A, :] = jnp.zeros((HALO_A, SEG), F32)
        bbuf[0:HALO_B, :] = jnp.zeros((HALO_B, SEG), F32)

    @pl.when(seq_tile > 0)
    def _():
        abuf[0:HALO_A, :] = abuf[ts:ts + HALO_A, :]
        bbuf[0:HALO_B, :] = bbuf[ts:ts + HALO_B, :]

    h = h_ref[...]
    xn = _rmsnorm(h, ng_ref[...]).astype(BF16)

    part = ts // EDGE_PARTS
    for r0 in range(0, ts, part):
        z_a = jnp.dot(xn[r0:r0 + part, :], win_ref[:, 0:2 * SEG], preferred_element_type=F32)
        abuf[HALO_A + r0:HALO_A + r0 + part, :] = z_a[:, 0:SEG] * jax.nn.sigmoid(z_a[:, SEG:2 * SEG])
        lo = 0 if r0 == 0 else r0 + SHIFT_EXTRA
        hi = r0 + part + SHIFT_EXTRA
        for phase in range(1, SUBLANES):
            shifted[phase - 1, lo:hi, :] = abuf[lo + phase:hi + phase, :]

    zbuf[...] = jnp.dot(xn, win_ref[:, 2 * SEG:N_IN_SEG * SEG], preferred_element_type=F32)

    for r in range(0, ts, CONV_ROWS):
        acc = jnp.concatenate([cab_ref[...]] * (CONV_ROWS // SUBLANES), axis=0)
        for k in range(A_KERNEL):
            groups, phase = divmod(HALO_A - A_KERNEL + 1 + k, SUBLANES)
            start = r + groups * SUBLANES
            if phase == 0:
                window = abuf[start:start + CONV_ROWS, :]
            else:
                window = shifted[phase - 1, start:start + CONV_ROWS, :]
            tap = caw_ref[k * SUBLANES:(k + 1) * SUBLANES, :]
            acc = acc + jnp.concatenate([tap] * (CONV_ROWS // SUBLANES), axis=0) * window
        mu = jnp.mean(acc, axis=-1, keepdims=True)
        xc = acc - mu
        var = jnp.mean(xc * xc, axis=-1, keepdims=True)
        ybuf[r:r + CONV_ROWS, 0:SEG] = jax.nn.silu(
            xc * lax.rsqrt(var + EPS) * lag_ref[...] + lab_ref[...])

    bbuf[HALO_B:HALO_B + ts, :] = zbuf[:, 1 * SEG:2 * SEG] * zbuf[:, 2 * SEG:3 * SEG]
    accb = jnp.zeros((ts, SEG), F32)
    for k in range(B_KERNEL):
        first = HALO_B - B_KERNEL + 1 + k
        accb = accb + cbw_ref[k:k + 1, :] * bbuf[first:first + ts, :]
    ybuf[:, 1 * SEG:2 * SEG] = zbuf[:, 0:SEG] * accb

    c_valid = lax.broadcasted_iota(jnp.int32, (1, SEG), 1) < C_WIDTH
    low_group = lax.broadcasted_iota(jnp.int32, (GMLP_BLOCK, LANES), 1) < HEAD_GROUP
    row_chunk = lax.broadcasted_iota(jnp.int32, (GMLP_BLOCK, GMLP_BLOCK), 0) // CHUNK
    col_chunk = lax.broadcasted_iota(jnp.int32, (GMLP_BLOCK, GMLP_BLOCK), 1) // CHUNK
    chunk_causal = row_chunk >= col_chunk
    ws = [jnp.where(chunk_causal, ws_ref[g], jnp.zeros((), BF16)) for g in range(C_GROUPS)]
    for r in range(0, ts, GMLP_BLOCK):
        rows = slice(r, r + GMLP_BLOCK)
        c_u = jax.nn.gelu(zbuf[rows, 3 * SEG:4 * SEG])
        c_v = jax.nn.gelu(zbuf[rows, 4 * SEG:5 * SEG])
        mu = jnp.sum(c_v, axis=-1, keepdims=True) * (1.0 / C_WIDTH)
        xc = jnp.where(c_valid, c_v - mu, 0.0)
        var = jnp.sum(xc * xc, axis=-1, keepdims=True) * (1.0 / C_WIDTH)
        v = (xc * lax.rsqrt(var + EPS) * lcg_ref[...] + lcb_ref[...]).astype(BF16)
        cols = []
        for j in range(SEG // LANES):
            vj = v[:, j * LANES:(j + 1) * LANES]
            col = jnp.dot(ws[2 * j], vj, preferred_element_type=F32)
            if 2 * j + 1 < C_GROUPS:
                col = jnp.where(low_group, col, jnp.dot(ws[2 * j + 1], vj, preferred_element_type=F32))
            cols.append(col)
        mixed = jnp.concatenate(cols, axis=1) + gbias_ref[...]
        ybuf[rows, 2 * SEG:3 * SEG] = c_u * mixed

    gsum = gsum_ref[...]
    for r0 in range(0, ts, part):
        y = ybuf[r0:r0 + part, :]
        y2 = (y * y).astype(BF16)
        sums = []
        for c0 in range(0, MIX_PAD, MXU_DIM):
            width = min(MXU_DIM, MIX_PAD - c0)
            sums.append(jnp.dot(y2[:, c0:c0 + width], gsum[0:width, 0:width],
                                preferred_element_type=F32))
        ms = jnp.concatenate(sums, axis=1)
        yn = (y * lax.rsqrt(ms + EPS) * og_ref[...]).astype(BF16)
        o_ref[r0:r0 + part, :] = h[r0:r0 + part, :] + jnp.dot(yn, wout_ref[...], preferred_element_type=F32)


def _mixer(h, p, seq_len):
    n_tok = h.shape[0]
    ts = TILE_MIX
    tiles_per_seq = seq_len // ts
    kern = functools.partial(_mixer_kernel, ts=ts, tiles_per_seq=tiles_per_seq)
    consts = [p["norm_g"], p["w_in"], p["conv_a_w"], p["conv_a_b"], p["ln_a_g"], p["ln_a_b"],
              p["conv_b_w"], p["ln_c_g"], p["ln_c_b"], p["gmlp_ws"], p["gmlp_bias"],
              p["out_g"], p["w_out"], p["gsum"]]
    return pl.pallas_call(
        kern,
        grid=(n_tok // ts,),
        in_specs=[pl.BlockSpec((ts, D_MODEL), lambda i: (i, 0))]
                 + [_const_spec(c.shape) for c in consts],
        out_specs=pl.BlockSpec((ts, D_MODEL), lambda i: (i, 0)),
        out_shape=jax.ShapeDtypeStruct((n_tok, D_MODEL), F32),
        scratch_shapes=[
            pltpu.VMEM((ts, (N_IN_SEG - 2) * SEG), F32),
            pltpu.VMEM((HALO_A + ts, SEG), F32),
            pltpu.VMEM((HALO_B + ts, SEG), F32),
            pltpu.VMEM((SUBLANES - 1, ts + SHIFT_EXTRA, SEG), F32),
            pltpu.VMEM((ts, MIX_PAD), F32),
        ],
        compiler_params=_params("arbitrary"),
        name="mixer",
    )(h, *consts)


def _kv_kernel(mem_ref, g_ref, wkv_ref, k_ref, v_ref):
    mn = _rmsnorm(mem_ref[...], g_ref[...]).astype(BF16)
    kv = jnp.dot(mn, wkv_ref[...], preferred_element_type=F32)
    k_ref[...] = kv[:, 0:D_MODEL].astype(BF16)
    v_ref[...] = kv[:, D_MODEL:2 * D_MODEL].astype(BF16)


def _kv_proj(mem2d, g, wkv):
    n = mem2d.shape[0]
    rows = min(KV_ROWS, n)
    return pl.pallas_call(
        _kv_kernel,
        grid=(n // rows,),
        in_specs=[pl.BlockSpec((rows, D_MODEL), lambda i: (i, 0)),
                  _const_spec(g.shape), _const_spec(wkv.shape)],
        out_specs=[pl.BlockSpec((rows, D_MODEL), lambda i: (i, 0))] * 2,
        out_shape=[jax.ShapeDtypeStruct((n, D_MODEL), BF16)] * 2,
        compiler_params=_params("arbitrary"),
        name="kv_proj",
    )(mem2d, g, wkv)


def _side_cast_specs(side, n_steps):
    rows = side.shape[0] // n_steps
    assert side.shape[0] % n_steps == 0 and rows % (2 * SUBLANES) == 0
    spec = pl.BlockSpec((rows, side.shape[1]), lambda i: (i, 0))
    return spec, spec, jax.ShapeDtypeStruct(side.shape, BF16)


def _xattn_kernel(h_ref, g_ref, wq_ref, k_ref, v_ref, wo_ref, side_ref, o_ref, side_out_ref):
    side_out_ref[...] = side_ref[...].astype(BF16)
    h = h_ref[...]
    xn = _rmsnorm(h, g_ref[...]).astype(BF16)
    q = (jnp.dot(xn, wq_ref[...], preferred_element_type=F32) * (X_HEAD_DIM ** -0.5)).astype(BF16)
    heads = []
    for hd in range(X_HEADS):
        cols = slice(hd * X_HEAD_DIM, (hd + 1) * X_HEAD_DIM)
        s = lax.dot_general(q[:, cols], k_ref[:, cols], (((1,), (1,)), ((), ())),
                            preferred_element_type=F32)
        e = jnp.exp(s - jnp.max(s, axis=-1, keepdims=True))
        pv = jnp.dot(e.astype(BF16), v_ref[:, cols], preferred_element_type=F32)
        heads.append(pv / jnp.sum(e, axis=-1, keepdims=True))
    o = jnp.concatenate(heads, axis=1).astype(BF16)
    o_ref[...] = h + jnp.dot(o, wo_ref[...], preferred_element_type=F32)


def _xattn(h, k, v, g, wq, wo, seq_len, mem_len, side):
    n_tok = h.shape[0]
    ts = TILE_ATT
    tiles_per_seq = seq_len // ts
    side_in, side_out, side_shape = _side_cast_specs(side, n_tok // ts)
    return pl.pallas_call(
        _xattn_kernel,
        grid=(n_tok // ts,),
        in_specs=[pl.BlockSpec((ts, D_MODEL), lambda i: (i, 0)),
                  _const_spec(g.shape), _const_spec(wq.shape),
                  pl.BlockSpec((mem_len, D_MODEL), lambda i: (i // tiles_per_seq, 0)),
                  pl.BlockSpec((mem_len, D_MODEL), lambda i: (i // tiles_per_seq, 0)),
                  _const_spec(wo.shape), side_in],
        out_specs=[pl.BlockSpec((ts, D_MODEL), lambda i: (i, 0)), side_out],
        out_shape=[jax.ShapeDtypeStruct((n_tok, D_MODEL), F32), side_shape],
        compiler_params=_params("arbitrary"),
        name="xattn",
    )(h, g, wq, k, v, wo, side)


def _ffn_chunks(d_ff):
    step = -(-d_ff // (3 * MXU_DIM)) * MXU_DIM
    return [(c0, min(c0 + step, d_ff)) for c0 in range(0, d_ff, step)]


def _ffn_kernel(h_ref, g_ref, wg_ref, wu_ref, wd_ref, side_ref, o_ref, side_out_ref, *, chunks):
    side_out_ref[...] = side_ref[...].astype(BF16)
    h = h_ref[...]
    xn = _rmsnorm(h, g_ref[...]).astype(BF16)
    acc = h
    for c0, c1 in chunks:
        gate = jnp.dot(xn, wg_ref[:, c0:c1], preferred_element_type=F32)
        up = jnp.dot(xn, wu_ref[:, c0:c1], preferred_element_type=F32)
        act = (jax.nn.silu(gate) * up).astype(BF16)
        acc = acc + jnp.dot(act, wd_ref[c0:c1, :], preferred_element_type=F32)
    o_ref[...] = acc


def _ffn(h, g, wg, wu, wd, side):
    n_tok = h.shape[0]
    ts = TILE_FFN
    kern = functools.partial(_ffn_kernel, chunks=_ffn_chunks(wg.shape[1]))
    side_in, side_out, side_shape = _side_cast_specs(side, n_tok // ts)
    return pl.pallas_call(
        kern,
        grid=(n_tok // ts,),
        in_specs=[pl.BlockSpec((ts, D_MODEL), lambda i: (i, 0)),
                  _const_spec(g.shape), _const_spec(wg.shape),
                  _const_spec(wu.shape), _const_spec(wd.shape), side_in],
        out_specs=[pl.BlockSpec((ts, D_MODEL), lambda i: (i, 0)), side_out],
        out_shape=[jax.ShapeDtypeStruct((n_tok, D_MODEL), F32), side_shape],
        compiler_params=_params("arbitrary"),
        name="ffn",
    )(h, g, wg, wu, wd, side)


META_IDX, META_RANK, META_W = 0, 2, 4


ROUTE_ROWS = 2 * SUBLANES


def _router_kernel(h_ref, g_ref, wr_ref, tri_ref, meta_ref, fields_ref, cnt_ref, run_ref):
    @pl.when(pl.program_id(0) == 0)
    def _():
        run_ref[...] = jnp.zeros_like(run_ref)

    hn = _rmsnorm(h_ref[...], g_ref[...])
    hn_hi = hn.astype(BF16)
    hn_lo = (hn - hn_hi.astype(F32)).astype(BF16)
    both = jnp.dot(hn_hi, wr_ref[...], preferred_element_type=F32)
    logits = (both[:, 0:LANES] + both[:, LANES:2 * LANES]
              + jnp.dot(hn_lo, wr_ref[:, 0:LANES], preferred_element_type=F32))
    lt = logits.T[0:ROUTE_ROWS, :]
    row = lax.broadcasted_iota(jnp.int32, lt.shape, 0).astype(F32)
    neg_inf = jnp.float32(-jnp.inf)
    l1 = jnp.where(row < N_EXPERTS, lt, neg_inf)
    m1 = jnp.max(l1, axis=0, keepdims=True)
    i1 = jnp.min(jnp.where(l1 == m1, row, float(ROUTE_ROWS)), axis=0, keepdims=True)
    l2 = jnp.where(row == i1, neg_inf, l1)
    m2 = jnp.max(l2, axis=0, keepdims=True)
    i2 = jnp.min(jnp.where(l2 == m2, row, float(ROUTE_ROWS)), axis=0, keepdims=True)
    e2 = jnp.exp(m2 - m1)
    w1 = 1.0 / (1.0 + e2)
    w2 = e2 / (1.0 + e2)
    sel1 = row == i1
    sel2 = row == i2
    onehot = jnp.where(sel1 | sel2, 1.0, 0.0)
    before = jnp.dot(onehot.astype(BF16), tri_ref[...], preferred_element_type=F32)
    run = run_ref[:, 0:1]
    rank = before + run
    r1 = jnp.sum(jnp.where(sel1, rank, 0.0), axis=0, keepdims=True)
    r2 = jnp.sum(jnp.where(sel2, rank, 0.0), axis=0, keepdims=True)
    total = run + jnp.sum(onehot, axis=1, keepdims=True)
    run_ref[...] = jnp.broadcast_to(total, run_ref.shape)
    cnt_ref[...] = jnp.broadcast_to(total, cnt_ref.shape)
    zero = jnp.zeros_like(w1)
    fields = jnp.concatenate([i1, i2, r1, r2, w1, w2, zero, zero], axis=0)
    fields_ref[...] = fields
    padded = jnp.concatenate([fields, jnp.zeros((LANES - SUBLANES, fields.shape[1]), F32)], axis=0)
    meta_ref[...] = padded.T


def _router(h, g, wr_pad, tri):
    n_tok = h.shape[0]
    ts = TILE_ROUTE
    return pl.pallas_call(
        _router_kernel,
        grid=(n_tok // ts,),
        in_specs=[pl.BlockSpec((ts, D_MODEL), lambda i: (i, 0)),
                  _const_spec(g.shape), _const_spec(wr_pad.shape), _const_spec(tri.shape)],
        out_specs=[pl.BlockSpec((ts, LANES), lambda i: (i, 0)),
                   pl.BlockSpec((SUBLANES, ts), lambda i: (0, i)),
                   pl.BlockSpec((ROUTE_ROWS, LANES), lambda i: (0, 0))],
        out_shape=[jax.ShapeDtypeStruct((n_tok, LANES), F32),
                   jax.ShapeDtypeStruct((SUBLANES, n_tok), F32),
                   jax.ShapeDtypeStruct((ROUTE_ROWS, LANES), F32)],
        scratch_shapes=[pltpu.VMEM((ROUTE_ROWS, LANES), F32)],
        compiler_params=_params("arbitrary"),
        name="router",
    )(h, g, wr_pad, tri)


def _rows_to_tiles(x, dst):
    n = x.shape[0]
    for g in range(n // SUBLANES):
        for j in range(ROW_TILE):
            dst[pl.ds(g * SUBLANES * ROW_TILE + j, SUBLANES, stride=ROW_TILE), :] = (
                x[g * SUBLANES:(g + 1) * SUBLANES, j * LANES:(j + 1) * LANES])


def _tiles_to_rows(src, n):
    groups = []
    for g in range(n // SUBLANES):
        groups.append(jnp.concatenate(
            [src[pl.ds(g * SUBLANES * ROW_TILE + j, SUBLANES, stride=ROW_TILE), :]
             for j in range(ROW_TILE)], axis=1))
    return jnp.concatenate(groups, axis=0)


def _tile_rows(ref, row):
    return ref.at[pl.ds(pl.multiple_of(row * ROW_TILE, ROW_TILE), ROW_TILE), :]


def _wait_rows(buf, sem):
    pltpu.make_async_copy(buf, buf, sem).wait()


def _dispatch_kernel(pos0_ref, pos1_ref, fill_ref, h_ref, g_ref, xs_ref, hn_buf, zero_buf, sems,
                     fill_sem, *, ts, tm):
    step = pl.program_id(0)
    slot = lax.rem(step, 2)

    @pl.when(step == 0)
    def _():
        zero_buf[...] = jnp.zeros_like(zero_buf)
        for k in range(2 * N_EXPERTS):
            @pl.when(fill_ref[k] >= 0)
            def _():
                cp = pltpu.make_async_copy(
                    zero_buf, xs_ref.at[pl.ds(fill_ref[k] * (tm * ROW_TILE), tm * ROW_TILE), :],
                    fill_sem)
                cp.start()
                cp.wait()

    rows = hn_buf.at[slot]
    _rows_to_tiles(_rmsnorm(h_ref[...], g_ref[...]), rows)

    def issue(q, carry):
        for l in range(LANES):
            src = _tile_rows(rows, q * LANES + l)
            for k, table in enumerate((pos0_ref, pos1_ref)):
                pltpu.make_async_copy(src, _tile_rows(xs_ref, table[q, l]),
                                      sems.at[slot, k]).start(priority=k)
        return carry

    lax.fori_loop(0, ts // LANES, issue, 0)

    @pl.when(step > 0)
    def _():
        for k in range(2):
            _wait_rows(hn_buf.at[1 - slot], sems.at[1 - slot, k])

    @pl.when(step == pl.num_programs(0) - 1)
    def _():
        for k in range(2):
            _wait_rows(rows, sems.at[slot, k])


def _dispatch(h, g, pos_tiles, fill_tiles, n_rows, tm):
    n_tok = h.shape[0]
    n_steps = pos_tiles[0].shape[0]
    ts = n_tok // n_steps
    kern = functools.partial(_dispatch_kernel, ts=ts, tm=tm)
    pos_spec = pl.BlockSpec((None,) + pos_tiles[0].shape[1:], lambda i: (i, 0, 0),
                            memory_space=pltpu.SMEM)
    return pl.pallas_call(
        kern,
        grid=(n_steps,),
        in_specs=[pos_spec, pos_spec,
                  pl.BlockSpec(memory_space=pltpu.SMEM),
                  pl.BlockSpec((ts, D_MODEL), lambda i: (i, 0)),
                  _const_spec(g.shape)],
        out_specs=pl.BlockSpec(memory_space=pl.ANY),
        out_shape=jax.ShapeDtypeStruct((n_rows * ROW_TILE, LANES), F32),
        scratch_shapes=[pltpu.VMEM((2, ts * ROW_TILE, LANES), F32),
                        pltpu.VMEM((tm * ROW_TILE, LANES), F32),
                        pltpu.SemaphoreType.DMA((2, 2)),
                        pltpu.SemaphoreType.DMA(())],
        compiler_params=_params("arbitrary"),
        name="dispatch",
    )(*pos_tiles, fill_tiles, h, g)


def _expert_kernel(te_ref, na_ref, x_ref, wg_ref, wu_ref, wd_ref, o_ref, *acc, tm, n_f):
    del te_ref
    acc_ref = acc[0] if acc else None
    f = pl.program_id(1)
    active = pl.program_id(0) < na_ref[0]

    @pl.when(jnp.logical_and(jnp.logical_not(active), f == 0))
    def _():
        o_ref[...] = jnp.zeros_like(o_ref)

    def swiglu_chunk(first, last):
        x = _tiles_to_rows(x_ref, tm).astype(BF16)
        part = None if first else acc_ref[...]
        for c0 in range(0, FF_CHUNK, FF_SUBCHUNK):
            c1 = min(c0 + FF_SUBCHUNK, FF_CHUNK)
            gate = jnp.dot(x, wg_ref[:, c0:c1], preferred_element_type=F32)
            up = jnp.dot(x, wu_ref[:, c0:c1], preferred_element_type=F32)
            act = (jax.nn.silu(gate) * up).astype(BF16)
            down = jnp.dot(act, wd_ref[c0:c1, :], preferred_element_type=F32)
            part = down if part is None else part + down
        if last:
            _rows_to_tiles(part, o_ref)
        else:
            acc_ref[...] = part

    if n_f == 1:
        pl.when(active)(functools.partial(swiglu_chunk, True, True))
        return
    pl.when(jnp.logical_and(active, f == 0))(functools.partial(swiglu_chunk, True, False))
    pl.when(jnp.logical_and(active, f == n_f - 1))(functools.partial(swiglu_chunk, False, True))
    if n_f > 2:
        pl.when(jnp.logical_and(active, jnp.logical_and(f > 0, f < n_f - 1)))(
            functools.partial(swiglu_chunk, False, False))


def _experts(xs, wg, wu, wd, tile_expert, n_active, tm):
    d_exp = wg.shape[2]
    n_f = d_exp // FF_CHUNK
    n_tiles = xs.shape[0] // (tm * ROW_TILE)

    def x_tile(i, na):
        return jnp.minimum(i, na[0] - 1)

    def chunk(i, f, na):
        return jnp.where(i < na[0], f, n_f - 1)

    grid_spec = pltpu.PrefetchScalarGridSpec(
        num_scalar_prefetch=2,
        grid=(n_tiles, n_f),
        in_specs=[
            pl.BlockSpec((tm * ROW_TILE, LANES), lambda i, f, te, na: (x_tile(i, na), 0)),
            pl.BlockSpec((None, D_MODEL, FF_CHUNK), lambda i, f, te, na: (te[i], 0, chunk(i, f, na))),
            pl.BlockSpec((None, D_MODEL, FF_CHUNK), lambda i, f, te, na: (te[i], 0, chunk(i, f, na))),
            pl.BlockSpec((None, FF_CHUNK, D_MODEL), lambda i, f, te, na: (te[i], chunk(i, f, na), 0)),
        ],
        out_specs=pl.BlockSpec((tm * ROW_TILE, LANES), lambda i, f, te, na: (i, 0)),
        scratch_shapes=[pltpu.VMEM((tm, D_MODEL), F32)] if n_f > 1 else [],
    )
    return pl.pallas_call(
        functools.partial(_expert_kernel, tm=tm, n_f=n_f),
        grid_spec=grid_spec,
        out_shape=jax.ShapeDtypeStruct(xs.shape, F32),
        compiler_params=_params("arbitrary", "arbitrary"),
        name="experts",
    )(tile_expert, n_active, xs, wg, wu, wd)


def _combine_kernel(pos0_ref, pos1_ref, next_pos0_ref, next_pos1_ref, h_ref, meta_ref, g_ref, ys_ref,
                    o_ref, bufs, sems, *, ts):
    step = pl.program_id(0)
    slot = lax.rem(step, 2)

    def gather(tables, dst_slot):
        def issue(q, carry):
            for l in range(LANES):
                for k, table in enumerate(tables):
                    dst = _tile_rows(bufs.at[dst_slot, k], q * LANES + l)
                    pltpu.make_async_copy(_tile_rows(ys_ref, table[q, l]), dst,
                                          sems.at[dst_slot, k]).start(priority=k)
            return carry

        lax.fori_loop(0, ts // LANES, issue, 0)

    @pl.when(step == 0)
    def _():
        gather((pos0_ref, pos1_ref), slot)

    @pl.when(step + 1 < pl.num_programs(0))
    def _():
        gather((next_pos0_ref, next_pos1_ref), 1 - slot)

    for k in range(2):
        _wait_rows(bufs.at[slot, k], sems.at[slot, k])
    meta = meta_ref[...]
    w1 = meta[:, META_W:META_W + 1]
    w2 = meta[:, META_W + 1:META_W + 2]
    y = h_ref[...] + (w1 * _tiles_to_rows(bufs.at[slot, 0], ts)
                      + w2 * _tiles_to_rows(bufs.at[slot, 1], ts))
    o_ref[...] = _rmsnorm(y, g_ref[...])


def _combine(h, meta, g, ys, pos_tiles):
    n_tok = h.shape[0]
    n_steps = pos_tiles[0].shape[0]
    ts = n_tok // n_steps
    kern = functools.partial(_combine_kernel, ts=ts)
    pos_block = (None,) + pos_tiles[0].shape[1:]
    pos_spec = pl.BlockSpec(pos_block, lambda i: (i, 0, 0), memory_space=pltpu.SMEM)
    next_spec = pl.BlockSpec(pos_block, lambda i: (jnp.minimum(i + 1, n_steps - 1), 0, 0),
                             memory_space=pltpu.SMEM)
    return pl.pallas_call(
        kern,
        grid=(n_steps,),
        in_specs=[pos_spec, pos_spec, next_spec, next_spec,
                  pl.BlockSpec((ts, D_MODEL), lambda i: (i, 0)),
                  pl.BlockSpec((ts, LANES), lambda i: (i, 0)),
                  _const_spec(g.shape),
                  pl.BlockSpec(memory_space=pl.ANY)],
        out_specs=pl.BlockSpec((ts, D_MODEL), lambda i: (i, 0)),
        out_shape=jax.ShapeDtypeStruct((n_tok, D_MODEL), F32),
        scratch_shapes=[pltpu.VMEM((2, 2, ts * ROW_TILE, LANES), F32),
                        pltpu.SemaphoreType.DMA((2, 2))],
        compiler_params=_params("arbitrary"),
        name="combine",
    )(*pos_tiles, *pos_tiles, h, meta, g, ys)


def _pos_tiles(pos, ts):
    return [p.reshape(p.shape[0] // ts, ts // LANES, LANES) for p in pos]


def _moe(h, norm_g, wr_pad, tri, wg, wu, wd, final_g):
    n_tok = h.shape[0]
    tm = TILE_EXPERT
    meta, fields, cnt = _router(h, norm_g, wr_pad, tri)

    counts = cnt[:N_EXPERTS, 0].astype(jnp.int32)
    tiles = (counts + (tm - 1)) // tm
    tile_end = jnp.cumsum(tiles)
    tile_start = tile_end - tiles
    n_tiles = (2 * n_tok) // tm + N_EXPERTS
    n_rows = n_tiles * tm

    def slots(k):
        idx = fields[META_IDX + k].astype(jnp.int32)
        rank = fields[META_RANK + k].astype(jnp.int32)
        first_tile = jnp.sum(jnp.where(idx[:, None] == jnp.arange(N_EXPERTS)[None, :],
                                       tile_start[None, :], 0), axis=1)
        return first_tile * tm + rank

    pos = [slots(0), slots(1)]
    n_active = tile_end[-1:].astype(jnp.int32)
    tail_tiles = jnp.where(tiles > 0, tile_end - 1, -1)
    slack_tiles = n_active[0] + jnp.arange(N_EXPERTS)
    slack_tiles = jnp.where(slack_tiles < n_tiles, slack_tiles, -1)
    fill_tiles = jnp.concatenate([tail_tiles, slack_tiles]).astype(jnp.int32)
    t = jnp.minimum(jnp.arange(n_tiles, dtype=jnp.int32), n_active[0] - 1)
    tile_expert = jnp.sum(t[:, None] >= tile_end[None, :], axis=1).astype(jnp.int32)

    xs = _dispatch(h, norm_g, _pos_tiles(pos, TILE_DISPATCH), fill_tiles, n_rows, tm)
    ys = _experts(xs, wg, wu, wd, tile_expert, n_active, tm)
    return _combine(h, meta, final_g, ys, _pos_tiles(pos, TILE_COMBINE))


def _pad_last(a, width):
    return jnp.pad(a, [(0, 0)] * (a.ndim - 1) + [(0, width - a.shape[-1])])


def _row(a):
    return a.reshape(1, -1)


def _router_weight(w):
    w_hi = w.astype(BF16)
    w_lo = (w - w_hi.astype(F32)).astype(BF16)
    return jnp.concatenate([_pad_last(w_hi, LANES), _pad_last(w_lo, LANES)], axis=1)


def _mixer_params(norm_g, w_in, conv_a_w, conv_a_b, ln_a_g, ln_a_b, conv_b_w, ln_c_g, ln_c_b,
                  gmlp_ws, gmlp_b, mix_out_g, w_mix_out):
    widths = [A_WIDTH, A_WIDTH, B_WIDTH, B_WIDTH, B_WIDTH, C_WIDTH, C_WIDTH]
    bounds = [0]
    for w in widths:
        bounds.append(bounds[-1] + w)
    w_in_p = jnp.concatenate(
        [_pad_last(w_in[:, bounds[s]:bounds[s + 1]], SEG) for s in range(N_IN_SEG)], axis=1)
    out_bounds = [0, A_WIDTH, A_WIDTH + B_WIDTH, A_WIDTH + B_WIDTH + C_WIDTH]
    out_g = jnp.concatenate(
        [_pad_last(mix_out_g[out_bounds[s]:out_bounds[s + 1]], SEG) for s in range(3)])
    w_out_p = jnp.concatenate(
        [jnp.pad(w_mix_out[out_bounds[s]:out_bounds[s + 1]],
                 [(0, SEG - (out_bounds[s + 1] - out_bounds[s])), (0, 0)]) for s in range(3)], axis=0)
    gbias = _pad_last(jnp.repeat(gmlp_b.T, HEAD_GROUP, axis=1), SEG)
    group = jnp.arange(MXU_DIM) // HEAD_GROUP
    gsum = ((group[:, None] == group[None, :]) * (1.0 / HEAD_GROUP)).astype(BF16)
    return {
        "norm_g": _row(norm_g), "w_in": w_in_p.astype(BF16),
        "conv_a_w": jnp.repeat(conv_a_w, SUBLANES, axis=0),
        "conv_a_b": jnp.broadcast_to(conv_a_b, (SUBLANES, A_WIDTH)),
        "ln_a_g": _row(ln_a_g), "ln_a_b": _row(ln_a_b),
        "conv_b_w": _pad_last(conv_b_w, SEG),
        "ln_c_g": _row(_pad_last(ln_c_g, SEG)), "ln_c_b": _row(_pad_last(ln_c_b, SEG)),
        "gmlp_ws": gmlp_ws.astype(BF16), "gmlp_bias": gbias,
        "out_g": _row(out_g), "w_out": w_out_p.astype(BF16), "gsum": gsum,
    }


def kernel(x, mem, norm_mix_g, w_in, conv_a_w, conv_a_b, ln_a_g, ln_a_b, conv_b_w, ln_c_g, ln_c_b,
           gmlp_ws, gmlp_b, mix_out_g, w_mix_out, norm_x_g, norm_mem_g, w_xq, w_xkv, w_xo,
           norm_ffn_g, ffn_w_gate, ffn_w_up, ffn_w_down, moe_router, moe_w_gate, moe_w_up,
           moe_w_down, norm_final_g):
    bsz, seq_len, _ = x.shape
    mem_len = mem.shape[1]
    depth = w_in.shape[0]
    assert depth == 2 and ffn_w_gate.shape[0] == 1 and moe_router.shape[0] == 1
    assert seq_len % TILE_MIX == 0 and seq_len % TILE_ATT == 0

    h = x.reshape(bsz * seq_len, D_MODEL)
    mem2d = mem.reshape(bsz * mem_len, D_MODEL)
    tri = (jnp.arange(TILE_ROUTE)[:, None] < jnp.arange(TILE_ROUTE)[None, :]).astype(BF16)
    n_exp, _, d_exp = moe_w_gate.shape[1:]
    sides = [moe_w_gate[0].reshape(n_exp * D_MODEL, d_exp), moe_w_up[0].reshape(n_exp * D_MODEL, d_exp),
             moe_w_down[0].reshape(n_exp * d_exp, D_MODEL)]
    casted = []

    def attend(h, layer):
        k, v = _kv_proj(mem2d, _row(norm_mem_g[layer]), w_xkv[layer].astype(BF16))
        h, done = _xattn(h, k, v, _row(norm_x_g[layer]), w_xq[layer].astype(BF16),
                         w_xo[layer].astype(BF16), seq_len, mem_len, sides[len(casted)])
        casted.append(done)
        return h

    def mix(h, layer):
        mp = _mixer_params(norm_mix_g[layer], w_in[layer], conv_a_w[layer], conv_a_b[layer],
                           ln_a_g[layer], ln_a_b[layer], conv_b_w[layer], ln_c_g[layer],
                           ln_c_b[layer], gmlp_ws[layer], gmlp_b[layer], mix_out_g[layer],
                           w_mix_out[layer])
        return _mixer(h, mp, seq_len)

    h = attend(mix(h, 0), 0)
    h, done = _ffn(h, _row(norm_ffn_g[0]), ffn_w_gate[0].astype(BF16), ffn_w_up[0].astype(BF16),
                   ffn_w_down[0].astype(BF16), sides[len(casted)])
    casted.append(done)
    h = attend(mix(h, 1), 1)
    out = _moe(h, _row(norm_ffn_g[1]), _router_weight(moe_router[0]), tri,
               casted[0].reshape(n_exp, D_MODEL, d_exp), casted[1].reshape(n_exp, D_MODEL, d_exp),
               casted[2].reshape(n_exp, d_exp, D_MODEL), _row(norm_final_g))
    return out.reshape(bsz, seq_len, D_MODEL)
```

```python
import functools

import jax
import jax.numpy as jnp
from jax import lax
from jax.experimental import pallas as pl
from jax.experimental.pallas import tpu as pltpu

F32 = jnp.float32
BF16 = jnp.bfloat16

D_MODEL = 1024
EPS = 1e-6
CHUNK = 64
HEAD_GROUP = 64
A_WIDTH, B_WIDTH, C_WIDTH = 384, 320, 320
A_KERNEL, B_KERNEL = 31, 3
GMLP_BLOCK = 128
C_GROUPS = 5
X_HEADS = 4
X_HEAD_DIM = D_MODEL // X_HEADS
N_EXPERTS = 8

LANES = 128
SUBLANES = 8
MXU_DIM = 256
VMEM_LIMIT_BYTES = 56 * 1024 * 1024

SEG = 384
N_IN_SEG = 7
MIX_PAD = 3 * SEG
HALO_A = 32
HALO_B = SUBLANES
CONV_ROWS = 32
EDGE_PARTS = 2
SHIFT_EXTRA = HALO_A - SUBLANES

TILE_MIX = 512
TILE_ATT = 1024
TILE_FFN = 1024
TILE_ROUTE = 1024
TILE_DISPATCH = 2048
TILE_COMBINE = 256
TILE_EXPERT = 512
FF_CHUNK = 3584
FF_SUBCHUNK = 256
KV_ROWS = 2048
ROW_TILE = 8


def _rmsnorm(x, g):
    ms = jnp.mean(x * x, axis=-1, keepdims=True)
    return x * lax.rsqrt(ms + EPS) * g


def _const_spec(shape):
    zeros = (0,) * len(shape)
    return pl.BlockSpec(shape, lambda *_: zeros, pipeline_mode=pl.Buffered(1))


def _params(*semantics):
    return pltpu.CompilerParams(dimension_semantics=semantics,
                                vmem_limit_bytes=VMEM_LIMIT_BYTES)


def _mixer_kernel(h_ref, ng_ref, win_ref, caw_ref, cab_ref, lag_ref, lab_ref, cbw_ref,
                  lcg_ref, lcb_ref, ws_ref, gbias_ref, og_ref, wout_ref, gsum_ref,
                  o_ref, zbuf, abuf, bbuf, shifted, ybuf, *, ts, tiles_per_seq):
    seq_tile = lax.rem(pl.program_id(0), tiles_per_seq)

    @pl.when(seq_tile == 0)
    def _():
        abuf[0:HALO_A, :] = jnp.zeros((HALO_A, SEG), F32)
        bbuf[0:HALO_B, :] = jnp.zeros((HALO_B, SEG), F32)

    @pl.when(seq_tile > 0)
    def _():
        abuf[0:HALO_A, :] = abuf[ts:ts + HALO_A, :]
        bbuf[0:HALO_B, :] = bbuf[ts:ts + HALO_B, :]

    h = h_ref[...]
    xn = _rmsnorm(h, ng_ref[...]).astype(BF16)

    part = ts // EDGE_PARTS
    for r0 in range(0, ts, part):
        z_a = jnp.dot(xn[r0:r0 + part, :], win_ref[:, 0:2 * SEG], preferred_element_type=F32)
        abuf[HALO_A + r0:HALO_A + r0 + part, :] = z_a[:, 0:SEG] * jax.nn.sigmoid(z_a[:, SEG:2 * SEG])
        lo = 0 if r0 == 0 else r0 + SHIFT_EXTRA
        hi = r0 + part + SHIFT_EXTRA
        for phase in range(1, SUBLANES):
            shifted[phase - 1, lo:hi, :] = abuf[lo + phase:hi + phase, :]

    zbuf[...] = jnp.dot(xn, win_ref[:, 2 * SEG:N_IN_SEG * SEG], preferred_element_type=F32)

    for r in range(0, ts, CONV_ROWS):
        acc = jnp.concatenate([cab_ref[...]] * (CONV_ROWS // SUBLANES), axis=0)
        for k in range(A_KERNEL):
            groups, phase = divmod(HALO_A - A_KERNEL + 1 + k, SUBLANES)
            start = r + groups * SUBLANES
            if phase == 0:
                window = abuf[start:start + CONV_ROWS, :]
            else:
                window = shifted[phase - 1, start:start + CONV_ROWS, :]
            tap = caw_ref[k * SUBLANES:(k + 1) * SUBLANES, :]
            acc = acc + jnp.concatenate([tap] * (CONV_ROWS // SUBLANES), axis=0) * window
        mu = jnp.mean(acc, axis=-1, keepdims=True)
        xc = acc - mu
        var = jnp.mean(xc * xc, axis=-1, keepdims=True)
        ybuf[r:r + CONV_ROWS, 0:SEG] = jax.nn.silu(
            xc * lax.rsqrt(var + EPS) * lag_ref[...] + lab_ref[...])

    bbuf[HALO_B:HALO_B + ts, :] = zbuf[:, 1 * SEG:2 * SEG] * zbuf[:, 2 * SEG:3 * SEG]
    accb = jnp.zeros((ts, SEG), F32)
    for k in range(B_KERNEL):
        first = HALO_B - B_KERNEL + 1 + k
        accb = accb + cbw_ref[k:k + 1, :] * bbuf[first:first + ts, :]
    ybuf[:, 1 * SEG:2 * SEG] = zbuf[:, 0:SEG] * accb

    c_valid = lax.broadcasted_iota(jnp.int32, (1, SEG), 1) < C_WIDTH
    low_group = lax.broadcasted_iota(jnp.int32, (GMLP_BLOCK, LANES), 1) < HEAD_GROUP
    row_chunk = lax.broadcasted_iota(jnp.int32, (GMLP_BLOCK, GMLP_BLOCK), 0) // CHUNK
    col_chunk = lax.broadcasted_iota(jnp.int32, (GMLP_BLOCK, GMLP_BLOCK), 1) // CHUNK
    chunk_causal = row_chunk >= col_chunk
    ws = [jnp.where(chunk_causal, ws_ref[g], jnp.zeros((), BF16)) for g in range(C_GROUPS)]
    for r in range(0, ts, GMLP_BLOCK):
        rows = slice(r, r + GMLP_BLOCK)
        c_u = jax.nn.gelu(zbuf[rows, 3 * SEG:4 * SEG])
        c_v = jax.nn.gelu(zbuf[rows, 4 * SEG:5 * SEG])
        mu = jnp.sum(c_v, axis=-1, keepdims=True) * (1.0 / C_WIDTH)
        xc = jnp.where(c_valid, c_v - mu, 0.0)
        var = jnp.sum(xc * xc, axis=-1, keepdims=True) * (1.0 / C_WIDTH)
        v = (xc * lax.rsqrt(var + EPS) * lcg_ref[...] + lcb_ref[...]).astype(BF16)
        cols = []
        for j in range(SEG // LANES):
            vj = v[:, j * LANES:(j + 1) * LANES]
            col = jnp.dot(ws[2 * j], vj, preferred_element_type=F32)
            if 2 * j + 1 < C_GROUPS:
                col = jnp.where(low_group, col, jnp.dot(ws[2 * j + 1], vj, preferred_element_type=F32))
            cols.append(col)
        mixed = jnp.concatenate(cols, axis=1) + gbias_ref[...]
        ybuf[rows, 2 * SEG:3 * SEG] = c_u * mixed

    gsum = gsum_ref[...]
    for r0 in range(0, ts, part):
        y = ybuf[r0:r0 + part, :]
        y2 = (y * y).astype(BF16)
        sums = []
        for c0 in range(0, MIX_PAD, MXU_DIM):
            width = min(MXU_DIM, MIX_PAD - c0)
            sums.append(jnp.dot(y2[:, c0:c0 + width], gsum[0:width, 0:width],
                                preferred_element_type=F32))
        ms = jnp.concatenate(sums, axis=1)
        yn = (y * lax.rsqrt(ms + EPS) * og_ref[...]).astype(BF16)
        o_ref[r0:r0 + part, :] = h[r0:r0 + part, :] + jnp.dot(yn, wout_ref[...], preferred_element_type=F32)


def _mixer(h, p, seq_len):
    n_tok = h.shape[0]
    ts = TILE_MIX
    tiles_per_seq = seq_len // ts
    kern = functools.partial(_mixer_kernel, ts=ts, tiles_per_seq=tiles_per_seq)
    consts = [p["norm_g"], p["w_in"], p["conv_a_w"], p["conv_a_b"], p["ln_a_g"], p["ln_a_b"],
              p["conv_b_w"], p["ln_c_g"], p["ln_c_b"], p["gmlp_ws"], p["gmlp_bias"],
              p["out_g"], p["w_out"], p["gsum"]]
    return pl.pallas_call(
        kern,
        grid=(n_tok // ts,),
        in_specs=[pl.BlockSpec((ts, D_MODEL), lambda i: (i, 0))]
                 + [_const_spec(c.shape) for c in consts],
        out_specs=pl.BlockSpec((ts, D_MODEL), lambda i: (i, 0)),
        out_shape=jax.ShapeDtypeStruct((n_tok, D_MODEL), F32),
        scratch_shapes=[
            pltpu.VMEM((ts, (N_IN_SEG - 2) * SEG), F32),
            pltpu.VMEM((HALO_A + ts, SEG), F32),
            pltpu.VMEM((HALO_B + ts, SEG), F32),
            pltpu.VMEM((SUBLANES - 1, ts + SHIFT_EXTRA, SEG), F32),
            pltpu.VMEM((ts, MIX_PAD), F32),
        ],
        compiler_params=_params("arbitrary"),
        name="mixer",
    )(h, *consts)


def _kv_kernel(mem_ref, g_ref, wkv_ref, k_ref, v_ref):
    mn = _rmsnorm(mem_ref[...], g_ref[...]).astype(BF16)
    kv = jnp.dot(mn, wkv_ref[...], preferred_element_type=F32)
    k_ref[...] = kv[:, 0:D_MODEL].astype(BF16)
    v_ref[...] = kv[:, D_MODEL:2 * D_MODEL].astype(BF16)


def _kv_proj(mem2d, g, wkv):
    n = mem2d.shape[0]
    rows = min(KV_ROWS, n)
    return pl.pallas_call(
        _kv_kernel,
        grid=(n // rows,),
        in_specs=[pl.BlockSpec((rows, D_MODEL), lambda i: (i, 0)),
                  _const_spec(g.shape), _const_spec(wkv.shape)],
        out_specs=[pl.BlockSpec((rows, D_MODEL), lambda i: (i, 0))] * 2,
        out_shape=[jax.ShapeDtypeStruct((n, D_MODEL), BF16)] * 2,
        compiler_params=_params("arbitrary"),
        name="kv_proj",
    )(mem2d, g, wkv)


def _side_cast_specs(side, n_steps):
    rows = side.shape[0] // n_steps
    assert side.shape[0] % n_steps == 0 and rows % (2 * SUBLANES) == 0
    spec = pl.BlockSpec((rows, side.shape[1]), lambda i: (i, 0))
    return spec, spec, jax.ShapeDtypeStruct(side.shape, BF16)


def _xattn_kernel(h_ref, g_ref, wq_ref, k_ref, v_ref, wo_ref, side_ref, o_ref, side_out_ref):
    side_out_ref[...] = side_ref[...].astype(BF16)
    h = h_ref[...]
    xn = _rmsnorm(h, g_ref[...]).astype(BF16)
    q = (jnp.dot(xn, wq_ref[...], preferred_element_type=F32) * (X_HEAD_DIM ** -0.5)).astype(BF16)
    heads = []
    for hd in range(X_HEADS):
        cols = slice(hd * X_HEAD_DIM, (hd + 1) * X_HEAD_DIM)
        s = lax.dot_general(q[:, cols], k_ref[:, cols], (((1,), (1,)), ((), ())),
                            preferred_element_type=F32)
        e = jnp.exp(s - jnp.max(s, axis=-1, keepdims=True))
        pv = jnp.dot(e.astype(BF16), v_ref[:, cols], preferred_element_type=F32)
        heads.append(pv / jnp.sum(e, axis=-1, keepdims=True))
    o = jnp.concatenate(heads, axis=1).astype(BF16)
    o_ref[...] = h + jnp.dot(o, wo_ref[...], preferred_element_type=F32)


def _xattn(h, k, v, g, wq, wo, seq_len, mem_len, side):
    n_tok = h.shape[0]
    ts = TILE_ATT
    tiles_per_seq = seq_len // ts
    side_in, side_out, side_shape = _side_cast_specs(side, n_tok // ts)
    return pl.pallas_call(
        _xattn_kernel,
        grid=(n_tok // ts,),
        in_specs=[pl.BlockSpec((ts, D_MODEL), lambda i: (i, 0)),
                  _const_spec(g.shape), _const_spec(wq.shape),
                  pl.BlockSpec((mem_len, D_MODEL), lambda i: (i // tiles_per_seq, 0)),
                  pl.BlockSpec((mem_len, D_MODEL), lambda i: (i // tiles_per_seq, 0)),
                  _const_spec(wo.shape), side_in],
        out_specs=[pl.BlockSpec((ts, D_MODEL), lambda i: (i, 0)), side_out],
        out_shape=[jax.ShapeDtypeStruct((n_tok, D_MODEL), F32), side_shape],
        compiler_params=_params("arbitrary"),
        name="xattn",
    )(h, g, wq, k, v, wo, side)


def _ffn_chunks(d_ff):
    return [(c0, min(c0 + FF_SUBCHUNK, d_ff)) for c0 in range(0, d_ff, FF_SUBCHUNK)]


def _ffn_kernel(h_ref, g_ref, wg_ref, wu_ref, wd_ref, side_ref, o_ref, side_out_ref, *, chunks):
    side_out_ref[...] = side_ref[...].astype(BF16)
    h = h_ref[...]
    xn = _rmsnorm(h, g_ref[...]).astype(BF16)
    acc = h
    for c0, c1 in chunks:
        gate = jnp.dot(xn, wg_ref[:, c0:c1], preferred_element_type=F32)
        up = jnp.dot(xn, wu_ref[:, c0:c1], preferred_element_type=F32)
        act = (jax.nn.silu(gate) * up).astype(BF16)
        acc = acc + jnp.dot(act, wd_ref[c0:c1, :], preferred_element_type=F32)
    o_ref[...] = acc


def _ffn(h, g, wg, wu, wd, side):
    n_tok = h.shape[0]
    ts = TILE_FFN
    kern = functools.partial(_ffn_kernel, chunks=_ffn_chunks(wg.shape[1]))
    side_in, side_out, side_shape = _side_cast_specs(side, n_tok // ts)
    return pl.pallas_call(
        kern,
        grid=(n_tok // ts,),
        in_specs=[pl.BlockSpec((ts, D_MODEL), lambda i: (i, 0)),
                  _const_spec(g.shape), _const_spec(wg.shape),
                  _const_spec(wu.shape), _const_spec(wd.shape), side_in],
        out_specs=[pl.BlockSpec((ts, D_MODEL), lambda i: (i, 0)), side_out],
        out_shape=[jax.ShapeDtypeStruct((n_tok, D_MODEL), F32), side_shape],
        compiler_params=_params("arbitrary"),
        name="ffn",
    )(h, g, wg, wu, wd, side)


META_IDX, META_RANK, META_W = 0, 2, 4


ROUTE_ROWS = 2 * SUBLANES


def _router_kernel(h_ref, g_ref, wr_ref, tri_ref, meta_ref, fields_ref, cnt_ref, run_ref):
    @pl.when(pl.program_id(0) == 0)
    def _():
        run_ref[...] = jnp.zeros_like(run_ref)

    hn = _rmsnorm(h_ref[...], g_ref[...])
    hn_hi = hn.astype(BF16)
    hn_lo = (hn - hn_hi.astype(F32)).astype(BF16)
    both = jnp.dot(hn_hi, wr_ref[...], preferred_element_type=F32)
    logits = (both[:, 0:LANES] + both[:, LANES:2 * LANES]
              + jnp.dot(hn_lo, wr_ref[:, 0:LANES], preferred_element_type=F32))
    lt = logits.T[0:ROUTE_ROWS, :]
    row = lax.broadcasted_iota(jnp.int32, lt.shape, 0).astype(F32)
    neg_inf = jnp.float32(-jnp.inf)
    l1 = jnp.where(row < N_EXPERTS, lt, neg_inf)
    m1 = jnp.max(l1, axis=0, keepdims=True)
    i1 = jnp.min(jnp.where(l1 == m1, row, float(ROUTE_ROWS)), axis=0, keepdims=True)
    l2 = jnp.where(row == i1, neg_inf, l1)
    m2 = jnp.max(l2, axis=0, keepdims=True)
    i2 = jnp.min(jnp.where(l2 == m2, row, float(ROUTE_ROWS)), axis=0, keepdims=True)
    e2 = jnp.exp(m2 - m1)
    w1 = 1.0 / (1.0 + e2)
    w2 = e2 / (1.0 + e2)
    sel1 = row == i1
    sel2 = row == i2
    onehot = jnp.where(sel1 | sel2, 1.0, 0.0)
    before = jnp.dot(onehot.astype(BF16), tri_ref[...], preferred_element_type=F32)
    run = run_ref[:, 0:1]
    rank = before + run
    r1 = jnp.sum(jnp.where(sel1, rank, 0.0), axis=0, keepdims=True)
    r2 = jnp.sum(jnp.where(sel2, rank, 0.0), axis=0, keepdims=True)
    total = run + jnp.sum(onehot, axis=1, keepdims=True)
    run_ref[...] = jnp.broadcast_to(total, run_ref.shape)
    cnt_ref[...] = jnp.broadcast_to(total, cnt_ref.shape)
    zero = jnp.zeros_like(w1)
    fields = jnp.concatenate([i1, i2, r1, r2, w1, w2, zero, zero], axis=0)
    fields_ref[...] = fields
    padded = jnp.concatenate([fields, jnp.zeros((LANES - SUBLANES, fields.shape[1]), F32)], axis=0)
    meta_ref[...] = padded.T


def _router(h, g, wr_pad, tri):
    n_tok = h.shape[0]
    ts = TILE_ROUTE
    return pl.pallas_call(
        _router_kernel,
        grid=(n_tok // ts,),
        in_specs=[pl.BlockSpec((ts, D_MODEL), lambda i: (i, 0)),
                  _const_spec(g.shape), _const_spec(wr_pad.shape), _const_spec(tri.shape)],
        out_specs=[pl.BlockSpec((ts, LANES), lambda i: (i, 0)),
                   pl.BlockSpec((SUBLANES, ts), lambda i: (0, i)),
                   pl.BlockSpec((ROUTE_ROWS, LANES), lambda i: (0, 0))],
        out_shape=[jax.ShapeDtypeStruct((n_tok, LANES), F32),
                   jax.ShapeDtypeStruct((SUBLANES, n_tok), F32),
                   jax.ShapeDtypeStruct((ROUTE_ROWS, LANES), F32)],
        scratch_shapes=[pltpu.VMEM((ROUTE_ROWS, LANES), F32)],
        compiler_params=_params("arbitrary"),
        name="router",
    )(h, g, wr_pad, tri)


def _rows_to_tiles(x, dst):
    n = x.shape[0]
    for g in range(n // SUBLANES):
        for j in range(ROW_TILE):
            dst[pl.ds(g * SUBLANES * ROW_TILE + j, SUBLANES, stride=ROW_TILE), :] = (
                x[g * SUBLANES:(g + 1) * SUBLANES, j * LANES:(j + 1) * LANES])


def _tiles_to_rows(src, n):
    groups = []
    for g in range(n // SUBLANES):
        groups.append(jnp.concatenate(
            [src[pl.ds(g * SUBLANES * ROW_TILE + j, SUBLANES, stride=ROW_TILE), :]
             for j in range(ROW_TILE)], axis=1))
    return jnp.concatenate(groups, axis=0)


def _tile_rows(ref, row):
    return ref.at[pl.ds(pl.multiple_of(row * ROW_TILE, ROW_TILE), ROW_TILE), :]


def _wait_rows(buf, sem):
    pltpu.make_async_copy(buf, buf, sem).wait()


def _dispatch_kernel(pos0_ref, pos1_ref, fill_ref, h_ref, g_ref, xs_ref, hn_buf, zero_buf, sems,
                     fill_sem, *, ts, tm):
    step = pl.program_id(0)
    slot = lax.rem(step, 2)

    @pl.when(step == 0)
    def _():
        zero_buf[...] = jnp.zeros_like(zero_buf)
        for k in range(2 * N_EXPERTS):
            @pl.when(fill_ref[k] >= 0)
            def _():
                cp = pltpu.make_async_copy(
                    zero_buf, xs_ref.at[pl.ds(fill_ref[k] * (tm * ROW_TILE), tm * ROW_TILE), :],
                    fill_sem)
                cp.start()
                cp.wait()

    rows = hn_buf.at[slot]
    _rows_to_tiles(_rmsnorm(h_ref[...], g_ref[...]), rows)

    def issue(q, carry):
        for l in range(LANES):
            src = _tile_rows(rows, q * LANES + l)
            for k, table in enumerate((pos0_ref, pos1_ref)):
                pltpu.make_async_copy(src, _tile_rows(xs_ref, table[q, l]),
                                      sems.at[slot, k]).start(priority=k)
        return carry

    lax.fori_loop(0, ts // LANES, issue, 0)

    @pl.when(step > 0)
    def _():
        for k in range(2):
            _wait_rows(hn_buf.at[1 - slot], sems.at[1 - slot, k])

    @pl.when(step == pl.num_programs(0) - 1)
    def _():
        for k in range(2):
            _wait_rows(rows, sems.at[slot, k])


def _dispatch(h, g, pos_tiles, fill_tiles, n_rows, tm):
    n_tok = h.shape[0]
    n_steps = pos_tiles[0].shape[0]
    ts = n_tok // n_steps
    kern = functools.partial(_dispatch_kernel, ts=ts, tm=tm)
    pos_spec = pl.BlockSpec((None,) + pos_tiles[0].shape[1:], lambda i: (i, 0, 0),
                            memory_space=pltpu.SMEM)
    return pl.pallas_call(
        kern,
        grid=(n_steps,),
        in_specs=[pos_spec, pos_spec,
                  pl.BlockSpec(memory_space=pltpu.SMEM),
                  pl.BlockSpec((ts, D_MODEL), lambda i: (i, 0)),
                  _const_spec(g.shape)],
        out_specs=pl.BlockSpec(memory_space=pl.ANY),
        out_shape=jax.ShapeDtypeStruct((n_rows * ROW_TILE, LANES), F32),
        scratch_shapes=[pltpu.VMEM((2, ts * ROW_TILE, LANES), F32),
                        pltpu.VMEM((tm * ROW_TILE, LANES), F32),
                        pltpu.SemaphoreType.DMA((2, 2)),
                        pltpu.SemaphoreType.DMA(())],
        compiler_params=_params("arbitrary"),
        name="dispatch",
    )(*pos_tiles, fill_tiles, h, g)


def _expert_kernel(te_ref, na_ref, x_ref, wg_ref, wu_ref, wd_ref, o_ref, *acc, tm, n_f):
    del te_ref
    acc_ref = acc[0] if acc else None
    f = pl.program_id(1)
    active = pl.program_id(0) < na_ref[0]

    @pl.when(jnp.logical_and(jnp.logical_not(active), f == 0))
    def _():
        o_ref[...] = jnp.zeros_like(o_ref)

    def swiglu_chunk(first, last):
        x = _tiles_to_rows(x_ref, tm).astype(BF16)
        part = None if first else acc_ref[...]
        for c0 in range(0, FF_CHUNK, FF_SUBCHUNK):
            c1 = min(c0 + FF_SUBCHUNK, FF_CHUNK)
            gate = jnp.dot(x, wg_ref[:, c0:c1], preferred_element_type=F32)
            up = jnp.dot(x, wu_ref[:, c0:c1], preferred_element_type=F32)
            act = (jax.nn.silu(gate) * up).astype(BF16)
            down = jnp.dot(act, wd_ref[c0:c1, :], preferred_element_type=F32)
            part = down if part is None else part + down
        if last:
            _rows_to_tiles(part, o_ref)
        else:
            acc_ref[...] = part

    if n_f == 1:
        pl.when(active)(functools.partial(swiglu_chunk, True, True))
        return
    pl.when(jnp.logical_and(active, f == 0))(functools.partial(swiglu_chunk, True, False))
    pl.when(jnp.logical_and(active, f == n_f - 1))(functools.partial(swiglu_chunk, False, True))
    if n_f > 2:
        pl.when(jnp.logical_and(active, jnp.logical_and(f > 0, f < n_f - 1)))(
            functools.partial(swiglu_chunk, False, False))


def _experts(xs, wg, wu, wd, tile_expert, n_active, tm):
    d_exp = wg.shape[2]
    n_f = d_exp // FF_CHUNK
    n_tiles = xs.shape[0] // (tm * ROW_TILE)

    def x_tile(i, na):
        return jnp.minimum(i, na[0] - 1)

    def chunk(i, f, na):
        return jnp.where(i < na[0], f, n_f - 1)

    grid_spec = pltpu.PrefetchScalarGridSpec(
        num_scalar_prefetch=2,
        grid=(n_tiles, n_f),
        in_specs=[
            pl.BlockSpec((tm * ROW_TILE, LANES), lambda i, f, te, na: (x_tile(i, na), 0)),
            pl.BlockSpec((None, D_MODEL, FF_CHUNK), lambda i, f, te, na: (te[i], 0, chunk(i, f, na))),
            pl.BlockSpec((None, D_MODEL, FF_CHUNK), lambda i, f, te, na: (te[i], 0, chunk(i, f, na))),
            pl.BlockSpec((None, FF_CHUNK, D_MODEL), lambda i, f, te, na: (te[i], chunk(i, f, na), 0)),
        ],
        out_specs=pl.BlockSpec((tm * ROW_TILE, LANES), lambda i, f, te, na: (i, 0)),
        scratch_shapes=[pltpu.VMEM((tm, D_MODEL), F32)] if n_f > 1 else [],
    )
    return pl.pallas_call(
        functools.partial(_expert_kernel, tm=tm, n_f=n_f),
        grid_spec=grid_spec,
        out_shape=jax.ShapeDtypeStruct(xs.shape, F32),
        compiler_params=_params("arbitrary", "arbitrary"),
        name="experts",
    )(tile_expert, n_active, xs, wg, wu, wd)


def _combine_kernel(pos0_ref, pos1_ref, next_pos0_ref, next_pos1_ref, h_ref, meta_ref, g_ref, ys_ref,
                    o_ref, bufs, sems, *, ts):
    step = pl.program_id(0)
    slot = lax.rem(step, 2)

    def gather(tables, dst_slot):
        def issue(q, carry):
            for l in range(LANES):
                for k, table in enumerate(tables):
                    dst = _tile_rows(bufs.at[dst_slot, k], q * LANES + l)
                    pltpu.make_async_copy(_tile_rows(ys_ref, table[q, l]), dst,
                                          sems.at[dst_slot, k]).start(priority=k)
            return carry

        lax.fori_loop(0, ts // LANES, issue, 0)

    @pl.when(step == 0)
    def _():
        gather((pos0_ref, pos1_ref), slot)

    @pl.when(step + 1 < pl.num_programs(0))
    def _():
        gather((next_pos0_ref, next_pos1_ref), 1 - slot)

    for k in range(2):
        _wait_rows(bufs.at[slot, k], sems.at[slot, k])
    meta = meta_ref[...]
    w1 = meta[:, META_W:META_W + 1]
    w2 = meta[:, META_W + 1:META_W + 2]
    y = h_ref[...] + (w1 * _tiles_to_rows(bufs.at[slot, 0], ts)
                      + w2 * _tiles_to_rows(bufs.at[slot, 1], ts))
    o_ref[...] = _rmsnorm(y, g_ref[...])


def _combine(h, meta, g, ys, pos_tiles):
    n_tok = h.shape[0]
    n_steps = pos_tiles[0].shape[0]
    ts = n_tok // n_steps
    kern = functools.partial(_combine_kernel, ts=ts)
    pos_block = (None,) + pos_tiles[0].shape[1:]
    pos_spec = pl.BlockSpec(pos_block, lambda i: (i, 0, 0), memory_space=pltpu.SMEM)
    next_spec = pl.BlockSpec(pos_block, lambda i: (jnp.minimum(i + 1, n_steps - 1), 0, 0),
                             memory_space=pltpu.SMEM)
    return pl.pallas_call(
        kern,
        grid=(n_steps,),
        in_specs=[pos_spec, pos_spec, next_spec, next_spec,
                  pl.BlockSpec((ts, D_MODEL), lambda i: (i, 0)),
                  pl.BlockSpec((ts, LANES), lambda i: (i, 0)),
                  _const_spec(g.shape),
                  pl.BlockSpec(memory_space=pl.ANY)],
        out_specs=pl.BlockSpec((ts, D_MODEL), lambda i: (i, 0)),
        out_shape=jax.ShapeDtypeStruct((n_tok, D_MODEL), F32),
        scratch_shapes=[pltpu.VMEM((2, 2, ts * ROW_TILE, LANES), F32),
                        pltpu.SemaphoreType.DMA((2, 2))],
        compiler_params=_params("arbitrary"),
        name="combine",
    )(*pos_tiles, *pos_tiles, h, meta, g, ys)


def _pos_tiles(pos, ts):
    return [p.reshape(p.shape[0] // ts, ts // LANES, LANES) for p in pos]


def _moe(h, norm_g, wr_pad, tri, wg, wu, wd, final_g):
    n_tok = h.shape[0]
    tm = TILE_EXPERT
    meta, fields, cnt = _router(h, norm_g, wr_pad, tri)

    counts = cnt[:N_EXPERTS, 0].astype(jnp.int32)
    tiles = (counts + (tm - 1)) // tm
    tile_end = jnp.cumsum(tiles)
    tile_start = tile_end - tiles
    n_tiles = (2 * n_tok) // tm + N_EXPERTS
    n_rows = n_tiles * tm

    def slots(k):
        idx = fields[META_IDX + k].astype(jnp.int32)
        rank = fields[META_RANK + k].astype(jnp.int32)
        first_tile = jnp.sum(jnp.where(idx[:, None] == jnp.arange(N_EXPERTS)[None, :],
                                       tile_start[None, :], 0), axis=1)
        return first_tile * tm + rank

    pos = [slots(0), slots(1)]
    n_active = tile_end[-1:].astype(jnp.int32)
    tail_tiles = jnp.where(tiles > 0, tile_end - 1, -1)
    slack_tiles = n_active[0] + jnp.arange(N_EXPERTS)
    slack_tiles = jnp.where(slack_tiles < n_tiles, slack_tiles, -1)
    fill_tiles = jnp.concatenate([tail_tiles, slack_tiles]).astype(jnp.int32)
    t = jnp.minimum(jnp.arange(n_tiles, dtype=jnp.int32), n_active[0] - 1)
    tile_expert = jnp.sum(t[:, None] >= tile_end[None, :], axis=1).astype(jnp.int32)

    xs = _dispatch(h, norm_g, _pos_tiles(pos, TILE_DISPATCH), fill_tiles, n_rows, tm)
    ys = _experts(xs, wg, wu, wd, tile_expert, n_active, tm)
    return _combine(h, meta, final_g, ys, _pos_tiles(pos, TILE_COMBINE))


def _pad_last(a, width):
    return jnp.pad(a, [(0, 0)] * (a.ndim - 1) + [(0, width - a.shape[-1])])


def _row(a):
    return a.reshape(1, -1)


def _router_weight(w):
    w_hi = w.astype(BF16)
    w_lo = (w - w_hi.astype(F32)).astype(BF16)
    return jnp.concatenate([_pad_last(w_hi, LANES), _pad_last(w_lo, LANES)], axis=1)


def _mixer_params(norm_g, w_in, conv_a_w, conv_a_b, ln_a_g, ln_a_b, conv_b_w, ln_c_g, ln_c_b,
                  gmlp_ws, gmlp_b, mix_out_g, w_mix_out):
    widths = [A_WIDTH, A_WIDTH, B_WIDTH, B_WIDTH, B_WIDTH, C_WIDTH, C_WIDTH]
    bounds = [0]
    for w in widths:
        bounds.append(bounds[-1] + w)
    w_in_p = jnp.concatenate(
        [_pad_last(w_in[:, bounds[s]:bounds[s + 1]], SEG) for s in range(N_IN_SEG)], axis=1)
    out_bounds = [0, A_WIDTH, A_WIDTH + B_WIDTH, A_WIDTH + B_WIDTH + C_WIDTH]
    out_g = jnp.concatenate(
        [_pad_last(mix_out_g[out_bounds[s]:out_bounds[s + 1]], SEG) for s in range(3)])
    w_out_p = jnp.concatenate(
        [jnp.pad(w_mix_out[out_bounds[s]:out_bounds[s + 1]],
                 [(0, SEG - (out_bounds[s + 1] - out_bounds[s])), (0, 0)]) for s in range(3)], axis=0)
    gbias = _pad_last(jnp.repeat(gmlp_b.T, HEAD_GROUP, axis=1), SEG)
    group = jnp.arange(MXU_DIM) // HEAD_GROUP
    gsum = ((group[:, None] == group[None, :]) * (1.0 / HEAD_GROUP)).astype(BF16)
    return {
        "norm_g": _row(norm_g), "w_in": w_in_p.astype(BF16),
        "conv_a_w": jnp.repeat(conv_a_w, SUBLANES, axis=0),
        "conv_a_b": jnp.broadcast_to(conv_a_b, (SUBLANES, A_WIDTH)),
        "ln_a_g": _row(ln_a_g), "ln_a_b": _row(ln_a_b),
        "conv_b_w": _pad_last(conv_b_w, SEG),
        "ln_c_g": _row(_pad_last(ln_c_g, SEG)), "ln_c_b": _row(_pad_last(ln_c_b, SEG)),
        "gmlp_ws": gmlp_ws.astype(BF16), "gmlp_bias": gbias,
        "out_g": _row(out_g), "w_out": w_out_p.astype(BF16), "gsum": gsum,
    }


def kernel(x, mem, norm_mix_g, w_in, conv_a_w, conv_a_b, ln_a_g, ln_a_b, conv_b_w, ln_c_g, ln_c_b,
           gmlp_ws, gmlp_b, mix_out_g, w_mix_out, norm_x_g, norm_mem_g, w_xq, w_xkv, w_xo,
           norm_ffn_g, ffn_w_gate, ffn_w_up, ffn_w_down, moe_router, moe_w_gate, moe_w_up,
           moe_w_down, norm_final_g):
    bsz, seq_len, _ = x.shape
    mem_len = mem.shape[1]
    depth = w_in.shape[0]
    assert depth == 2 and ffn_w_gate.shape[0] == 1 and moe_router.shape[0] == 1
    assert seq_len % TILE_MIX == 0 and seq_len % TILE_ATT == 0

    h = x.reshape(bsz * seq_len, D_MODEL)
    mem2d = mem.reshape(bsz * mem_len, D_MODEL)
    tri = (jnp.arange(TILE_ROUTE)[:, None] < jnp.arange(TILE_ROUTE)[None, :]).astype(BF16)
    n_exp, _, d_exp = moe_w_gate.shape[1:]
    sides = [moe_w_gate[0].reshape(n_exp * D_MODEL, d_exp), moe_w_up[0].reshape(n_exp * D_MODEL, d_exp),
             moe_w_down[0].reshape(n_exp * d_exp, D_MODEL)]
    casted = []

    def attend(h, layer):
        k, v = _kv_proj(mem2d, _row(norm_mem_g[layer]), w_xkv[layer].astype(BF16))
        h, done = _xattn(h, k, v, _row(norm_x_g[layer]), w_xq[layer].astype(BF16),
                         w_xo[layer].astype(BF16), seq_len, mem_len, sides[len(casted)])
        casted.append(done)
        return h

    def mix(h, layer):
        mp = _mixer_params(norm_mix_g[layer], w_in[layer], conv_a_w[layer], conv_a_b[layer],
                           ln_a_g[layer], ln_a_b[layer], conv_b_w[layer], ln_c_g[layer],
                           ln_c_b[layer], gmlp_ws[layer], gmlp_b[layer], mix_out_g[layer],
                           w_mix_out[layer])
        return _mixer(h, mp, seq_len)

    h = attend(mix(h, 0), 0)
    h, done = _ffn(h, _row(norm_ffn_g[0]), ffn_w_gate[0].astype(BF16), ffn_w_up[0].astype(BF16),
                   ffn_w_down[0].astype(BF16), sides[len(casted)])
    casted.append(done)
    h = attend(mix(h, 1), 1)
    out = _moe(h, _row(norm_ffn_g[1]), _router_weight(moe_router[0]), tri,
               casted[0].reshape(n_exp, D_MODEL, d_exp), casted[1].reshape(n_exp, D_MODEL, d_exp),
               casted[2].reshape(n_exp, d_exp, D_MODEL), _row(norm_final_g))
    return out.reshape(bsz, seq_len, D_MODEL)
```

```python
import functools

import jax
import jax.numpy as jnp
from jax import lax
from jax.experimental import pallas as pl
from jax.experimental.pallas import tpu as pltpu

F32 = jnp.float32
BF16 = jnp.bfloat16

D_MODEL = 1024
EPS = 1e-6
CHUNK = 64
HEAD_GROUP = 64
A_WIDTH, B_WIDTH, C_WIDTH = 384, 320, 320
A_KERNEL, B_KERNEL = 31, 3
GMLP_BLOCK = 128
C_GROUPS = 5
X_HEADS = 4
X_HEAD_DIM = D_MODEL // X_HEADS
N_EXPERTS = 8

LANES = 128
SUBLANES = 8
MXU_DIM = 256
VMEM_LIMIT_BYTES = 56 * 1024 * 1024

SEG = 384
N_IN_SEG = 7
MIX_PAD = 3 * SEG
HALO_A = 32
HALO_B = SUBLANES
CONV_ROWS = 32
EDGE_PARTS = 2
SHIFT_EXTRA = HALO_A - SUBLANES

TILE_MIX = 512
TILE_ATT = 1024
TILE_FFN = 1024
TILE_ROUTE = 1024
TILE_DISPATCH = 1024
TILE_COMBINE = 256
TILE_EXPERT = 512
FF_CHUNK = 3584
FF_SUBCHUNK = 256
KV_ROWS = 1024
ROW_TILE = 8


def _rmsnorm(x, g):
    ms = jnp.mean(x * x, axis=-1, keepdims=True)
    return x * lax.rsqrt(ms + EPS) * g


def _const_spec(shape):
    zeros = (0,) * len(shape)
    return pl.BlockSpec(shape, lambda *_: zeros, pipeline_mode=pl.Buffered(1))


def _params(*semantics):
    return pltpu.CompilerParams(dimension_semantics=semantics,
                                vmem_limit_bytes=VMEM_LIMIT_BYTES)


def _mixer_kernel(h_ref, ng_ref, win_ref, caw_ref, cab_ref, lag_ref, lab_ref, cbw_ref,
                  lcg_ref, lcb_ref, ws_ref, gbias_ref, og_ref, wout_ref, gsum_ref,
                  o_ref, zbuf, abuf, bbuf, shifted, ybuf, *, ts, tiles_per_seq):
    seq_tile = lax.rem(pl.program_id(0), tiles_per_seq)

    @pl.when(seq_tile == 0)
    def _():
        abuf[0:HALO_A, :] = jnp.zeros((HALO_A, SEG), F32)
        bbuf[0:HALO_B, :] = jnp.zeros((HALO_B, SEG), F32)

    @pl.when(seq_tile > 0)
    def _():
        abuf[0:HALO_A, :] = abuf[ts:ts + HALO_A, :]
        bbuf[0:HALO_B, :] = bbuf[ts:ts + HALO_B, :]

    h = h_ref[...]
    xn = _rmsnorm(h, ng_ref[...]).astype(BF16)

    part = ts // EDGE_PARTS
    for r0 in range(0, ts, part):
        z_a = jnp.dot(xn[r0:r0 + part, :], win_ref[:, 0:2 * SEG], preferred_element_type=F32)
        abuf[HALO_A + r0:HALO_A + r0 + part, :] = z_a[:, 0:SEG] * jax.nn.sigmoid(z_a[:, SEG:2 * SEG])
        lo = 0 if r0 == 0 else r0 + SHIFT_EXTRA
        hi = r0 + part + SHIFT_EXTRA
        for phase in range(1, SUBLANES):
            shifted[phase - 1, lo:hi, :] = abuf[lo + phase:hi + phase, :]

    zbuf[...] = jnp.dot(xn, win_ref[:, 2 * SEG:N_IN_SEG * SEG], preferred_element_type=F32)

    for r in range(0, ts, CONV_ROWS):
        acc = jnp.concatenate([cab_ref[...]] * (CONV_ROWS // SUBLANES), axis=0)
        for k in range(A_KERNEL):
            groups, phase = divmod(HALO_A - A_KERNEL + 1 + k, SUBLANES)
            start = r + groups * SUBLANES
            if phase == 0:
                window = abuf[start:start + CONV_ROWS, :]
            else:
                window = shifted[phase - 1, start:start + CONV_ROWS, :]
            tap = caw_ref[k * SUBLANES:(k + 1) * SUBLANES, :]
            acc = acc + jnp.concatenate([tap] * (CONV_ROWS // SUBLANES), axis=0) * window
        mu = jnp.mean(acc, axis=-1, keepdims=True)
        xc = acc - mu
        var = jnp.mean(xc * xc, axis=-1, keepdims=True)
        ybuf[r:r + CONV_ROWS, 0:SEG] = jax.nn.silu(
            xc * lax.rsqrt(var + EPS) * lag_ref[...] + lab_ref[...])

    bbuf[HALO_B:HALO_B + ts, :] = zbuf[:, 1 * SEG:2 * SEG] * zbuf[:, 2 * SEG:3 * SEG]
    accb = jnp.zeros((ts, SEG), F32)
    for k in range(B_KERNEL):
        first = HALO_B - B_KERNEL + 1 + k
        accb = accb + cbw_ref[k:k + 1, :] * bbuf[first:first + ts, :]
    ybuf[:, 1 * SEG:2 * SEG] = zbuf[:, 0:SEG] * accb

    c_valid = lax.broadcasted_iota(jnp.int32, (1, SEG), 1) < C_WIDTH
    low_group = lax.broadcasted_iota(jnp.int32, (GMLP_BLOCK, LANES), 1) < HEAD_GROUP
    row_chunk = lax.broadcasted_iota(jnp.int32, (GMLP_BLOCK, GMLP_BLOCK), 0) // CHUNK
    col_chunk = lax.broadcasted_iota(jnp.int32, (GMLP_BLOCK, GMLP_BLOCK), 1) // CHUNK
    chunk_causal = row_chunk >= col_chunk
    ws = [jnp.where(chunk_causal, ws_ref[g], jnp.zeros((), BF16)) for g in range(C_GROUPS)]
    for r in range(0, ts, GMLP_BLOCK):
        rows = slice(r, r + GMLP_BLOCK)
        c_u = jax.nn.gelu(zbuf[rows, 3 * SEG:4 * SEG])
        c_v = jax.nn.gelu(zbuf[rows, 4 * SEG:5 * SEG])
        mu = jnp.sum(c_v, axis=-1, keepdims=True) * (1.0 / C_WIDTH)
        xc = jnp.where(c_valid, c_v - mu, 0.0)
        var = jnp.sum(xc * xc, axis=-1, keepdims=True) * (1.0 / C_WIDTH)
        v = (xc * lax.rsqrt(var + EPS) * lcg_ref[...] + lcb_ref[...]).astype(BF16)
        cols = []
        for j in range(SEG // LANES):
            vj = v[:, j * LANES:(j + 1) * LANES]
            col = jnp.dot(ws[2 * j], vj, preferred_element_type=F32)
            if 2 * j + 1 < C_GROUPS:
                col = jnp.where(low_group, col, jnp.dot(ws[2 * j + 1], vj, preferred_element_type=F32))
            cols.append(col)
        mixed = jnp.concatenate(cols, axis=1) + gbias_ref[...]
        ybuf[rows, 2 * SEG:3 * SEG] = c_u * mixed

    gsum = gsum_ref[...]
    for r0 in range(0, ts, part):
        y = ybuf[r0:r0 + part, :]
        y2 = (y * y).astype(BF16)
        sums = []
        for c0 in range(0, MIX_PAD, MXU_DIM):
            width = min(MXU_DIM, MIX_PAD - c0)
            sums.append(jnp.dot(y2[:, c0:c0 + width], gsum[0:width, 0:width],
                                preferred_element_type=F32))
        ms = jnp.concatenate(sums, axis=1)
        yn = (y * lax.rsqrt(ms + EPS) * og_ref[...]).astype(BF16)
        o_ref[r0:r0 + part, :] = h[r0:r0 + part, :] + jnp.dot(yn, wout_ref[...], preferred_element_type=F32)


def _mixer(h, p, seq_len):
    n_tok = h.shape[0]
    ts = TILE_MIX
    tiles_per_seq = seq_len // ts
    kern = functools.partial(_mixer_kernel, ts=ts, tiles_per_seq=tiles_per_seq)
    consts = [p["norm_g"], p["w_in"], p["conv_a_w"], p["conv_a_b"], p["ln_a_g"], p["ln_a_b"],
              p["conv_b_w"], p["ln_c_g"], p["ln_c_b"], p["gmlp_ws"], p["gmlp_bias"],
              p["out_g"], p["w_out"], p["gsum"]]
    return pl.pallas_call(
        kern,
        grid=(n_tok // ts,),
        in_specs=[pl.BlockSpec((ts, D_MODEL), lambda i: (i, 0))]
                 + [_const_spec(c.shape) for c in consts],
        out_specs=pl.BlockSpec((ts, D_MODEL), lambda i: (i, 0)),
        out_shape=jax.ShapeDtypeStruct((n_tok, D_MODEL), F32),
        scratch_shapes=[
            pltpu.VMEM((ts, (N_IN_SEG - 2) * SEG), F32),
            pltpu.VMEM((HALO_A + ts, SEG), F32),
            pltpu.VMEM((HALO_B + ts, SEG), F32),
            pltpu.VMEM((SUBLANES - 1, ts + SHIFT_EXTRA, SEG), F32),
            pltpu.VMEM((ts, MIX_PAD), F32),
        ],
        compiler_params=_params("arbitrary"),
        name="mixer",
    )(h, *consts)


def _kv_kernel(mem_ref, g_ref, wkv_ref, k_ref, v_ref):
    mn = _rmsnorm(mem_ref[...], g_ref[...]).astype(BF16)
    kv = jnp.dot(mn, wkv_ref[...], preferred_element_type=F32)
    k_ref[...] = kv[:, 0:D_MODEL].astype(BF16)
    v_ref[...] = kv[:, D_MODEL:2 * D_MODEL].astype(BF16)


def _kv_proj(mem2d, g, wkv):
    n = mem2d.shape[0]
    rows = min(KV_ROWS, n)
    return pl.pallas_call(
        _kv_kernel,
        grid=(n // rows,),
        in_specs=[pl.BlockSpec((rows, D_MODEL), lambda i: (i, 0)),
                  _const_spec(g.shape), _const_spec(wkv.shape)],
        out_specs=[pl.BlockSpec((rows, D_MODEL), lambda i: (i, 0))] * 2,
        out_shape=[jax.ShapeDtypeStruct((n, D_MODEL), BF16)] * 2,
        compiler_params=_params("arbitrary"),
        name="kv_proj",
    )(mem2d, g, wkv)


def _side_cast_specs(side, n_steps):
    rows = side.shape[0] // n_steps
    assert side.shape[0] % n_steps == 0 and rows % (2 * SUBLANES) == 0
    spec = pl.BlockSpec((rows, side.shape[1]), lambda i: (i, 0))
    return spec, spec, jax.ShapeDtypeStruct(side.shape, BF16)


def _xattn_kernel(h_ref, g_ref, wq_ref, k_ref, v_ref, wo_ref, side_ref, o_ref, side_out_ref):
    side_out_ref[...] = side_ref[...].astype(BF16)
    h = h_ref[...]
    xn = _rmsnorm(h, g_ref[...]).astype(BF16)
    q = (jnp.dot(xn, wq_ref[...], preferred_element_type=F32) * (X_HEAD_DIM ** -0.5)).astype(BF16)
    heads = []
    for hd in range(X_HEADS):
        cols = slice(hd * X_HEAD_DIM, (hd + 1) * X_HEAD_DIM)
        s = lax.dot_general(q[:, cols], k_ref[:, cols], (((1,), (1,)), ((), ())),
                            preferred_element_type=F32)
        e = jnp.exp(s - jnp.max(s, axis=-1, keepdims=True))
        pv = jnp.dot(e.astype(BF16), v_ref[:, cols], preferred_element_type=F32)
        heads.append(pv / jnp.sum(e, axis=-1, keepdims=True))
    o = jnp.concatenate(heads, axis=1).astype(BF16)
    o_ref[...] = h + jnp.dot(o, wo_ref[...], preferred_element_type=F32)


def _xattn(h, k, v, g, wq, wo, seq_len, mem_len, side):
    n_tok = h.shape[0]
    ts = TILE_ATT
    tiles_per_seq = seq_len // ts
    side_in, side_out, side_shape = _side_cast_specs(side, n_tok // ts)
    return pl.pallas_call(
        _xattn_kernel,
        grid=(n_tok // ts,),
        in_specs=[pl.BlockSpec((ts, D_MODEL), lambda i: (i, 0)),
                  _const_spec(g.shape), _const_spec(wq.shape),
                  pl.BlockSpec((mem_len, D_MODEL), lambda i: (i // tiles_per_seq, 0)),
                  pl.BlockSpec((mem_len, D_MODEL), lambda i: (i // tiles_per_seq, 0)),
                  _const_spec(wo.shape), side_in],
        out_specs=[pl.BlockSpec((ts, D_MODEL), lambda i: (i, 0)), side_out],
        out_shape=[jax.ShapeDtypeStruct((n_tok, D_MODEL), F32), side_shape],
        compiler_params=_params("arbitrary"),
        name="xattn",
    )(h, g, wq, k, v, wo, side)


def _ffn_chunks(d_ff):
    return [(c0, min(c0 + FF_SUBCHUNK, d_ff)) for c0 in range(0, d_ff, FF_SUBCHUNK)]


def _ffn_kernel(h_ref, g_ref, wg_ref, wu_ref, wd_ref, side_ref, o_ref, side_out_ref, *, chunks):
    side_out_ref[...] = side_ref[...].astype(BF16)
    h = h_ref[...]
    xn = _rmsnorm(h, g_ref[...]).astype(BF16)
    acc = h
    for c0, c1 in chunks:
        gate = jnp.dot(xn, wg_ref[:, c0:c1], preferred_element_type=F32)
        up = jnp.dot(xn, wu_ref[:, c0:c1], preferred_element_type=F32)
        act = (jax.nn.silu(gate) * up).astype(BF16)
        acc = acc + jnp.dot(act, wd_ref[c0:c1, :], preferred_element_type=F32)
    o_ref[...] = acc


def _ffn(h, g, wg, wu, wd, side):
    n_tok = h.shape[0]
    ts = TILE_FFN
    kern = functools.partial(_ffn_kernel, chunks=_ffn_chunks(wg.shape[1]))
    side_in, side_out, side_shape = _side_cast_specs(side, n_tok // ts)
    return pl.pallas_call(
        kern,
        grid=(n_tok // ts,),
        in_specs=[pl.BlockSpec((ts, D_MODEL), lambda i: (i, 0)),
                  _const_spec(g.shape), _const_spec(wg.shape),
                  _const_spec(wu.shape), _const_spec(wd.shape), side_in],
        out_specs=[pl.BlockSpec((ts, D_MODEL), lambda i: (i, 0)), side_out],
        out_shape=[jax.ShapeDtypeStruct((n_tok, D_MODEL), F32), side_shape],
        compiler_params=_params("arbitrary"),
        name="ffn",
    )(h, g, wg, wu, wd, side)


META_IDX, META_RANK, META_W = 0, 2, 4


ROUTE_ROWS = 2 * SUBLANES


def _router_kernel(h_ref, g_ref, wr_ref, tri_ref, meta_ref, fields_ref, cnt_ref, run_ref):
    @pl.when(pl.program_id(0) == 0)
    def _():
        run_ref[...] = jnp.zeros_like(run_ref)

    hn = _rmsnorm(h_ref[...], g_ref[...])
    hn_hi = hn.astype(BF16)
    hn_lo = (hn - hn_hi.astype(F32)).astype(BF16)
    both = jnp.dot(hn_hi, wr_ref[...], preferred_element_type=F32)
    logits = (both[:, 0:LANES] + both[:, LANES:2 * LANES]
              + jnp.dot(hn_lo, wr_ref[:, 0:LANES], preferred_element_type=F32))
    lt = logits.T[0:ROUTE_ROWS, :]
    row = lax.broadcasted_iota(jnp.int32, lt.shape, 0).astype(F32)
    neg_inf = jnp.float32(-jnp.inf)
    l1 = jnp.where(row < N_EXPERTS, lt, neg_inf)
    m1 = jnp.max(l1, axis=0, keepdims=True)
    i1 = jnp.min(jnp.where(l1 == m1, row, float(ROUTE_ROWS)), axis=0, keepdims=True)
    l2 = jnp.where(row == i1, neg_inf, l1)
    m2 = jnp.max(l2, axis=0, keepdims=True)
    i2 = jnp.min(jnp.where(l2 == m2, row, float(ROUTE_ROWS)), axis=0, keepdims=True)
    e2 = jnp.exp(m2 - m1)
    w1 = 1.0 / (1.0 + e2)
    w2 = e2 / (1.0 + e2)
    sel1 = row == i1
    sel2 = row == i2
    onehot = jnp.where(sel1 | sel2, 1.0, 0.0)
    before = jnp.dot(onehot.astype(BF16), tri_ref[...], preferred_element_type=F32)
    run = run_ref[:, 0:1]
    rank = before + run
    r1 = jnp.sum(jnp.where(sel1, rank, 0.0), axis=0, keepdims=True)
    r2 = jnp.sum(jnp.where(sel2, rank, 0.0), axis=0, keepdims=True)
    total = run + jnp.sum(onehot, axis=1, keepdims=True)
    run_ref[...] = jnp.broadcast_to(total, run_ref.shape)
    cnt_ref[...] = jnp.broadcast_to(total, cnt_ref.shape)
    zero = jnp.zeros_like(w1)
    fields = jnp.concatenate([i1, i2, r1, r2, w1, w2, zero, zero], axis=0)
    fields_ref[...] = fields
    padded = jnp.concatenate([fields, jnp.zeros((LANES - SUBLANES, fields.shape[1]), F32)], axis=0)
    meta_ref[...] = padded.T


def _router(h, g, wr_pad, tri):
    n_tok = h.shape[0]
    ts = TILE_ROUTE
    return pl.pallas_call(
        _router_kernel,
        grid=(n_tok // ts,),
        in_specs=[pl.BlockSpec((ts, D_MODEL), lambda i: (i, 0)),
                  _const_spec(g.shape), _const_spec(wr_pad.shape), _const_spec(tri.shape)],
        out_specs=[pl.BlockSpec((ts, LANES), lambda i: (i, 0)),
                   pl.BlockSpec((SUBLANES, ts), lambda i: (0, i)),
                   pl.BlockSpec((ROUTE_ROWS, LANES), lambda i: (0, 0))],
        out_shape=[jax.ShapeDtypeStruct((n_tok, LANES), F32),
                   jax.ShapeDtypeStruct((SUBLANES, n_tok), F32),
                   jax.ShapeDtypeStruct((ROUTE_ROWS, LANES), F32)],
        scratch_shapes=[pltpu.VMEM((ROUTE_ROWS, LANES), F32)],
        compiler_params=_params("arbitrary"),
        name="router",
    )(h, g, wr_pad, tri)


def _rows_to_tiles(x, dst):
    n = x.shape[0]
    for g in range(n // SUBLANES):
        for j in range(ROW_TILE):
            dst[pl.ds(g * SUBLANES * ROW_TILE + j, SUBLANES, stride=ROW_TILE), :] = (
                x[g * SUBLANES:(g + 1) * SUBLANES, j * LANES:(j + 1) * LANES])


def _tiles_to_rows(src, n):
    groups = []
    for g in range(n // SUBLANES):
        groups.append(jnp.concatenate(
            [src[pl.ds(g * SUBLANES * ROW_TILE + j, SUBLANES, stride=ROW_TILE), :]
             for j in range(ROW_TILE)], axis=1))
    return jnp.concatenate(groups, axis=0)


def _tile_rows(ref, row):
    return ref.at[pl.ds(pl.multiple_of(row * ROW_TILE, ROW_TILE), ROW_TILE), :]


def _wait_rows(buf, sem):
    pltpu.make_async_copy(buf, buf, sem).wait()


def _dispatch_kernel(pos0_ref, pos1_ref, fill_ref, h_ref, g_ref, xs_ref, hn_buf, zero_buf, sems,
                     fill_sem, *, ts, tm):
    step = pl.program_id(0)
    slot = lax.rem(step, 2)

    @pl.when(step == 0)
    def _():
        zero_buf[...] = jnp.zeros_like(zero_buf)
        for k in range(2 * N_EXPERTS):
            @pl.when(fill_ref[k] >= 0)
            def _():
                cp = pltpu.make_async_copy(
                    zero_buf, xs_ref.at[pl.ds(fill_ref[k] * (tm * ROW_TILE), tm * ROW_TILE), :],
                    fill_sem)
                cp.start()
                cp.wait()

    rows = hn_buf.at[slot]
    _rows_to_tiles(_rmsnorm(h_ref[...], g_ref[...]), rows)

    def issue(q, carry):
        for l in range(LANES):
            src = _tile_rows(rows, q * LANES + l)
            for k, table in enumerate((pos0_ref, pos1_ref)):
                pltpu.make_async_copy(src, _tile_rows(xs_ref, table[q, l]),
                                      sems.at[slot, k]).start(priority=k)
        return carry

    lax.fori_loop(0, ts // LANES, issue, 0)

    @pl.when(step > 0)
    def _():
        for k in range(2):
            _wait_rows(hn_buf.at[1 - slot], sems.at[1 - slot, k])

    @pl.when(step == pl.num_programs(0) - 1)
    def _():
        for k in range(2):
            _wait_rows(rows, sems.at[slot, k])


def _dispatch(h, g, pos_tiles, fill_tiles, n_rows, tm):
    n_tok = h.shape[0]
    n_steps = pos_tiles[0].shape[0]
    ts = n_tok // n_steps
    kern = functools.partial(_dispatch_kernel, ts=ts, tm=tm)
    pos_spec = pl.BlockSpec((None,) + pos_tiles[0].shape[1:], lambda i: (i, 0, 0),
                            memory_space=pltpu.SMEM)
    return pl.pallas_call(
        kern,
        grid=(n_steps,),
        in_specs=[pos_spec, pos_spec,
                  pl.BlockSpec(memory_space=pltpu.SMEM),
                  pl.BlockSpec((ts, D_MODEL), lambda i: (i, 0)),
                  _const_spec(g.shape)],
        out_specs=pl.BlockSpec(memory_space=pl.ANY),
        out_shape=jax.ShapeDtypeStruct((n_rows * ROW_TILE, LANES), F32),
        scratch_shapes=[pltpu.VMEM((2, ts * ROW_TILE, LANES), F32),
                        pltpu.VMEM((tm * ROW_TILE, LANES), F32),
                        pltpu.SemaphoreType.DMA((2, 2)),
                        pltpu.SemaphoreType.DMA(())],
        compiler_params=_params("arbitrary"),
        name="dispatch",
    )(*pos_tiles, fill_tiles, h, g)


def _expert_kernel(te_ref, na_ref, x_ref, wg_ref, wu_ref, wd_ref, o_ref, *acc, tm, n_f):
    del te_ref
    acc_ref = acc[0] if acc else None
    f = pl.program_id(1)
    active = pl.program_id(0) < na_ref[0]

    @pl.when(jnp.logical_and(jnp.logical_not(active), f == 0))
    def _():
        o_ref[...] = jnp.zeros_like(o_ref)

    def swiglu_chunk(first, last):
        x = _tiles_to_rows(x_ref, tm).astype(BF16)
        part = None if first else acc_ref[...]
        for c0 in range(0, FF_CHUNK, FF_SUBCHUNK):
            c1 = min(c0 + FF_SUBCHUNK, FF_CHUNK)
            gate = jnp.dot(x, wg_ref[:, c0:c1], preferred_element_type=F32)
            up = jnp.dot(x, wu_ref[:, c0:c1], preferred_element_type=F32)
            act = (jax.nn.silu(gate) * up).astype(BF16)
            down = jnp.dot(act, wd_ref[c0:c1, :], preferred_element_type=F32)
            part = down if part is None else part + down
        if last:
            _rows_to_tiles(part, o_ref)
        else:
            acc_ref[...] = part

    if n_f == 1:
        pl.when(active)(functools.partial(swiglu_chunk, True, True))
        return
    pl.when(jnp.logical_and(active, f == 0))(functools.partial(swiglu_chunk, True, False))
    pl.when(jnp.logical_and(active, f == n_f - 1))(functools.partial(swiglu_chunk, False, True))
    if n_f > 2:
        pl.when(jnp.logical_and(active, jnp.logical_and(f > 0, f < n_f - 1)))(
            functools.partial(swiglu_chunk, False, False))


def _experts(xs, wg, wu, wd, tile_expert, n_active, tm):
    d_exp = wg.shape[2]
    n_f = d_exp // FF_CHUNK
    n_tiles = xs.shape[0] // (tm * ROW_TILE)

    def x_tile(i, na):
        return jnp.minimum(i, na[0] - 1)

    def chunk(i, f, na):
        return jnp.where(i < na[0], f, n_f - 1)

    grid_spec = pltpu.PrefetchScalarGridSpec(
        num_scalar_prefetch=2,
        grid=(n_tiles, n_f),
        in_specs=[
            pl.BlockSpec((tm * ROW_TILE, LANES), lambda i, f, te, na: (x_tile(i, na), 0)),
            pl.BlockSpec((None, D_MODEL, FF_CHUNK), lambda i, f, te, na: (te[i], 0, chunk(i, f, na))),
            pl.BlockSpec((None, D_MODEL, FF_CHUNK), lambda i, f, te, na: (te[i], 0, chunk(i, f, na))),
            pl.BlockSpec((None, FF_CHUNK, D_MODEL), lambda i, f, te, na: (te[i], chunk(i, f, na), 0)),
        ],
        out_specs=pl.BlockSpec((tm * ROW_TILE, LANES), lambda i, f, te, na: (i, 0)),
        scratch_shapes=[pltpu.VMEM((tm, D_MODEL), F32)] if n_f > 1 else [],
    )
    return pl.pallas_call(
        functools.partial(_expert_kernel, tm=tm, n_f=n_f),
        grid_spec=grid_spec,
        out_shape=jax.ShapeDtypeStruct(xs.shape, F32),
        compiler_params=_params("arbitrary", "arbitrary"),
        name="experts",
    )(tile_expert, n_active, xs, wg, wu, wd)


def _combine_kernel(pos0_ref, pos1_ref, next_pos0_ref, next_pos1_ref, h_ref, meta_ref, g_ref, ys_ref,
                    o_ref, bufs, sems, *, ts):
    step = pl.program_id(0)
    slot = lax.rem(step, 2)

    def gather(tables, dst_slot):
        def issue(q, carry):
            for l in range(LANES):
                for k, table in enumerate(tables):
                    dst = _tile_rows(bufs.at[dst_slot, k], q * LANES + l)
                    pltpu.make_async_copy(_tile_rows(ys_ref, table[q, l]), dst,
                                          sems.at[dst_slot, k]).start(priority=k)
            return carry

        lax.fori_loop(0, ts // LANES, issue, 0)

    @pl.when(step == 0)
    def _():
        gather((pos0_ref, pos1_ref), slot)

    @pl.when(step + 1 < pl.num_programs(0))
    def _():
        gather((next_pos0_ref, next_pos1_ref), 1 - slot)

    for k in range(2):
        _wait_rows(bufs.at[slot, k], sems.at[slot, k])
    meta = meta_ref[...]
    w1 = meta[:, META_W:META_W + 1]
    w2 = meta[:, META_W + 1:META_W + 2]
    y = h_ref[...] + (w1 * _tiles_to_rows(bufs.at[slot, 0], ts)
                      + w2 * _tiles_to_rows(bufs.at[slot, 1], ts))
    o_ref[...] = _rmsnorm(y, g_ref[...])


def _combine(h, meta, g, ys, pos_tiles):
    n_tok = h.shape[0]
    n_steps = pos_tiles[0].shape[0]
    ts = n_tok // n_steps
    kern = functools.partial(_combine_kernel, ts=ts)
    pos_block = (None,) + pos_tiles[0].shape[1:]
    pos_spec = pl.BlockSpec(pos_block, lambda i: (i, 0, 0), memory_space=pltpu.SMEM)
    next_spec = pl.BlockSpec(pos_block, lambda i: (jnp.minimum(i + 1, n_steps - 1), 0, 0),
                             memory_space=pltpu.SMEM)
    return pl.pallas_call(
        kern,
        grid=(n_steps,),
        in_specs=[pos_spec, pos_spec, next_spec, next_spec,
                  pl.BlockSpec((ts, D_MODEL), lambda i: (i, 0)),
                  pl.BlockSpec((ts, LANES), lambda i: (i, 0)),
                  _const_spec(g.shape),
                  pl.BlockSpec(memory_space=pl.ANY)],
        out_specs=pl.BlockSpec((ts, D_MODEL), lambda i: (i, 0)),
        out_shape=jax.ShapeDtypeStruct((n_tok, D_MODEL), F32),
        scratch_shapes=[pltpu.VMEM((2, 2, ts * ROW_TILE, LANES), F32),
                        pltpu.SemaphoreType.DMA((2, 2))],
        compiler_params=_params("arbitrary"),
        name="combine",
    )(*pos_tiles, *pos_tiles, h, meta, g, ys)


def _pos_tiles(pos, ts):
    return [p.reshape(p.shape[0] // ts, ts // LANES, LANES) for p in pos]


def _moe(h, norm_g, wr_pad, tri, wg, wu, wd, final_g):
    n_tok = h.shape[0]
    tm = TILE_EXPERT
    meta, fields, cnt = _router(h, norm_g, wr_pad, tri)

    counts = cnt[:N_EXPERTS, 0].astype(jnp.int32)
    tiles = (counts + (tm - 1)) // tm
    tile_end = jnp.cumsum(tiles)
    tile_start = tile_end - tiles
    n_tiles = (2 * n_tok) // tm + N_EXPERTS
    n_rows = n_tiles * tm

    def slots(k):
        idx = fields[META_IDX + k].astype(jnp.int32)
        rank = fields[META_RANK + k].astype(jnp.int32)
        first_tile = jnp.sum(jnp.where(idx[:, None] == jnp.arange(N_EXPERTS)[None, :],
                                       tile_start[None, :], 0), axis=1)
        return first_tile * tm + rank

    pos = [slots(0), slots(1)]
    n_active = tile_end[-1:].astype(jnp.int32)
    tail_tiles = jnp.where(tiles > 0, tile_end - 1, -1)
    slack_tiles = n_active[0] + jnp.arange(N_EXPERTS)
    slack_tiles = jnp.where(slack_tiles < n_tiles, slack_tiles, -1)
    fill_tiles = jnp.concatenate([tail_tiles, slack_tiles]).astype(jnp.int32)
    t = jnp.minimum(jnp.arange(n_tiles, dtype=jnp.int32), n_active[0] - 1)
    tile_expert = jnp.sum(t[:, None] >= tile_end[None, :], axis=1).astype(jnp.int32)

    xs = _dispatch(h, norm_g, _pos_tiles(pos, TILE_DISPATCH), fill_tiles, n_rows, tm)
    ys = _experts(xs, wg, wu, wd, tile_expert, n_active, tm)
    return _combine(h, meta, final_g, ys, _pos_tiles(pos, TILE_COMBINE))


def _pad_last(a, width):
    return jnp.pad(a, [(0, 0)] * (a.ndim - 1) + [(0, width - a.shape[-1])])


def _row(a):
    return a.reshape(1, -1)


def _router_weight(w):
    w_hi = w.astype(BF16)
    w_lo = (w - w_hi.astype(F32)).astype(BF16)
    return jnp.concatenate([_pad_last(w_hi, LANES), _pad_last(w_lo, LANES)], axis=1)


def _mixer_params(norm_g, w_in, conv_a_w, conv_a_b, ln_a_g, ln_a_b, conv_b_w, ln_c_g, ln_c_b,
                  gmlp_ws, gmlp_b, mix_out_g, w_mix_out):
    widths = [A_WIDTH, A_WIDTH, B_WIDTH, B_WIDTH, B_WIDTH, C_WIDTH, C_WIDTH]
    bounds = [0]
    for w in widths:
        bounds.append(bounds[-1] + w)
    w_in_p = jnp.concatenate(
        [_pad_last(w_in[:, bounds[s]:bounds[s + 1]], SEG) for s in range(N_IN_SEG)], axis=1)
    out_bounds = [0, A_WIDTH, A_WIDTH + B_WIDTH, A_WIDTH + B_WIDTH + C_WIDTH]
    out_g = jnp.concatenate(
        [_pad_last(mix_out_g[out_bounds[s]:out_bounds[s + 1]], SEG) for s in range(3)])
    w_out_p = jnp.concatenate(
        [jnp.pad(w_mix_out[out_bounds[s]:out_bounds[s + 1]],
                 [(0, SEG - (out_bounds[s + 1] - out_bounds[s])), (0, 0)]) for s in range(3)], axis=0)
    gbias = _pad_last(jnp.repeat(gmlp_b.T, HEAD_GROUP, axis=1), SEG)
    group = jnp.arange(MXU_DIM) // HEAD_GROUP
    gsum = ((group[:, None] == group[None, :]) * (1.0 / HEAD_GROUP)).astype(BF16)
    return {
        "norm_g": _row(norm_g), "w_in": w_in_p.astype(BF16),
        "conv_a_w": jnp.repeat(conv_a_w, SUBLANES, axis=0),
        "conv_a_b": jnp.broadcast_to(conv_a_b, (SUBLANES, A_WIDTH)),
        "ln_a_g": _row(ln_a_g), "ln_a_b": _row(ln_a_b),
        "conv_b_w": _pad_last(conv_b_w, SEG),
        "ln_c_g": _row(_pad_last(ln_c_g, SEG)), "ln_c_b": _row(_pad_last(ln_c_b, SEG)),
        "gmlp_ws": gmlp_ws.astype(BF16), "gmlp_bias": gbias,
        "out_g": _row(out_g), "w_out": w_out_p.astype(BF16), "gsum": gsum,
    }


def kernel(x, mem, norm_mix_g, w_in, conv_a_w, conv_a_b, ln_a_g, ln_a_b, conv_b_w, ln_c_g, ln_c_b,
           gmlp_ws, gmlp_b, mix_out_g, w_mix_out, norm_x_g, norm_mem_g, w_xq, w_xkv, w_xo,
           norm_ffn_g, ffn_w_gate, ffn_w_up, ffn_w_down, moe_router, moe_w_gate, moe_w_up,
           moe_w_down, norm_final_g):
    bsz, seq_len, _ = x.shape
    mem_len = mem.shape[1]
    depth = w_in.shape[0]
    assert depth == 2 and ffn_w_gate.shape[0] == 1 and moe_router.shape[0] == 1
    assert seq_len % TILE_MIX == 0 and seq_len % TILE_ATT == 0

    h = x.reshape(bsz * seq_len, D_MODEL)
    mem2d = mem.reshape(bsz * mem_len, D_MODEL)
    tri = (jnp.arange(TILE_ROUTE)[:, None] < jnp.arange(TILE_ROUTE)[None, :]).astype(BF16)
    n_exp, _, d_exp = moe_w_gate.shape[1:]
    sides = [moe_w_gate[0].reshape(n_exp * D_MODEL, d_exp), moe_w_up[0].reshape(n_exp * D_MODEL, d_exp),
             moe_w_down[0].reshape(n_exp * d_exp, D_MODEL)]
    casted = []

    def attend(h, layer):
        k, v = _kv_proj(mem2d, _row(norm_mem_g[layer]), w_xkv[layer].astype(BF16))
        h, done = _xattn(h, k, v, _row(norm_x_g[layer]), w_xq[layer].astype(BF16),
                         w_xo[layer].astype(BF16), seq_len, mem_len, sides[len(casted)])
        casted.append(done)
        return h

    def mix(h, layer):
        mp = _mixer_params(norm_mix_g[layer], w_in[layer], conv_a_w[layer], conv_a_b[layer],
                           ln_a_g[layer], ln_a_b[layer], conv_b_w[layer], ln_c_g[layer],
                           ln_c_b[layer], gmlp_ws[layer], gmlp_b[layer], mix_out_g[layer],
                           w_mix_out[layer])
        return _mixer(h, mp, seq_len)

    h = attend(mix(h, 0), 0)
    h, done = _ffn(h, _row(norm_ffn_g[0]), ffn_w_gate[0].astype(BF16), ffn_w_up[0].astype(BF16),
                   ffn_w_down[0].astype(BF16), sides[len(casted)])
    casted.append(done)
    h = attend(mix(h, 1), 1)
    out = _moe(h, _row(norm_ffn_g[1]), _router_weight(moe_router[0]), tri,
               casted[0].reshape(n_exp, D_MODEL, d_exp), casted[1].reshape(n_exp, D_MODEL, d_exp),
               casted[2].reshape(n_exp, d_exp, D_MODEL), _row(norm_final_g))
    return out.reshape(bsz, seq_len, D_MODEL)
```

```python
import functools

import jax
import jax.numpy as jnp
from jax import lax
from jax.experimental import pallas as pl
from jax.experimental.pallas import tpu as pltpu

F32 = jnp.float32
BF16 = jnp.bfloat16

D_MODEL = 1024
EPS = 1e-6
CHUNK = 64
HEAD_GROUP = 64
A_WIDTH, B_WIDTH, C_WIDTH = 384, 320, 320
A_KERNEL, B_KERNEL = 31, 3
GMLP_BLOCK = 128
C_GROUPS = 5
X_HEADS = 4
X_HEAD_DIM = D_MODEL // X_HEADS
N_EXPERTS = 8

LANES = 128
SUBLANES = 8
MXU_DIM = 256
VMEM_LIMIT_BYTES = 56 * 1024 * 1024

SEG = 384
N_IN_SEG = 7
MIX_PAD = 3 * SEG
HALO_A = 32
HALO_B = SUBLANES
CONV_ROWS = 32
EDGE_PARTS = 2
SHIFT_EXTRA = HALO_A - SUBLANES

TILE_MIX = 512
TILE_ATT = 1024
TILE_FFN = 1024
TILE_ROUTE = 1024
TILE_DISPATCH = 1024
TILE_COMBINE = 256
TILE_EXPERT = 512
FF_CHUNK = 3584
FF_SUBCHUNK = 256
KV_ROWS = 1024
ROW_TILE = 8


def _rmsnorm(x, g):
    ms = jnp.mean(x * x, axis=-1, keepdims=True)
    return x * lax.rsqrt(ms + EPS) * g


def _const_spec(shape):
    zeros = (0,) * len(shape)
    return pl.BlockSpec(shape, lambda *_: zeros, pipeline_mode=pl.Buffered(1))


def _params(*semantics):
    return pltpu.CompilerParams(dimension_semantics=semantics,
                                vmem_limit_bytes=VMEM_LIMIT_BYTES)


def _mixer_kernel(h_ref, ng_ref, win_ref, caw_ref, cab_ref, lag_ref, lab_ref, cbw_ref,
                  lcg_ref, lcb_ref, ws_ref, gbias_ref, og_ref, wout_ref, gsum_ref,
                  o_ref, zbuf, abuf, bbuf, shifted, ybuf, *, ts, tiles_per_seq):
    seq_tile = lax.rem(pl.program_id(0), tiles_per_seq)

    @pl.when(seq_tile == 0)
    def _():
        abuf[0:HALO_A, :] = jnp.zeros((HALO_A, SEG), F32)
        bbuf[0:HALO_B, :] = jnp.zeros((HALO_B, SEG), F32)

    @pl.when(seq_tile > 0)
    def _():
        abuf[0:HALO_A, :] = abuf[ts:ts + HALO_A, :]
        bbuf[0:HALO_B, :] = bbuf[ts:ts + HALO_B, :]

    h = h_ref[...]
    xn = _rmsnorm(h, ng_ref[...]).astype(BF16)

    part = ts // EDGE_PARTS
    for r0 in range(0, ts, part):
        z_a = jnp.dot(xn[r0:r0 + part, :], win_ref[:, 0:2 * SEG], preferred_element_type=F32)
        abuf[HALO_A + r0:HALO_A + r0 + part, :] = z_a[:, 0:SEG] * jax.nn.sigmoid(z_a[:, SEG:2 * SEG])
        lo = 0 if r0 == 0 else r0 + SHIFT_EXTRA
        hi = r0 + part + SHIFT_EXTRA
        for phase in range(1, SUBLANES):
            shifted[phase - 1, lo:hi, :] = abuf[lo + phase:hi + phase, :]

    zbuf[...] = jnp.dot(xn, win_ref[:, 2 * SEG:N_IN_SEG * SEG], preferred_element_type=F32)

    for r in range(0, ts, CONV_ROWS):
        acc = jnp.concatenate([cab_ref[...]] * (CONV_ROWS // SUBLANES), axis=0)
        for k in range(A_KERNEL):
            groups, phase = divmod(HALO_A - A_KERNEL + 1 + k, SUBLANES)
            start = r + groups * SUBLANES
            if phase == 0:
                window = abuf[start:start + CONV_ROWS, :]
            else:
                window = shifted[phase - 1, start:start + CONV_ROWS, :]
            tap = caw_ref[k * SUBLANES:(k + 1) * SUBLANES, :]
            acc = acc + jnp.concatenate([tap] * (CONV_ROWS // SUBLANES), axis=0) * window
        mu = jnp.mean(acc, axis=-1, keepdims=True)
        xc = acc - mu
        var = jnp.mean(xc * xc, axis=-1, keepdims=True)
        ybuf[r:r + CONV_ROWS, 0:SEG] = jax.nn.silu(
            xc * lax.rsqrt(var + EPS) * lag_ref[...] + lab_ref[...])

    bbuf[HALO_B:HALO_B + ts, :] = zbuf[:, 1 * SEG:2 * SEG] * zbuf[:, 2 * SEG:3 * SEG]
    accb = jnp.zeros((ts, SEG), F32)
    for k in range(B_KERNEL):
        first = HALO_B - B_KERNEL + 1 + k
        accb = accb + cbw_ref[k:k + 1, :] * bbuf[first:first + ts, :]
    ybuf[:, 1 * SEG:2 * SEG] = zbuf[:, 0:SEG] * accb

    c_valid = lax.broadcasted_iota(jnp.int32, (1, SEG), 1) < C_WIDTH
    low_group = lax.broadcasted_iota(jnp.int32, (GMLP_BLOCK, LANES), 1) < HEAD_GROUP
    row_chunk = lax.broadcasted_iota(jnp.int32, (GMLP_BLOCK, GMLP_BLOCK), 0) // CHUNK
    col_chunk = lax.broadcasted_iota(jnp.int32, (GMLP_BLOCK, GMLP_BLOCK), 1) // CHUNK
    chunk_causal = row_chunk >= col_chunk
    ws = [jnp.where(chunk_causal, ws_ref[g], jnp.zeros((), BF16)) for g in range(C_GROUPS)]
    for r in range(0, ts, GMLP_BLOCK):
        rows = slice(r, r + GMLP_BLOCK)
        c_u = jax.nn.gelu(zbuf[rows, 3 * SEG:4 * SEG])
        c_v = jax.nn.gelu(zbuf[rows, 4 * SEG:5 * SEG])
        mu = jnp.sum(c_v, axis=-1, keepdims=True) * (1.0 / C_WIDTH)
        xc = jnp.where(c_valid, c_v - mu, 0.0)
        var = jnp.sum(xc * xc, axis=-1, keepdims=True) * (1.0 / C_WIDTH)
        v = (xc * lax.rsqrt(var + EPS) * lcg_ref[...] + lcb_ref[...]).astype(BF16)
        cols = []
        for j in range(SEG // LANES):
            vj = v[:, j * LANES:(j + 1) * LANES]
            col = jnp.dot(ws[2 * j], vj, preferred_element_type=F32)
            if 2 * j + 1 < C_GROUPS:
                col = jnp.where(low_group, col, jnp.dot(ws[2 * j + 1], vj, preferred_element_type=F32))
            cols.append(col)
        mixed = jnp.concatenate(cols, axis=1) + gbias_ref[...]
        ybuf[rows, 2 * SEG:3 * SEG] = c_u * mixed

    gsum = gsum_ref[...]
    for r0 in range(0, ts, part):
        y = ybuf[r0:r0 + part, :]
        y2 = (y * y).astype(BF16)
        sums = []
        for c0 in range(0, MIX_PAD, MXU_DIM):
            width = min(MXU_DIM, MIX_PAD - c0)
            sums.append(jnp.dot(y2[:, c0:c0 + width], gsum[0:width, 0:width],
                                preferred_element_type=F32))
        ms = jnp.concatenate(sums, axis=1)
        yn = (y * lax.rsqrt(ms + EPS) * og_ref[...]).astype(BF16)
        o_ref[r0:r0 + part, :] = h[r0:r0 + part, :] + jnp.dot(yn, wout_ref[...], preferred_element_type=F32)


def _mixer(h, p, seq_len):
    n_tok = h.shape[0]
    ts = TILE_MIX
    tiles_per_seq = seq_len // ts
    kern = functools.partial(_mixer_kernel, ts=ts, tiles_per_seq=tiles_per_seq)
    consts = [p["norm_g"], p["w_in"], p["conv_a_w"], p["conv_a_b"], p["ln_a_g"], p["ln_a_b"],
              p["conv_b_w"], p["ln_c_g"], p["ln_c_b"], p["gmlp_ws"], p["gmlp_bias"],
              p["out_g"], p["w_out"], p["gsum"]]
    return pl.pallas_call(
        kern,
        grid=(n_tok // ts,),
        in_specs=[pl.BlockSpec((ts, D_MODEL), lambda i: (i, 0))]
                 + [_const_spec(c.shape) for c in consts],
        out_specs=pl.BlockSpec((ts, D_MODEL), lambda i: (i, 0)),
        out_shape=jax.ShapeDtypeStruct((n_tok, D_MODEL), F32),
        scratch_shapes=[
            pltpu.VMEM((ts, (N_IN_SEG - 2) * SEG), F32),
            pltpu.VMEM((HALO_A + ts, SEG), F32),
            pltpu.VMEM((HALO_B + ts, SEG), F32),
            pltpu.VMEM((SUBLANES - 1, ts + SHIFT_EXTRA, SEG), F32),
            pltpu.VMEM((ts, MIX_PAD), F32),
        ],
        compiler_params=_params("arbitrary"),
        name="mixer",
    )(h, *consts)


def _kv_kernel(mem_ref, g_ref, wkv_ref, k_ref, v_ref):
    mn = _rmsnorm(mem_ref[...], g_ref[...]).astype(BF16)
    kv = jnp.dot(mn, wkv_ref[...], preferred_element_type=F32)
    k_ref[...] = kv[:, 0:D_MODEL].astype(BF16)
    v_ref[...] = kv[:, D_MODEL:2 * D_MODEL].astype(BF16)


def _kv_proj(mem2d, g, wkv):
    n = mem2d.shape[0]
    rows = min(KV_ROWS, n)
    return pl.pallas_call(
        _kv_kernel,
        grid=(n // rows,),
        in_specs=[pl.BlockSpec((rows, D_MODEL), lambda i: (i, 0)),
                  _const_spec(g.shape), _const_spec(wkv.shape)],
        out_specs=[pl.BlockSpec((rows, D_MODEL), lambda i: (i, 0))] * 2,
        out_shape=[jax.ShapeDtypeStruct((n, D_MODEL), BF16)] * 2,
        compiler_params=_params("arbitrary"),
        name="kv_proj",
    )(mem2d, g, wkv)


def _side_cast_specs(side, n_steps):
    rows = side.shape[0] // n_steps
    assert side.shape[0] % n_steps == 0 and rows % (2 * SUBLANES) == 0
    spec = pl.BlockSpec((rows, side.shape[1]), lambda i: (i, 0))
    return spec, spec, jax.ShapeDtypeStruct(side.shape, BF16)


def _xattn_kernel(h_ref, g_ref, wq_ref, k_ref, v_ref, wo_ref, side_ref, o_ref, side_out_ref):
    side_out_ref[...] = side_ref[...].astype(BF16)
    h = h_ref[...]
    xn = _rmsnorm(h, g_ref[...]).astype(BF16)
    q = (jnp.dot(xn, wq_ref[...], preferred_element_type=F32) * (X_HEAD_DIM ** -0.5)).astype(BF16)
    heads = []
    for hd in range(X_HEADS):
        cols = slice(hd * X_HEAD_DIM, (hd + 1) * X_HEAD_DIM)
        s = lax.dot_general(q[:, cols], k_ref[:, cols], (((1,), (1,)), ((), ())),
                            preferred_element_type=F32)
        e = jnp.exp(s - jnp.max(s, axis=-1, keepdims=True))
        pv = jnp.dot(e.astype(BF16), v_ref[:, cols], preferred_element_type=F32)
        heads.append(pv / jnp.sum(e, axis=-1, keepdims=True))
    o = jnp.concatenate(heads, axis=1).astype(BF16)
    o_ref[...] = h + jnp.dot(o, wo_ref[...], preferred_element_type=F32)


def _xattn(h, k, v, g, wq, wo, seq_len, mem_len, side):
    n_tok = h.shape[0]
    ts = TILE_ATT
    tiles_per_seq = seq_len // ts
    side_in, side_out, side_shape = _side_cast_specs(side, n_tok // ts)
    return pl.pallas_call(
        _xattn_kernel,
        grid=(n_tok // ts,),
        in_specs=[pl.BlockSpec((ts, D_MODEL), lambda i: (i, 0)),
                  _const_spec(g.shape), _const_spec(wq.shape),
                  pl.BlockSpec((mem_len, D_MODEL), lambda i: (i // tiles_per_seq, 0)),
                  pl.BlockSpec((mem_len, D_MODEL), lambda i: (i // tiles_per_seq, 0)),
                  _const_spec(wo.shape), side_in],
        out_specs=[pl.BlockSpec((ts, D_MODEL), lambda i: (i, 0)), side_out],
        out_shape=[jax.ShapeDtypeStruct((n_tok, D_MODEL), F32), side_shape],
        compiler_params=_params("arbitrary"),
        name="xattn",
    )(h, g, wq, k, v, wo, side)


def _ffn_chunks(d_ff):
    return [(c0, min(c0 + FF_SUBCHUNK, d_ff)) for c0 in range(0, d_ff, FF_SUBCHUNK)]


def _ffn_kernel(h_ref, g_ref, wg_ref, wu_ref, wd_ref, side_ref, o_ref, side_out_ref, *, chunks):
    side_out_ref[...] = side_ref[...].astype(BF16)
    h = h_ref[...]
    xn = _rmsnorm(h, g_ref[...]).astype(BF16)
    acc = h
    for c0, c1 in chunks:
        gate = jnp.dot(xn, wg_ref[:, c0:c1], preferred_element_type=F32)
        up = jnp.dot(xn, wu_ref[:, c0:c1], preferred_element_type=F32)
        act = (jax.nn.silu(gate) * up).astype(BF16)
        acc = acc + jnp.dot(act, wd_ref[c0:c1, :], preferred_element_type=F32)
    o_ref[...] = acc


def _ffn(h, g, wg, wu, wd, side):
    n_tok = h.shape[0]
    ts = TILE_FFN
    kern = functools.partial(_ffn_kernel, chunks=_ffn_chunks(wg.shape[1]))
    side_in, side_out, side_shape = _side_cast_specs(side, n_tok // ts)
    return pl.pallas_call(
        kern,
        grid=(n_tok // ts,),
        in_specs=[pl.BlockSpec((ts, D_MODEL), lambda i: (i, 0)),
                  _const_spec(g.shape), _const_spec(wg.shape),
                  _const_spec(wu.shape), _const_spec(wd.shape), side_in],
        out_specs=[pl.BlockSpec((ts, D_MODEL), lambda i: (i, 0)), side_out],
        out_shape=[jax.ShapeDtypeStruct((n_tok, D_MODEL), F32), side_shape],
        compiler_params=_params("arbitrary"),
        name="ffn",
    )(h, g, wg, wu, wd, side)


META_IDX, META_RANK, META_W = 0, 2, 4


ROUTE_ROWS = 2 * SUBLANES


def _router_kernel(h_ref, g_ref, wr_ref, tri_ref, meta_ref, fields_ref, cnt_ref, run_ref):
    @pl.when(pl.program_id(0) == 0)
    def _():
        run_ref[...] = jnp.zeros_like(run_ref)

    hn = _rmsnorm(h_ref[...], g_ref[...])
    hn_hi = hn.astype(BF16)
    hn_lo = (hn - hn_hi.astype(F32)).astype(BF16)
    both = jnp.dot(hn_hi, wr_ref[...], preferred_element_type=F32)
    logits = (both[:, 0:LANES] + both[:, LANES:2 * LANES]
              + jnp.dot(hn_lo, wr_ref[:, 0:LANES], preferred_element_type=F32))
    lt = logits.T[0:ROUTE_ROWS, :]
    row = lax.broadcasted_iota(jnp.int32, lt.shape, 0).astype(F32)
    neg_inf = jnp.float32(-jnp.inf)
    l1 = jnp.where(row < N_EXPERTS, lt, neg_inf)
    m1 = jnp.max(l1, axis=0, keepdims=True)
    i1 = jnp.min(jnp.where(l1 == m1, row, float(ROUTE_ROWS)), axis=0, keepdims=True)
    l2 = jnp.where(row == i1, neg_inf, l1)
    m2 = jnp.max(l2, axis=0, keepdims=True)
    i2 = jnp.min(jnp.where(l2 == m2, row, float(ROUTE_ROWS)), axis=0, keepdims=True)
    e2 = jnp.exp(m2 - m1)
    w1 = 1.0 / (1.0 + e2)
    w2 = e2 / (1.0 + e2)
    sel1 = row == i1
    sel2 = row == i2
    onehot = jnp.where(sel1 | sel2, 1.0, 0.0)
    before = jnp.dot(onehot.astype(BF16), tri_ref[...], preferred_element_type=F32)
    run = run_ref[:, 0:1]
    rank = before + run
    r1 = jnp.sum(jnp.where(sel1, rank, 0.0), axis=0, keepdims=True)
    r2 = jnp.sum(jnp.where(sel2, rank, 0.0), axis=0, keepdims=True)
    total = run + jnp.sum(onehot, axis=1, keepdims=True)
    run_ref[...] = jnp.broadcast_to(total, run_ref.shape)
    cnt_ref[...] = jnp.broadcast_to(total, cnt_ref.shape)
    zero = jnp.zeros_like(w1)
    fields = jnp.concatenate([i1, i2, r1, r2, w1, w2, zero, zero], axis=0)
    fields_ref[...] = fields
    padded = jnp.concatenate([fields, jnp.zeros((LANES - SUBLANES, fields.shape[1]), F32)], axis=0)
    meta_ref[...] = padded.T


def _router(h, g, wr_pad, tri):
    n_tok = h.shape[0]
    ts = TILE_ROUTE
    return pl.pallas_call(
        _router_kernel,
        grid=(n_tok // ts,),
        in_specs=[pl.BlockSpec((ts, D_MODEL), lambda i: (i, 0)),
                  _const_spec(g.shape), _const_spec(wr_pad.shape), _const_spec(tri.shape)],
        out_specs=[pl.BlockSpec((ts, LANES), lambda i: (i, 0)),
                   pl.BlockSpec((SUBLANES, ts), lambda i: (0, i)),
                   pl.BlockSpec((ROUTE_ROWS, LANES), lambda i: (0, 0))],
        out_shape=[jax.ShapeDtypeStruct((n_tok, LANES), F32),
                   jax.ShapeDtypeStruct((SUBLANES, n_tok), F32),
                   jax.ShapeDtypeStruct((ROUTE_ROWS, LANES), F32)],
        scratch_shapes=[pltpu.VMEM((ROUTE_ROWS, LANES), F32)],
        compiler_params=_params("arbitrary"),
        name="router",
    )(h, g, wr_pad, tri)


def _rows_to_tiles(x, dst):
    n = x.shape[0]
    for g in range(n // SUBLANES):
        for j in range(ROW_TILE):
            dst[pl.ds(g * SUBLANES * ROW_TILE + j, SUBLANES, stride=ROW_TILE), :] = (
                x[g * SUBLANES:(g + 1) * SUBLANES, j * LANES:(j + 1) * LANES])


def _tiles_to_rows(src, n):
    groups = []
    for g in range(n // SUBLANES):
        groups.append(jnp.concatenate(
            [src[pl.ds(g * SUBLANES * ROW_TILE + j, SUBLANES, stride=ROW_TILE), :]
             for j in range(ROW_TILE)], axis=1))
    return jnp.concatenate(groups, axis=0)


def _tile_rows(ref, row):
    return ref.at[pl.ds(pl.multiple_of(row * ROW_TILE, ROW_TILE), ROW_TILE), :]


def _wait_rows(buf, sem):
    pltpu.make_async_copy(buf, buf, sem).wait()


def _dispatch_kernel(pos0_ref, pos1_ref, fill_ref, h_ref, g_ref, xs_ref, hn_buf, zero_buf, sems,
                     fill_sem, *, ts, tm):
    step = pl.program_id(0)
    slot = lax.rem(step, 2)

    @pl.when(step == 0)
    def _():
        zero_buf[...] = jnp.zeros_like(zero_buf)
        for k in range(2 * N_EXPERTS):
            @pl.when(fill_ref[k] >= 0)
            def _():
                cp = pltpu.make_async_copy(
                    zero_buf, xs_ref.at[pl.ds(fill_ref[k] * (tm * ROW_TILE), tm * ROW_TILE), :],
                    fill_sem)
                cp.start()
                cp.wait()

    rows = hn_buf.at[slot]
    _rows_to_tiles(_rmsnorm(h_ref[...], g_ref[...]), rows)

    def issue(q, carry):
        for l in range(LANES):
            src = _tile_rows(rows, q * LANES + l)
            for k, table in enumerate((pos0_ref, pos1_ref)):
                pltpu.make_async_copy(src, _tile_rows(xs_ref, table[q, l]),
                                      sems.at[slot, k]).start(priority=k)
        return carry

    lax.fori_loop(0, ts // LANES, issue, 0)

    @pl.when(step > 0)
    def _():
        for k in range(2):
            _wait_rows(hn_buf.at[1 - slot], sems.at[1 - slot, k])

    @pl.when(step == pl.num_programs(0) - 1)
    def _():
        for k in range(2):
            _wait_rows(rows, sems.at[slot, k])


def _dispatch(h, g, pos_tiles, fill_tiles, n_rows, tm):
    n_tok = h.shape[0]
    n_steps = pos_tiles[0].shape[0]
    ts = n_tok // n_steps
    kern = functools.partial(_dispatch_kernel, ts=ts, tm=tm)
    pos_spec = pl.BlockSpec((None,) + pos_tiles[0].shape[1:], lambda i: (i, 0, 0),
                            memory_space=pltpu.SMEM)
    return pl.pallas_call(
        kern,
        grid=(n_steps,),
        in_specs=[pos_spec, pos_spec,
                  pl.BlockSpec(memory_space=pltpu.SMEM),
                  pl.BlockSpec((ts, D_MODEL), lambda i: (i, 0)),
                  _const_spec(g.shape)],
        out_specs=pl.BlockSpec(memory_space=pl.ANY),
        out_shape=jax.ShapeDtypeStruct((n_rows * ROW_TILE, LANES), F32),
        scratch_shapes=[pltpu.VMEM((2, ts * ROW_TILE, LANES), F32),
                        pltpu.VMEM((tm * ROW_TILE, LANES), F32),
                        pltpu.SemaphoreType.DMA((2, 2)),
                        pltpu.SemaphoreType.DMA(())],
        compiler_params=_params("arbitrary"),
        name="dispatch",
    )(*pos_tiles, fill_tiles, h, g)


def _expert_kernel(te_ref, na_ref, x_ref, wg_ref, wu_ref, wd_ref, o_ref, *acc, tm, n_f):
    del te_ref
    acc_ref = acc[0] if acc else None
    f = pl.program_id(1)
    active = pl.program_id(0) < na_ref[0]

    @pl.when(jnp.logical_and(jnp.logical_not(active), f == 0))
    def _():
        o_ref[...] = jnp.zeros_like(o_ref)

    def swiglu_chunk(first, last):
        x = _tiles_to_rows(x_ref, tm).astype(BF16)
        part = None if first else acc_ref[...]
        for c0 in range(0, FF_CHUNK, FF_SUBCHUNK):
            c1 = min(c0 + FF_SUBCHUNK, FF_CHUNK)
            gate = jnp.dot(x, wg_ref[:, c0:c1], preferred_element_type=F32)
            up = jnp.dot(x, wu_ref[:, c0:c1], preferred_element_type=F32)
            act = (jax.nn.silu(gate) * up).astype(BF16)
            down = jnp.dot(act, wd_ref[c0:c1, :], preferred_element_type=F32)
            part = down if part is None else part + down
        if last:
            _rows_to_tiles(part, o_ref)
        else:
            acc_ref[...] = part

    if n_f == 1:
        pl.when(active)(functools.partial(swiglu_chunk, True, True))
        return
    pl.when(jnp.logical_and(active, f == 0))(functools.partial(swiglu_chunk, True, False))
    pl.when(jnp.logical_and(active, f == n_f - 1))(functools.partial(swiglu_chunk, False, True))
    if n_f > 2:
        pl.when(jnp.logical_and(active, jnp.logical_and(f > 0, f < n_f - 1)))(
            functools.partial(swiglu_chunk, False, False))


def _experts(xs, wg, wu, wd, tile_expert, n_active, tm):
    d_exp = wg.shape[2]
    n_f = d_exp // FF_CHUNK
    n_tiles = xs.shape[0] // (tm * ROW_TILE)

    def x_tile(i, na):
        return jnp.minimum(i, na[0] - 1)

    def chunk(i, f, na):
        return jnp.where(i < na[0], f, n_f - 1)

    grid_spec = pltpu.PrefetchScalarGridSpec(
        num_scalar_prefetch=2,
        grid=(n_tiles, n_f),
        in_specs=[
            pl.BlockSpec((tm * ROW_TILE, LANES), lambda i, f, te, na: (x_tile(i, na), 0)),
            pl.BlockSpec((None, D_MODEL, FF_CHUNK), lambda i, f, te, na: (te[i], 0, chunk(i, f, na))),
            pl.BlockSpec((None, D_MODEL, FF_CHUNK), lambda i, f, te, na: (te[i], 0, chunk(i, f, na))),
            pl.BlockSpec((None, FF_CHUNK, D_MODEL), lambda i, f, te, na: (te[i], chunk(i, f, na), 0)),
        ],
        out_specs=pl.BlockSpec((tm * ROW_TILE, LANES), lambda i, f, te, na: (i, 0)),
        scratch_shapes=[pltpu.VMEM((tm, D_MODEL), F32)] if n_f > 1 else [],
    )
    return pl.pallas_call(
        functools.partial(_expert_kernel, tm=tm, n_f=n_f),
        grid_spec=grid_spec,
        out_shape=jax.ShapeDtypeStruct(xs.shape, F32),
        compiler_params=_params("arbitrary", "arbitrary"),
        name="experts",
    )(tile_expert, n_active, xs, wg, wu, wd)


def _combine_kernel(pos0_ref, pos1_ref, next_pos0_ref, next_pos1_ref, h_ref, meta_ref, g_ref, ys_ref,
                    o_ref, bufs, sems, *, ts):
    step = pl.program_id(0)
    slot = lax.rem(step, 2)

    def gather(tables, dst_slot):
        def issue(q, carry):
            for l in range(LANES):
                for k, table in enumerate(tables):
                    dst = _tile_rows(bufs.at[dst_slot, k], q * LANES + l)
                    pltpu.make_async_copy(_tile_rows(ys_ref, table[q, l]), dst,
                                          sems.at[dst_slot, k]).start(priority=k)
            return carry

        lax.fori_loop(0, ts // LANES, issue, 0)

    @pl.when(step == 0)
    def _():
        gather((pos0_ref, pos1_ref), slot)

    @pl.when(step + 1 < pl.num_programs(0))
    def _():
        gather((next_pos0_ref, next_pos1_ref), 1 - slot)

    for k in range(2):
        _wait_rows(bufs.at[slot, k], sems.at[slot, k])
    meta = meta_ref[...]
    w1 = meta[:, META_W:META_W + 1]
    w2 = meta[:, META_W + 1:META_W + 2]
    y = h_ref[...] + (w1 * _tiles_to_rows(bufs.at[slot, 0], ts)
                      + w2 * _tiles_to_rows(bufs.at[slot, 1], ts))
    o_ref[...] = _rmsnorm(y, g_ref[...])


def _combine(h, meta, g, ys, pos_tiles):
    n_tok = h.shape[0]
    n_steps = pos_tiles[0].shape[0]
    ts = n_tok // n_steps
    kern = functools.partial(_combine_kernel, ts=ts)
    pos_block = (None,) + pos_tiles[0].shape[1:]
    pos_spec = pl.BlockSpec(pos_block, lambda i: (i, 0, 0), memory_space=pltpu.SMEM)
    next_spec = pl.BlockSpec(pos_block, lambda i: (jnp.minimum(i + 1, n_steps - 1), 0, 0),
                             memory_space=pltpu.SMEM)
    return pl.pallas_call(
        kern,
        grid=(n_steps,),
        in_specs=[pos_spec, pos_spec, next_spec, next_spec,
                  pl.BlockSpec((ts, D_MODEL), lambda i: (i, 0)),
                  pl.BlockSpec((ts, LANES), lambda i: (i, 0)),
                  _const_spec(g.shape),
                  pl.BlockSpec(memory_space=pl.ANY)],
        out_specs=pl.BlockSpec((ts, D_MODEL), lambda i: (i, 0)),
        out_shape=jax.ShapeDtypeStruct((n_tok, D_MODEL), F32),
        scratch_shapes=[pltpu.VMEM((2, 2, ts * ROW_TILE, LANES), F32),
                        pltpu.SemaphoreType.DMA((2, 2))],
        compiler_params=_params("arbitrary"),
        name="combine",
    )(*pos_tiles, *pos_tiles, h, meta, g, ys)


def _pos_tiles(pos, ts):
    return [p.reshape(p.shape[0] // ts, ts // LANES, LANES) for p in pos]


def _moe(h, norm_g, wr_pad, tri, wg, wu, wd, final_g):
    n_tok = h.shape[0]
    tm = TILE_EXPERT
    meta, fields, cnt = _router(h, norm_g, wr_pad, tri)

    counts = cnt[:N_EXPERTS, 0].astype(jnp.int32)
    tiles = (counts + (tm - 1)) // tm
    tile_end = jnp.cumsum(tiles)
    tile_start = tile_end - tiles
    n_tiles = (2 * n_tok) // tm + N_EXPERTS
    n_rows = n_tiles * tm

    def slots(k):
        idx = fields[META_IDX + k].astype(jnp.int32)
        rank = fields[META_RANK + k].astype(jnp.int32)
        first_tile = jnp.sum(jnp.where(idx[:, None] == jnp.arange(N_EXPERTS)[None, :],
                                       tile_start[None, :], 0), axis=1)
        return first_tile * tm + rank

    pos = [slots(0), slots(1)]
    n_active = tile_end[-1:].astype(jnp.int32)
    tail_tiles = jnp.where(tiles > 0, tile_end - 1, -1)
    slack_tiles = n_active[0] + jnp.arange(N_EXPERTS)
    slack_tiles = jnp.where(slack_tiles < n_tiles, slack_tiles, -1)
    fill_tiles = jnp.concatenate([tail_tiles, slack_tiles]).astype(jnp.int32)
    t = jnp.minimum(jnp.arange(n_tiles, dtype=jnp.int32), n_active[0] - 1)
    tile_expert = jnp.sum(t[:, None] >= tile_end[None, :], axis=1).astype(jnp.int32)

    xs = _dispatch(h, norm_g, _pos_tiles(pos, TILE_DISPATCH), fill_tiles, n_rows, tm)
    ys = _experts(xs, wg, wu, wd, tile_expert, n_active, tm)
    return _combine(h, meta, final_g, ys, _pos_tiles(pos, TILE_COMBINE))


def _pad_last(a, width):
    return jnp.pad(a, [(0, 0)] * (a.ndim - 1) + [(0, width - a.shape[-1])])


def _row(a):
    return a.reshape(1, -1)


def _router_weight(w):
    w_hi = w.astype(BF16)
    w_lo = (w - w_hi.astype(F32)).astype(BF16)
    return jnp.concatenate([_pad_last(w_hi, LANES), _pad_last(w_lo, LANES)], axis=1)


def _pad_tail_cols(a, prefix, width):
    head, tail = a[..., :prefix], a[..., prefix:]
    n = tail.shape[-1] // width
    tail = _pad_last(tail.reshape(tail.shape[:-1] + (n, width)), SEG)
    return jnp.concatenate([head, tail.reshape(tail.shape[:-2] + (n * SEG,))], axis=-1)


def _pad_tail_rows(a, prefix, width):
    head, tail = a[:prefix], a[prefix:]
    n = tail.shape[0] // width
    tail = jnp.pad(tail.reshape(n, width, -1), [(0, 0), (0, SEG - width), (0, 0)])
    return jnp.concatenate([head, tail.reshape(n * SEG, -1)], axis=0)


def _mixer_params(norm_g, w_in, conv_a_w, conv_a_b, ln_a_g, ln_a_b, conv_b_w, ln_c_g, ln_c_b,
                  gmlp_ws, gmlp_b, mix_out_g, w_mix_out):
    assert B_WIDTH == C_WIDTH
    w_in_p = _pad_tail_cols(w_in.astype(BF16), 2 * A_WIDTH, B_WIDTH)
    out_g = _pad_tail_cols(_row(mix_out_g), A_WIDTH, B_WIDTH)
    w_out_p = _pad_tail_rows(w_mix_out.astype(BF16), A_WIDTH, B_WIDTH)
    gbias = _pad_last(jnp.repeat(gmlp_b.T, HEAD_GROUP, axis=1), SEG)
    group = jnp.arange(MXU_DIM) // HEAD_GROUP
    gsum = ((group[:, None] == group[None, :]) * (1.0 / HEAD_GROUP)).astype(BF16)
    return {
        "norm_g": _row(norm_g), "w_in": w_in_p,
        "conv_a_w": jnp.repeat(conv_a_w, SUBLANES, axis=0),
        "conv_a_b": jnp.broadcast_to(conv_a_b, (SUBLANES, A_WIDTH)),
        "ln_a_g": _row(ln_a_g), "ln_a_b": _row(ln_a_b),
        "conv_b_w": _pad_last(conv_b_w, SEG),
        "ln_c_g": _row(_pad_last(ln_c_g, SEG)), "ln_c_b": _row(_pad_last(ln_c_b, SEG)),
        "gmlp_ws": gmlp_ws.astype(BF16), "gmlp_bias": gbias,
        "out_g": out_g, "w_out": w_out_p, "gsum": gsum,
    }


def kernel(x, mem, norm_mix_g, w_in, conv_a_w, conv_a_b, ln_a_g, ln_a_b, conv_b_w, ln_c_g, ln_c_b,
           gmlp_ws, gmlp_b, mix_out_g, w_mix_out, norm_x_g, norm_mem_g, w_xq, w_xkv, w_xo,
           norm_ffn_g, ffn_w_gate, ffn_w_up, ffn_w_down, moe_router, moe_w_gate, moe_w_up,
           moe_w_down, norm_final_g):
    bsz, seq_len, _ = x.shape
    mem_len = mem.shape[1]
    depth = w_in.shape[0]
    assert depth == 2 and ffn_w_gate.shape[0] == 1 and moe_router.shape[0] == 1
    assert seq_len % TILE_MIX == 0 and seq_len % TILE_ATT == 0

    h = x.reshape(bsz * seq_len, D_MODEL)
    mem2d = mem.reshape(bsz * mem_len, D_MODEL)
    tri = (jnp.arange(TILE_ROUTE)[:, None] < jnp.arange(TILE_ROUTE)[None, :]).astype(BF16)
    n_exp, _, d_exp = moe_w_gate.shape[1:]
    sides = [moe_w_gate[0].reshape(n_exp * D_MODEL, d_exp), moe_w_up[0].reshape(n_exp * D_MODEL, d_exp),
             moe_w_down[0].reshape(n_exp * d_exp, D_MODEL)]
    casted = []

    def attend(h, layer):
        k, v = _kv_proj(mem2d, _row(norm_mem_g[layer]), w_xkv[layer].astype(BF16))
        h, done = _xattn(h, k, v, _row(norm_x_g[layer]), w_xq[layer].astype(BF16),
                         w_xo[layer].astype(BF16), seq_len, mem_len, sides[len(casted)])
        casted.append(done)
        return h

    def mix(h, layer):
        mp = _mixer_params(norm_mix_g[layer], w_in[layer], conv_a_w[layer], conv_a_b[layer],
                           ln_a_g[layer], ln_a_b[layer], conv_b_w[layer], ln_c_g[layer],
                           ln_c_b[layer], gmlp_ws[layer], gmlp_b[layer], mix_out_g[layer],
                           w_mix_out[layer])
        return _mixer(h, mp, seq_len)

    h = attend(mix(h, 0), 0)
    h, done = _ffn(h, _row(norm_ffn_g[0]), ffn_w_gate[0].astype(BF16), ffn_w_up[0].astype(BF16),
                   ffn_w_down[0].astype(BF16), sides[len(casted)])
    casted.append(done)
    h = attend(mix(h, 1), 1)
    out = _moe(h, _row(norm_ffn_g[1]), _router_weight(moe_router[0]), tri,
               casted[0].reshape(n_exp, D_MODEL, d_exp), casted[1].reshape(n_exp, D_MODEL, d_exp),
               casted[2].reshape(n_exp, d_exp, D_MODEL), _row(norm_final_g))
    return out.reshape(bsz, seq_len, D_MODEL)
```

```python
import functools

import jax
import jax.numpy as jnp
from jax import lax
from jax.experimental import pallas as pl
from jax.experimental.pallas import tpu as pltpu

F32 = jnp.float32
BF16 = jnp.bfloat16

D_MODEL = 1024
EPS = 1e-6
CHUNK = 64
HEAD_GROUP = 64
A_WIDTH, B_WIDTH, C_WIDTH = 384, 320, 320
A_KERNEL, B_KERNEL = 31, 3
GMLP_BLOCK = 128
C_GROUPS = 5
X_HEADS = 4
X_HEAD_DIM = D_MODEL // X_HEADS
N_EXPERTS = 8

LANES = 128
SUBLANES = 8
MXU_DIM = 256
VMEM_LIMIT_BYTES = 56 * 1024 * 1024

SEG = 384
N_IN_SEG = 7
MIX_PAD = 3 * SEG
HALO_A = 32
HALO_B = SUBLANES
CONV_ROWS = 32
EDGE_PARTS = 2
SHIFT_EXTRA = HALO_A - SUBLANES

TILE_MIX = 512
TILE_ATT = 1024
TILE_FFN = 1024
TILE_ROUTE = 1024
TILE_DISPATCH = 1024
TILE_COMBINE = 512
TILE_EXPERT = 512
FF_CHUNK = 3584
FF_SUBCHUNK = 256
KV_ROWS = 1024
ROW_TILE = 8


def _rmsnorm(x, g):
    ms = jnp.mean(x * x, axis=-1, keepdims=True)
    return x * lax.rsqrt(ms + EPS) * g


def _const_spec(shape):
    zeros = (0,) * len(shape)
    return pl.BlockSpec(shape, lambda *_: zeros, pipeline_mode=pl.Buffered(1))


def _params(*semantics):
    return pltpu.CompilerParams(dimension_semantics=semantics,
                                vmem_limit_bytes=VMEM_LIMIT_BYTES)


def _mixer_kernel(h_ref, ng_ref, win_ref, caw_ref, cab_ref, lag_ref, lab_ref, cbw_ref,
                  lcg_ref, lcb_ref, ws_ref, gbias_ref, og_ref, wout_ref, gsum_ref,
                  o_ref, zbuf, abuf, bbuf, shifted, ybuf, *, ts, tiles_per_seq):
    seq_tile = lax.rem(pl.program_id(0), tiles_per_seq)

    @pl.when(seq_tile == 0)
    def _():
        abuf[0:HALO_A, :] = jnp.zeros((HALO_A, SEG), F32)
        bbuf[0:HALO_B, :] = jnp.zeros((HALO_B, SEG), F32)

    @pl.when(seq_tile > 0)
    def _():
        abuf[0:HALO_A, :] = abuf[ts:ts + HALO_A, :]
        bbuf[0:HALO_B, :] = bbuf[ts:ts + HALO_B, :]

    h = h_ref[...]
    xn = _rmsnorm(h, ng_ref[...]).astype(BF16)

    part = ts // EDGE_PARTS
    for r0 in range(0, ts, part):
        z_a = jnp.dot(xn[r0:r0 + part, :], win_ref[:, 0:2 * SEG], preferred_element_type=F32)
        abuf[HALO_A + r0:HALO_A + r0 + part, :] = z_a[:, 0:SEG] * jax.nn.sigmoid(z_a[:, SEG:2 * SEG])
        lo = 0 if r0 == 0 else r0 + SHIFT_EXTRA
        hi = r0 + part + SHIFT_EXTRA
        for phase in range(1, SUBLANES):
            shifted[phase - 1, lo:hi, :] = abuf[lo + phase:hi + phase, :]

    zbuf[...] = jnp.dot(xn, win_ref[:, 2 * SEG:N_IN_SEG * SEG], preferred_element_type=F32)

    for r in range(0, ts, CONV_ROWS):
        acc = jnp.concatenate([cab_ref[...]] * (CONV_ROWS // SUBLANES), axis=0)
        for k in range(A_KERNEL):
            groups, phase = divmod(HALO_A - A_KERNEL + 1 + k, SUBLANES)
            start = r + groups * SUBLANES
            if phase == 0:
                window = abuf[start:start + CONV_ROWS, :]
            else:
                window = shifted[phase - 1, start:start + CONV_ROWS, :]
            tap = caw_ref[k * SUBLANES:(k + 1) * SUBLANES, :]
            acc = acc + jnp.concatenate([tap] * (CONV_ROWS // SUBLANES), axis=0) * window
        mu = jnp.mean(acc, axis=-1, keepdims=True)
        xc = acc - mu
        var = jnp.mean(xc * xc, axis=-1, keepdims=True)
        ybuf[r:r + CONV_ROWS, 0:SEG] = jax.nn.silu(
            xc * lax.rsqrt(var + EPS) * lag_ref[...] + lab_ref[...])

    bbuf[HALO_B:HALO_B + ts, :] = zbuf[:, 1 * SEG:2 * SEG] * zbuf[:, 2 * SEG:3 * SEG]
    accb = jnp.zeros((ts, SEG), F32)
    for k in range(B_KERNEL):
        first = HALO_B - B_KERNEL + 1 + k
        accb = accb + cbw_ref[k:k + 1, :] * bbuf[first:first + ts, :]
    ybuf[:, 1 * SEG:2 * SEG] = zbuf[:, 0:SEG] * accb

    c_valid = lax.broadcasted_iota(jnp.int32, (1, SEG), 1) < C_WIDTH
    low_group = lax.broadcasted_iota(jnp.int32, (GMLP_BLOCK, LANES), 1) < HEAD_GROUP
    row_chunk = lax.broadcasted_iota(jnp.int32, (GMLP_BLOCK, GMLP_BLOCK), 0) // CHUNK
    col_chunk = lax.broadcasted_iota(jnp.int32, (GMLP_BLOCK, GMLP_BLOCK), 1) // CHUNK
    chunk_causal = row_chunk >= col_chunk
    ws = [jnp.where(chunk_causal, ws_ref[g], jnp.zeros((), BF16)) for g in range(C_GROUPS)]
    for r in range(0, ts, GMLP_BLOCK):
        rows = slice(r, r + GMLP_BLOCK)
        c_u = jax.nn.gelu(zbuf[rows, 3 * SEG:4 * SEG])
        c_v = jax.nn.gelu(zbuf[rows, 4 * SEG:5 * SEG])
        mu = jnp.sum(c_v, axis=-1, keepdims=True) * (1.0 / C_WIDTH)
        xc = jnp.where(c_valid, c_v - mu, 0.0)
        var = jnp.sum(xc * xc, axis=-1, keepdims=True) * (1.0 / C_WIDTH)
        v = (xc * lax.rsqrt(var + EPS) * lcg_ref[...] + lcb_ref[...]).astype(BF16)
        cols = []
        for j in range(SEG // LANES):
            vj = v[:, j * LANES:(j + 1) * LANES]
            col = jnp.dot(ws[2 * j], vj, preferred_element_type=F32)
            if 2 * j + 1 < C_GROUPS:
                col = jnp.where(low_group, col, jnp.dot(ws[2 * j + 1], vj, preferred_element_type=F32))
            cols.append(col)
        mixed = jnp.concatenate(cols, axis=1) + gbias_ref[...]
        ybuf[rows, 2 * SEG:3 * SEG] = c_u * mixed

    gsum = gsum_ref[...]
    for r0 in range(0, ts, part):
        y = ybuf[r0:r0 + part, :]
        y2 = (y * y).astype(BF16)
        sums = []
        for c0 in range(0, MIX_PAD, MXU_DIM):
            width = min(MXU_DIM, MIX_PAD - c0)
            sums.append(jnp.dot(y2[:, c0:c0 + width], gsum[0:width, 0:width],
                                preferred_element_type=F32))
        ms = jnp.concatenate(sums, axis=1)
        yn = (y * lax.rsqrt(ms + EPS) * og_ref[...]).astype(BF16)
        o_ref[r0:r0 + part, :] = h[r0:r0 + part, :] + jnp.dot(yn, wout_ref[...], preferred_element_type=F32)


def _mixer(h, p, seq_len):
    n_tok = h.shape[0]
    ts = TILE_MIX
    tiles_per_seq = seq_len // ts
    kern = functools.partial(_mixer_kernel, ts=ts, tiles_per_seq=tiles_per_seq)
    consts = [p["norm_g"], p["w_in"], p["conv_a_w"], p["conv_a_b"], p["ln_a_g"], p["ln_a_b"],
              p["conv_b_w"], p["ln_c_g"], p["ln_c_b"], p["gmlp_ws"], p["gmlp_bias"],
              p["out_g"], p["w_out"], p["gsum"]]
    return pl.pallas_call(
        kern,
        grid=(n_tok // ts,),
        in_specs=[pl.BlockSpec((ts, D_MODEL), lambda i: (i, 0))]
                 + [_const_spec(c.shape) for c in consts],
        out_specs=pl.BlockSpec((ts, D_MODEL), lambda i: (i, 0)),
        out_shape=jax.ShapeDtypeStruct((n_tok, D_MODEL), F32),
        scratch_shapes=[
            pltpu.VMEM((ts, (N_IN_SEG - 2) * SEG), F32),
            pltpu.VMEM((HALO_A + ts, SEG), F32),
            pltpu.VMEM((HALO_B + ts, SEG), F32),
            pltpu.VMEM((SUBLANES - 1, ts + SHIFT_EXTRA, SEG), F32),
            pltpu.VMEM((ts, MIX_PAD), F32),
        ],
        compiler_params=_params("arbitrary"),
        name="mixer",
    )(h, *consts)


def _kv_kernel(mem_ref, g_ref, wkv_ref, k_ref, v_ref):
    mn = _rmsnorm(mem_ref[...], g_ref[...]).astype(BF16)
    kv = jnp.dot(mn, wkv_ref[...], preferred_element_type=F32)
    k_ref[...] = kv[:, 0:D_MODEL].astype(BF16)
    v_ref[...] = kv[:, D_MODEL:2 * D_MODEL].astype(BF16)


def _kv_proj(mem2d, g, wkv):
    n = mem2d.shape[0]
    rows = min(KV_ROWS, n)
    return pl.pallas_call(
        _kv_kernel,
        grid=(n // rows,),
        in_specs=[pl.BlockSpec((rows, D_MODEL), lambda i: (i, 0)),
                  _const_spec(g.shape), _const_spec(wkv.shape)],
        out_specs=[pl.BlockSpec((rows, D_MODEL), lambda i: (i, 0))] * 2,
        out_shape=[jax.ShapeDtypeStruct((n, D_MODEL), BF16)] * 2,
        compiler_params=_params("arbitrary"),
        name="kv_proj",
    )(mem2d, g, wkv)


def _side_cast_specs(side, n_steps):
    rows = side.shape[0] // n_steps
    assert side.shape[0] % n_steps == 0 and rows % (2 * SUBLANES) == 0
    spec = pl.BlockSpec((rows, side.shape[1]), lambda i: (i, 0))
    return spec, spec, jax.ShapeDtypeStruct(side.shape, BF16)


def _xattn_kernel(h_ref, g_ref, wq_ref, k_ref, v_ref, wo_ref, side_ref, o_ref, side_out_ref):
    side_out_ref[...] = side_ref[...].astype(BF16)
    h = h_ref[...]
    xn = _rmsnorm(h, g_ref[...]).astype(BF16)
    q = (jnp.dot(xn, wq_ref[...], preferred_element_type=F32) * (X_HEAD_DIM ** -0.5)).astype(BF16)
    heads = []
    for hd in range(X_HEADS):
        cols = slice(hd * X_HEAD_DIM, (hd + 1) * X_HEAD_DIM)
        s = lax.dot_general(q[:, cols], k_ref[:, cols], (((1,), (1,)), ((), ())),
                            preferred_element_type=F32)
        e = jnp.exp(s - jnp.max(s, axis=-1, keepdims=True))
        pv = jnp.dot(e.astype(BF16), v_ref[:, cols], preferred_element_type=F32)
        heads.append(pv * (1.0 / jnp.sum(e, axis=-1, keepdims=True)))
    o = jnp.concatenate(heads, axis=1).astype(BF16)
    o_ref[...] = h + jnp.dot(o, wo_ref[...], preferred_element_type=F32)


def _xattn(h, k, v, g, wq, wo, seq_len, mem_len, side):
    n_tok = h.shape[0]
    ts = TILE_ATT
    tiles_per_seq = seq_len // ts
    side_in, side_out, side_shape = _side_cast_specs(side, n_tok // ts)
    return pl.pallas_call(
        _xattn_kernel,
        grid=(n_tok // ts,),
        in_specs=[pl.BlockSpec((ts, D_MODEL), lambda i: (i, 0)),
                  _const_spec(g.shape), _const_spec(wq.shape),
                  pl.BlockSpec((mem_len, D_MODEL), lambda i: (i // tiles_per_seq, 0)),
                  pl.BlockSpec((mem_len, D_MODEL), lambda i: (i // tiles_per_seq, 0)),
                  _const_spec(wo.shape), side_in],
        out_specs=[pl.BlockSpec((ts, D_MODEL), lambda i: (i, 0)), side_out],
        out_shape=[jax.ShapeDtypeStruct((n_tok, D_MODEL), F32), side_shape],
        compiler_params=_params("arbitrary"),
        name="xattn",
    )(h, g, wq, k, v, wo, side)


def _ffn_chunks(d_ff):
    return [(c0, min(c0 + FF_SUBCHUNK, d_ff)) for c0 in range(0, d_ff, FF_SUBCHUNK)]


def _ffn_kernel(h_ref, g_ref, wg_ref, wu_ref, wd_ref, side_ref, o_ref, side_out_ref, *, chunks):
    side_out_ref[...] = side_ref[...].astype(BF16)
    h = h_ref[...]
    xn = _rmsnorm(h, g_ref[...]).astype(BF16)
    acc = h
    for c0, c1 in chunks:
        gate = jnp.dot(xn, wg_ref[:, c0:c1], preferred_element_type=F32)
        up = jnp.dot(xn, wu_ref[:, c0:c1], preferred_element_type=F32)
        act = (jax.nn.silu(gate) * up).astype(BF16)
        acc = acc + jnp.dot(act, wd_ref[c0:c1, :], preferred_element_type=F32)
    o_ref[...] = acc


def _ffn(h, g, wg, wu, wd, side):
    n_tok = h.shape[0]
    ts = TILE_FFN
    kern = functools.partial(_ffn_kernel, chunks=_ffn_chunks(wg.shape[1]))
    side_in, side_out, side_shape = _side_cast_specs(side, n_tok // ts)
    return pl.pallas_call(
        kern,
        grid=(n_tok // ts,),
        in_specs=[pl.BlockSpec((ts, D_MODEL), lambda i: (i, 0)),
                  _const_spec(g.shape), _const_spec(wg.shape),
                  _const_spec(wu.shape), _const_spec(wd.shape), side_in],
        out_specs=[pl.BlockSpec((ts, D_MODEL), lambda i: (i, 0)), side_out],
        out_shape=[jax.ShapeDtypeStruct((n_tok, D_MODEL), F32), side_shape],
        compiler_params=_params("arbitrary"),
        name="ffn",
    )(h, g, wg, wu, wd, side)


META_IDX, META_RANK, META_W = 0, 2, 4


ROUTE_ROWS = 2 * SUBLANES


def _router_kernel(h_ref, g_ref, wr_ref, tri_ref, meta_ref, fields_ref, cnt_ref, run_ref):
    @pl.when(pl.program_id(0) == 0)
    def _():
        run_ref[...] = jnp.zeros_like(run_ref)

    hn = _rmsnorm(h_ref[...], g_ref[...])
    hn_hi = hn.astype(BF16)
    hn_lo = (hn - hn_hi.astype(F32)).astype(BF16)
    both = jnp.dot(hn_hi, wr_ref[...], preferred_element_type=F32)
    logits = (both[:, 0:LANES] + both[:, LANES:2 * LANES]
              + jnp.dot(hn_lo, wr_ref[:, 0:LANES], preferred_element_type=F32))
    lt = logits.T[0:ROUTE_ROWS, :]
    row = lax.broadcasted_iota(jnp.int32, lt.shape, 0).astype(F32)
    neg_inf = jnp.float32(-jnp.inf)
    l1 = jnp.where(row < N_EXPERTS, lt, neg_inf)
    m1 = jnp.max(l1, axis=0, keepdims=True)
    i1 = jnp.min(jnp.where(l1 == m1, row, float(ROUTE_ROWS)), axis=0, keepdims=True)
    l2 = jnp.where(row == i1, neg_inf, l1)
    m2 = jnp.max(l2, axis=0, keepdims=True)
    i2 = jnp.min(jnp.where(l2 == m2, row, float(ROUTE_ROWS)), axis=0, keepdims=True)
    e2 = jnp.exp(m2 - m1)
    w1 = 1.0 / (1.0 + e2)
    w2 = e2 / (1.0 + e2)
    sel1 = row == i1
    sel2 = row == i2
    onehot = jnp.where(sel1 | sel2, 1.0, 0.0)
    before = jnp.dot(onehot.astype(BF16), tri_ref[...], preferred_element_type=F32)
    run = run_ref[:, 0:1]
    rank = before + run
    r1 = jnp.sum(jnp.where(sel1, rank, 0.0), axis=0, keepdims=True)
    r2 = jnp.sum(jnp.where(sel2, rank, 0.0), axis=0, keepdims=True)
    total = run + jnp.sum(onehot, axis=1, keepdims=True)
    run_ref[...] = jnp.broadcast_to(total, run_ref.shape)
    cnt_ref[...] = jnp.broadcast_to(total, cnt_ref.shape)
    zero = jnp.zeros_like(w1)
    fields = jnp.concatenate([i1, i2, r1, r2, w1, w2, zero, zero], axis=0)
    fields_ref[...] = fields
    padded = jnp.concatenate([fields, jnp.zeros((LANES - SUBLANES, fields.shape[1]), F32)], axis=0)
    meta_ref[...] = padded.T


def _router(h, g, wr_pad, tri):
    n_tok = h.shape[0]
    ts = TILE_ROUTE
    return pl.pallas_call(
        _router_kernel,
        grid=(n_tok // ts,),
        in_specs=[pl.BlockSpec((ts, D_MODEL), lambda i: (i, 0)),
                  _const_spec(g.shape), _const_spec(wr_pad.shape), _const_spec(tri.shape)],
        out_specs=[pl.BlockSpec((ts, LANES), lambda i: (i, 0)),
                   pl.BlockSpec((SUBLANES, ts), lambda i: (0, i)),
                   pl.BlockSpec((ROUTE_ROWS, LANES), lambda i: (0, 0))],
        out_shape=[jax.ShapeDtypeStruct((n_tok, LANES), F32),
                   jax.ShapeDtypeStruct((SUBLANES, n_tok), F32),
                   jax.ShapeDtypeStruct((ROUTE_ROWS, LANES), F32)],
        scratch_shapes=[pltpu.VMEM((ROUTE_ROWS, LANES), F32)],
        compiler_params=_params("arbitrary"),
        name="router",
    )(h, g, wr_pad, tri)


def _rows_to_tiles(x, dst):
    n = x.shape[0]
    for g in range(n // SUBLANES):
        for j in range(ROW_TILE):
            dst[pl.ds(g * SUBLANES * ROW_TILE + j, SUBLANES, stride=ROW_TILE), :] = (
                x[g * SUBLANES:(g + 1) * SUBLANES, j * LANES:(j + 1) * LANES])


def _tiles_to_rows(src, n):
    groups = []
    for g in range(n // SUBLANES):
        groups.append(jnp.concatenate(
            [src[pl.ds(g * SUBLANES * ROW_TILE + j, SUBLANES, stride=ROW_TILE), :]
             for j in range(ROW_TILE)], axis=1))
    return jnp.concatenate(groups, axis=0)


def _tile_rows(ref, row):
    return ref.at[pl.ds(pl.multiple_of(row * ROW_TILE, ROW_TILE), ROW_TILE), :]


def _wait_rows(buf, sem):
    pltpu.make_async_copy(buf, buf, sem).wait()


def _dispatch_kernel(pos0_ref, pos1_ref, fill_ref, h_ref, g_ref, xs_ref, hn_buf, zero_buf, sems,
                     fill_sem, *, ts, tm):
    step = pl.program_id(0)
    slot = lax.rem(step, 2)

    @pl.when(step == 0)
    def _():
        zero_buf[...] = jnp.zeros_like(zero_buf)
        for k in range(2 * N_EXPERTS):
            @pl.when(fill_ref[k] >= 0)
            def _():
                cp = pltpu.make_async_copy(
                    zero_buf, xs_ref.at[pl.ds(fill_ref[k] * (tm * ROW_TILE), tm * ROW_TILE), :],
                    fill_sem)
                cp.start()
                cp.wait()

    rows = hn_buf.at[slot]
    _rows_to_tiles(_rmsnorm(h_ref[...], g_ref[...]), rows)

    def issue(q, carry):
        for l in range(LANES):
            src = _tile_rows(rows, q * LANES + l)
            for k, table in enumerate((pos0_ref, pos1_ref)):
                pltpu.make_async_copy(src, _tile_rows(xs_ref, table[q, l]),
                                      sems.at[slot, k]).start(priority=k)
        return carry

    lax.fori_loop(0, ts // LANES, issue, 0)

    @pl.when(step > 0)
    def _():
        for k in range(2):
            _wait_rows(hn_buf.at[1 - slot], sems.at[1 - slot, k])

    @pl.when(step == pl.num_programs(0) - 1)
    def _():
        for k in range(2):
            _wait_rows(rows, sems.at[slot, k])


def _dispatch(h, g, pos_tiles, fill_tiles, n_rows, tm):
    n_tok = h.shape[0]
    n_steps = pos_tiles[0].shape[0]
    ts = n_tok // n_steps
    kern = functools.partial(_dispatch_kernel, ts=ts, tm=tm)
    pos_spec = pl.BlockSpec((None,) + pos_tiles[0].shape[1:], lambda i: (i, 0, 0),
                            memory_space=pltpu.SMEM)
    return pl.pallas_call(
        kern,
        grid=(n_steps,),
        in_specs=[pos_spec, pos_spec,
                  pl.BlockSpec(memory_space=pltpu.SMEM),
                  pl.BlockSpec((ts, D_MODEL), lambda i: (i, 0)),
                  _const_spec(g.shape)],
        out_specs=pl.BlockSpec(memory_space=pl.ANY),
        out_shape=jax.ShapeDtypeStruct((n_rows * ROW_TILE, LANES), F32),
        scratch_shapes=[pltpu.VMEM((2, ts * ROW_TILE, LANES), F32),
                        pltpu.VMEM((tm * ROW_TILE, LANES), F32),
                        pltpu.SemaphoreType.DMA((2, 2)),
                        pltpu.SemaphoreType.DMA(())],
        compiler_params=_params("arbitrary"),
        name="dispatch",
    )(*pos_tiles, fill_tiles, h, g)


def _expert_kernel(te_ref, na_ref, x_ref, wg_ref, wu_ref, wd_ref, o_ref, *acc, tm, n_f):
    del te_ref
    acc_ref = acc[0] if acc else None
    f = pl.program_id(1)
    active = pl.program_id(0) < na_ref[0]

    @pl.when(jnp.logical_and(jnp.logical_not(active), f == 0))
    def _():
        o_ref[...] = jnp.zeros_like(o_ref)

    def swiglu_chunk(first, last):
        x = _tiles_to_rows(x_ref, tm).astype(BF16)
        part = None if first else acc_ref[...]
        for c0 in range(0, FF_CHUNK, FF_SUBCHUNK):
            c1 = min(c0 + FF_SUBCHUNK, FF_CHUNK)
            gate = jnp.dot(x, wg_ref[:, c0:c1], preferred_element_type=F32)
            up = jnp.dot(x, wu_ref[:, c0:c1], preferred_element_type=F32)
            act = (jax.nn.silu(gate) * up).astype(BF16)
            down = jnp.dot(act, wd_ref[c0:c1, :], preferred_element_type=F32)
            part = down if part is None else part + down
        if last:
            _rows_to_tiles(part, o_ref)
        else:
            acc_ref[...] = part

    if n_f == 1:
        pl.when(active)(functools.partial(swiglu_chunk, True, True))
        return
    pl.when(jnp.logical_and(active, f == 0))(functools.partial(swiglu_chunk, True, False))
    pl.when(jnp.logical_and(active, f == n_f - 1))(functools.partial(swiglu_chunk, False, True))
    if n_f > 2:
        pl.when(jnp.logical_and(active, jnp.logical_and(f > 0, f < n_f - 1)))(
            functools.partial(swiglu_chunk, False, False))


def _experts(xs, wg, wu, wd, tile_expert, n_active, tm):
    d_exp = wg.shape[2]
    n_f = d_exp // FF_CHUNK
    n_tiles = xs.shape[0] // (tm * ROW_TILE)

    def x_tile(i, na):
        return jnp.minimum(i, na[0] - 1)

    def chunk(i, f, na):
        return jnp.where(i < na[0], f, n_f - 1)

    grid_spec = pltpu.PrefetchScalarGridSpec(
        num_scalar_prefetch=2,
        grid=(n_tiles, n_f),
        in_specs=[
            pl.BlockSpec((tm * ROW_TILE, LANES), lambda i, f, te, na: (x_tile(i, na), 0)),
            pl.BlockSpec((None, D_MODEL, FF_CHUNK), lambda i, f, te, na: (te[i], 0, chunk(i, f, na))),
            pl.BlockSpec((None, D_MODEL, FF_CHUNK), lambda i, f, te, na: (te[i], 0, chunk(i, f, na))),
            pl.BlockSpec((None, FF_CHUNK, D_MODEL), lambda i, f, te, na: (te[i], chunk(i, f, na), 0)),
        ],
        out_specs=pl.BlockSpec((tm * ROW_TILE, LANES), lambda i, f, te, na: (i, 0)),
        scratch_shapes=[pltpu.VMEM((tm, D_MODEL), F32)] if n_f > 1 else [],
    )
    return pl.pallas_call(
        functools.partial(_expert_kernel, tm=tm, n_f=n_f),
        grid_spec=grid_spec,
        out_shape=jax.ShapeDtypeStruct(xs.shape, F32),
        compiler_params=_params("arbitrary", "arbitrary"),
        name="experts",
    )(tile_expert, n_active, xs, wg, wu, wd)


def _combine_kernel(pos0_ref, pos1_ref, next_pos0_ref, next_pos1_ref, h_ref, meta_ref, g_ref, ys_ref,
                    o_ref, bufs, sems, *, ts):
    step = pl.program_id(0)
    slot = lax.rem(step, 2)

    def gather(tables, dst_slot):
        def issue(q, carry):
            for l in range(LANES):
                for k, table in enumerate(tables):
                    dst = _tile_rows(bufs.at[dst_slot, k], q * LANES + l)
                    pltpu.make_async_copy(_tile_rows(ys_ref, table[q, l]), dst,
                                          sems.at[dst_slot, k]).start(priority=k)
            return carry

        lax.fori_loop(0, ts // LANES, issue, 0)

    @pl.when(step == 0)
    def _():
        gather((pos0_ref, pos1_ref), slot)

    @pl.when(step + 1 < pl.num_programs(0))
    def _():
        gather((next_pos0_ref, next_pos1_ref), 1 - slot)

    for k in range(2):
        _wait_rows(bufs.at[slot, k], sems.at[slot, k])
    meta = meta_ref[...]
    w1 = meta[:, META_W:META_W + 1]
    w2 = meta[:, META_W + 1:META_W + 2]
    y = h_ref[...] + (w1 * _tiles_to_rows(bufs.at[slot, 0], ts)
                      + w2 * _tiles_to_rows(bufs.at[slot, 1], ts))
    o_ref[...] = _rmsnorm(y, g_ref[...])


def _combine(h, meta, g, ys, pos_tiles):
    n_tok = h.shape[0]
    n_steps = pos_tiles[0].shape[0]
    ts = n_tok // n_steps
    kern = functools.partial(_combine_kernel, ts=ts)
    pos_block = (None,) + pos_tiles[0].shape[1:]
    pos_spec = pl.BlockSpec(pos_block, lambda i: (i, 0, 0), memory_space=pltpu.SMEM)
    next_spec = pl.BlockSpec(pos_block, lambda i: (jnp.minimum(i + 1, n_steps - 1), 0, 0),
                             memory_space=pltpu.SMEM)
    return pl.pallas_call(
        kern,
        grid=(n_steps,),
        in_specs=[pos_spec, pos_spec, next_spec, next_spec,
                  pl.BlockSpec((ts, D_MODEL), lambda i: (i, 0)),
                  pl.BlockSpec((ts, LANES), lambda i: (i, 0)),
                  _const_spec(g.shape),
                  pl.BlockSpec(memory_space=pl.ANY)],
        out_specs=pl.BlockSpec((ts, D_MODEL), lambda i: (i, 0)),
        out_shape=jax.ShapeDtypeStruct((n_tok, D_MODEL), F32),
        scratch_shapes=[pltpu.VMEM((2, 2, ts * ROW_TILE, LANES), F32),
                        pltpu.SemaphoreType.DMA((2, 2))],
        compiler_params=_params("arbitrary"),
        name="combine",
    )(*pos_tiles, *pos_tiles, h, meta, g, ys)


def _pos_tiles(pos, ts):
    return [p.reshape(p.shape[0] // ts, ts // LANES, LANES) for p in pos]


def _moe(h, norm_g, wr_pad, tri, wg, wu, wd, final_g):
    n_tok = h.shape[0]
    tm = TILE_EXPERT
    meta, fields, cnt = _router(h, norm_g, wr_pad, tri)

    counts = cnt[:N_EXPERTS, 0].astype(jnp.int32)
    tiles = (counts + (tm - 1)) // tm
    tile_end = jnp.cumsum(tiles)
    tile_start = tile_end - tiles
    n_tiles = (2 * n_tok) // tm + N_EXPERTS
    n_rows = n_tiles * tm

    def slots(k):
        idx = fields[META_IDX + k].astype(jnp.int32)
        rank = fields[META_RANK + k].astype(jnp.int32)
        first_tile = jnp.sum(jnp.where(idx[:, None] == jnp.arange(N_EXPERTS)[None, :],
                                       tile_start[None, :], 0), axis=1)
        return first_tile * tm + rank

    pos = [slots(0), slots(1)]
    n_active = tile_end[-1:].astype(jnp.int32)
    tail_tiles = jnp.where(tiles > 0, tile_end - 1, -1)
    slack_tiles = n_active[0] + jnp.arange(N_EXPERTS)
    slack_tiles = jnp.where(slack_tiles < n_tiles, slack_tiles, -1)
    fill_tiles = jnp.concatenate([tail_tiles, slack_tiles]).astype(jnp.int32)
    t = jnp.minimum(jnp.arange(n_tiles, dtype=jnp.int32), n_active[0] - 1)
    tile_expert = jnp.sum(t[:, None] >= tile_end[None, :], axis=1).astype(jnp.int32)

    xs = _dispatch(h, norm_g, _pos_tiles(pos, TILE_DISPATCH), fill_tiles, n_rows, tm)
    ys = _experts(xs, wg, wu, wd, tile_expert, n_active, tm)
    return _combine(h, meta, final_g, ys, _pos_tiles(pos, TILE_COMBINE))


def _pad_last(a, width):
    return jnp.pad(a, [(0, 0)] * (a.ndim - 1) + [(0, width - a.shape[-1])])


def _row(a):
    return a.reshape(1, -1)


def _router_weight(w):
    w_hi = w.astype(BF16)
    w_lo = (w - w_hi.astype(F32)).astype(BF16)
    return jnp.concatenate([_pad_last(w_hi, LANES), _pad_last(w_lo, LANES)], axis=1)


def _mixer_params(norm_g, w_in, conv_a_w, conv_a_b, ln_a_g, ln_a_b, conv_b_w, ln_c_g, ln_c_b,
                  gmlp_ws, gmlp_b, mix_out_g, w_mix_out):
    widths = [A_WIDTH, A_WIDTH, B_WIDTH, B_WIDTH, B_WIDTH, C_WIDTH, C_WIDTH]
    bounds = [0]
    for w in widths:
        bounds.append(bounds[-1] + w)
    w_in_p = jnp.concatenate(
        [_pad_last(w_in[:, bounds[s]:bounds[s + 1]], SEG) for s in range(N_IN_SEG)], axis=1)
    out_bounds = [0, A_WIDTH, A_WIDTH + B_WIDTH, A_WIDTH + B_WIDTH + C_WIDTH]
    out_g = jnp.concatenate(
        [_pad_last(mix_out_g[out_bounds[s]:out_bounds[s + 1]], SEG) for s in range(3)])
    w_out_p = jnp.concatenate(
        [jnp.pad(w_mix_out[out_bounds[s]:out_bounds[s + 1]],
                 [(0, SEG - (out_bounds[s + 1] - out_bounds[s])), (0, 0)]) for s in range(3)], axis=0)
    gbias = _pad_last(jnp.repeat(gmlp_b.T, HEAD_GROUP, axis=1), SEG)
    group = jnp.arange(MXU_DIM) // HEAD_GROUP
    gsum = ((group[:, None] == group[None, :]) * (1.0 / HEAD_GROUP)).astype(BF16)
    return {
        "norm_g": _row(norm_g), "w_in": w_in_p.astype(BF16),
        "conv_a_w": jnp.repeat(conv_a_w, SUBLANES, axis=0),
        "conv_a_b": jnp.broadcast_to(conv_a_b, (SUBLANES, A_WIDTH)),
        "ln_a_g": _row(ln_a_g), "ln_a_b": _row(ln_a_b),
        "conv_b_w": _pad_last(conv_b_w, SEG),
        "ln_c_g": _row(_pad_last(ln_c_g, SEG)), "ln_c_b": _row(_pad_last(ln_c_b, SEG)),
        "gmlp_ws": gmlp_ws.astype(BF16), "gmlp_bias": gbias,
        "out_g": _row(out_g), "w_out": w_out_p.astype(BF16), "gsum": gsum,
    }


def kernel(x, mem, norm_mix_g, w_in, conv_a_w, conv_a_b, ln_a_g, ln_a_b, conv_b_w, ln_c_g, ln_c_b,
           gmlp_ws, gmlp_b, mix_out_g, w_mix_out, norm_x_g, norm_mem_g, w_xq, w_xkv, w_xo,
           norm_ffn_g, ffn_w_gate, ffn_w_up, ffn_w_down, moe_router, moe_w_gate, moe_w_up,
           moe_w_down, norm_final_g):
    bsz, seq_len, _ = x.shape
    mem_len = mem.shape[1]
    depth = w_in.shape[0]
    assert depth == 2 and ffn_w_gate.shape[0] == 1 and moe_router.shape[0] == 1
    assert seq_len % TILE_MIX == 0 and seq_len % TILE_ATT == 0

    h = x.reshape(bsz * seq_len, D_MODEL)
    mem2d = mem.reshape(bsz * mem_len, D_MODEL)
    tri = (jnp.arange(TILE_ROUTE)[:, None] < jnp.arange(TILE_ROUTE)[None, :]).astype(BF16)
    n_exp, _, d_exp = moe_w_gate.shape[1:]
    sides = [moe_w_gate[0].reshape(n_exp * D_MODEL, d_exp), moe_w_up[0].reshape(n_exp * D_MODEL, d_exp),
             moe_w_down[0].reshape(n_exp * d_exp, D_MODEL)]
    casted = []

    def attend(h, layer):
        k, v = _kv_proj(mem2d, _row(norm_mem_g[layer]), w_xkv[layer].astype(BF16))
        h, done = _xattn(h, k, v, _row(norm_x_g[layer]), w_xq[layer].astype(BF16),
                         w_xo[layer].astype(BF16), seq_len, mem_len, sides[len(casted)])
        casted.append(done)
        return h

    def mix(h, layer):
        mp = _mixer_params(norm_mix_g[layer], w_in[layer], conv_a_w[layer], conv_a_b[layer],
                           ln_a_g[layer], ln_a_b[layer], conv_b_w[layer], ln_c_g[layer],
                           ln_c_b[layer], gmlp_ws[layer], gmlp_b[layer], mix_out_g[layer],
                           w_mix_out[layer])
        return _mixer(h, mp, seq_len)

    h = attend(mix(h, 0), 0)
    h, done = _ffn(h, _row(norm_ffn_g[0]), ffn_w_gate[0].astype(BF16), ffn_w_up[0].astype(BF16),
                   ffn_w_down[0].astype(BF16), sides[len(casted)])
    casted.append(done)
    h = attend(mix(h, 1), 1)
    out = _moe(h, _row(norm_ffn_g[1]), _router_weight(moe_router[0]), tri,
               casted[0].reshape(n_exp, D_MODEL, d_exp), casted[1].reshape(n_exp, D_MODEL, d_exp),
               casted[2].reshape(n_exp, d_exp, D_MODEL), _row(norm_final_g))
    return out.reshape(bsz, seq_len, D_MODEL)
```

```python
import functools

import jax
import jax.numpy as jnp
from jax import lax
from jax.experimental import pallas as pl
from jax.experimental.pallas import tpu as pltpu

F32 = jnp.float32
BF16 = jnp.bfloat16

D_MODEL = 1024
EPS = 1e-6
CHUNK = 64
HEAD_GROUP = 64
A_WIDTH, B_WIDTH, C_WIDTH = 384, 320, 320
A_KERNEL, B_KERNEL = 31, 3
GMLP_BLOCK = 128
C_GROUPS = 5
X_HEADS = 4
X_HEAD_DIM = D_MODEL // X_HEADS
N_EXPERTS = 8

LANES = 128
SUBLANES = 8
MXU_DIM = 256
VMEM_LIMIT_BYTES = 56 * 1024 * 1024

SEG = 384
N_IN_SEG = 7
MIX_PAD = 3 * SEG
HALO_A = 32
HALO_B = SUBLANES
CONV_ROWS = 32
EDGE_PARTS = 2
SHIFT_EXTRA = HALO_A - SUBLANES

TILE_MIX = 512
TILE_ATT = 1024
TILE_FFN = 1024
TILE_ROUTE = 1024
TILE_DISPATCH = 1024
TILE_COMBINE = 256
TILE_EXPERT = 512
FF_CHUNK = 3584
FF_SUBCHUNK = 256
KV_ROWS = 1024
ROW_TILE = 8


def _rmsnorm(x, g):
    ms = jnp.mean(x * x, axis=-1, keepdims=True)
    return x * lax.rsqrt(ms + EPS) * g


def _const_spec(shape):
    zeros = (0,) * len(shape)
    return pl.BlockSpec(shape, lambda *_: zeros, pipeline_mode=pl.Buffered(1))


def _params(*semantics):
    return pltpu.CompilerParams(dimension_semantics=semantics,
                                vmem_limit_bytes=VMEM_LIMIT_BYTES)


def _mixer_kernel(h_ref, ng_ref, win_ref, caw_ref, cab_ref, lag_ref, lab_ref, cbw_ref,
                  lcg_ref, lcb_ref, ws_ref, gbias_ref, og_ref, wout_ref, gsum_ref,
                  o_ref, zbuf, abuf, bbuf, shifted, ybuf, *, ts, tiles_per_seq):
    seq_tile = lax.rem(pl.program_id(0), tiles_per_seq)

    @pl.when(seq_tile == 0)
    def _():
        abuf[0:HALO_A, :] = jnp.zeros((HALO_A, SEG), F32)
        bbuf[0:HALO_B, :] = jnp.zeros((HALO_B, SEG), F32)

    @pl.when(seq_tile > 0)
    def _():
        abuf[0:HALO_A, :] = abuf[ts:ts + HALO_A, :]
        bbuf[0:HALO_B, :] = bbuf[ts:ts + HALO_B, :]

    h = h_ref[...]
    xn = _rmsnorm(h, ng_ref[...]).astype(BF16)

    part = ts // EDGE_PARTS
    for r0 in range(0, ts, part):
        z_a = jnp.dot(xn[r0:r0 + part, :], win_ref[:, 0:2 * SEG], preferred_element_type=F32)
        abuf[HALO_A + r0:HALO_A + r0 + part, :] = z_a[:, 0:SEG] * jax.nn.sigmoid(z_a[:, SEG:2 * SEG])
        lo = 0 if r0 == 0 else r0 + SHIFT_EXTRA
        hi = r0 + part + SHIFT_EXTRA
        for phase in range(1, SUBLANES):
            shifted[phase - 1, lo:hi, :] = abuf[lo + phase:hi + phase, :]

    zbuf[...] = jnp.dot(xn, win_ref[:, 2 * SEG:N_IN_SEG * SEG], preferred_element_type=F32)

    for r in range(0, ts, CONV_ROWS):
        acc = jnp.concatenate([cab_ref[...]] * (CONV_ROWS // SUBLANES), axis=0)
        for k in range(A_KERNEL):
            groups, phase = divmod(HALO_A - A_KERNEL + 1 + k, SUBLANES)
            start = r + groups * SUBLANES
            if phase == 0:
                window = abuf[start:start + CONV_ROWS, :]
            else:
                window = shifted[phase - 1, start:start + CONV_ROWS, :]
            tap = caw_ref[k * SUBLANES:(k + 1) * SUBLANES, :]
            acc = acc + jnp.concatenate([tap] * (CONV_ROWS // SUBLANES), axis=0) * window
        mu = jnp.mean(acc, axis=-1, keepdims=True)
        xc = acc - mu
        var = jnp.mean(xc * xc, axis=-1, keepdims=True)
        ybuf[r:r + CONV_ROWS, 0:SEG] = jax.nn.silu(
            xc * lax.rsqrt(var + EPS) * lag_ref[...] + lab_ref[...])

    bbuf[HALO_B:HALO_B + ts, :] = zbuf[:, 1 * SEG:2 * SEG] * zbuf[:, 2 * SEG:3 * SEG]
    accb = jnp.zeros((ts, SEG), F32)
    for k in range(B_KERNEL):
        first = HALO_B - B_KERNEL + 1 + k
        accb = accb + cbw_ref[k:k + 1, :] * bbuf[first:first + ts, :]
    ybuf[:, 1 * SEG:2 * SEG] = zbuf[:, 0:SEG] * accb

    c_valid = lax.broadcasted_iota(jnp.int32, (1, SEG), 1) < C_WIDTH
    low_group = lax.broadcasted_iota(jnp.int32, (GMLP_BLOCK, LANES), 1) < HEAD_GROUP
    row_chunk = lax.broadcasted_iota(jnp.int32, (GMLP_BLOCK, GMLP_BLOCK), 0) // CHUNK
    col_chunk = lax.broadcasted_iota(jnp.int32, (GMLP_BLOCK, GMLP_BLOCK), 1) // CHUNK
    chunk_causal = row_chunk >= col_chunk
    ws = [jnp.where(chunk_causal, ws_ref[g], jnp.zeros((), BF16)) for g in range(C_GROUPS)]
    for r in range(0, ts, GMLP_BLOCK):
        rows = slice(r, r + GMLP_BLOCK)
        c_u = jax.nn.gelu(zbuf[rows, 3 * SEG:4 * SEG])
        c_v = jax.nn.gelu(zbuf[rows, 4 * SEG:5 * SEG])
        mu = jnp.sum(c_v, axis=-1, keepdims=True) * (1.0 / C_WIDTH)
        xc = jnp.where(c_valid, c_v - mu, 0.0)
        var = jnp.sum(xc * xc, axis=-1, keepdims=True) * (1.0 / C_WIDTH)
        v = (xc * lax.rsqrt(var + EPS) * lcg_ref[...] + lcb_ref[...]).astype(BF16)
        cols = []
        for j in range(SEG // LANES):
            vj = v[:, j * LANES:(j + 1) * LANES]
            col = jnp.dot(ws[2 * j], vj, preferred_element_type=F32)
            if 2 * j + 1 < C_GROUPS:
                col = jnp.where(low_group, col, jnp.dot(ws[2 * j + 1], vj, preferred_element_type=F32))
            cols.append(col)
        mixed = jnp.concatenate(cols, axis=1) + gbias_ref[...]
        ybuf[rows, 2 * SEG:3 * SEG] = c_u * mixed

    gsum = gsum_ref[...]
    for r0 in range(0, ts, part):
        y = ybuf[r0:r0 + part, :]
        y2 = (y * y).astype(BF16)
        sums = []
        for c0 in range(0, MIX_PAD, MXU_DIM):
            width = min(MXU_DIM, MIX_PAD - c0)
            sums.append(jnp.dot(y2[:, c0:c0 + width], gsum[0:width, 0:width],
                                preferred_element_type=F32))
        ms = jnp.concatenate(sums, axis=1)
        yn = (y * lax.rsqrt(ms + EPS) * og_ref[...]).astype(BF16)
        o_ref[r0:r0 + part, :] = h[r0:r0 + part, :] + jnp.dot(yn, wout_ref[...], preferred_element_type=F32)


def _mixer(h, p, seq_len):
    n_tok = h.shape[0]
    ts = TILE_MIX
    tiles_per_seq = seq_len // ts
    kern = functools.partial(_mixer_kernel, ts=ts, tiles_per_seq=tiles_per_seq)
    consts = [p["norm_g"], p["w_in"], p["conv_a_w"], p["conv_a_b"], p["ln_a_g"], p["ln_a_b"],
              p["conv_b_w"], p["ln_c_g"], p["ln_c_b"], p["gmlp_ws"], p["gmlp_bias"],
              p["out_g"], p["w_out"], p["gsum"]]
    return pl.pallas_call(
        kern,
        grid=(n_tok // ts,),
        in_specs=[pl.BlockSpec((ts, D_MODEL), lambda i: (i, 0))]
                 + [_const_spec(c.shape) for c in consts],
        out_specs=pl.BlockSpec((ts, D_MODEL), lambda i: (i, 0)),
        out_shape=jax.ShapeDtypeStruct((n_tok, D_MODEL), F32),
        scratch_shapes=[
            pltpu.VMEM((ts, (N_IN_SEG - 2) * SEG), F32),
            pltpu.VMEM((HALO_A + ts, SEG), F32),
            pltpu.VMEM((HALO_B + ts, SEG), F32),
            pltpu.VMEM((SUBLANES - 1, ts + SHIFT_EXTRA, SEG), F32),
            pltpu.VMEM((ts, MIX_PAD), F32),
        ],
        compiler_params=_params("arbitrary"),
        name="mixer",
    )(h, *consts)


def _kv_kernel(mem_ref, g_ref, wkv_ref, k_ref, v_ref):
    mn = _rmsnorm(mem_ref[...], g_ref[...]).astype(BF16)
    kv = jnp.dot(mn, wkv_ref[...], preferred_element_type=F32)
    k_ref[...] = kv[:, 0:D_MODEL].astype(BF16)
    v_ref[...] = kv[:, D_MODEL:2 * D_MODEL].astype(BF16)


def _kv_proj(mem2d, g, wkv):
    n = mem2d.shape[0]
    rows = min(KV_ROWS, n)
    return pl.pallas_call(
        _kv_kernel,
        grid=(n // rows,),
        in_specs=[pl.BlockSpec((rows, D_MODEL), lambda i: (i, 0)),
                  _const_spec(g.shape), _const_spec(wkv.shape)],
        out_specs=[pl.BlockSpec((rows, D_MODEL), lambda i: (i, 0))] * 2,
        out_shape=[jax.ShapeDtypeStruct((n, D_MODEL), BF16)] * 2,
        compiler_params=_params("arbitrary"),
        name="kv_proj",
    )(mem2d, g, wkv)


def _side_cast_specs(side, n_steps):
    rows = side.shape[0] // n_steps
    assert side.shape[0] % n_steps == 0 and rows % (2 * SUBLANES) == 0
    spec = pl.BlockSpec((rows, side.shape[1]), lambda i: (i, 0))
    return spec, spec, jax.ShapeDtypeStruct(side.shape, BF16)


def _xattn_kernel(h_ref, g_ref, wq_ref, k_ref, v_ref, wo_ref, side_ref, o_ref, side_out_ref):
    side_out_ref[...] = side_ref[...].astype(BF16)
    h = h_ref[...]
    xn = _rmsnorm(h, g_ref[...]).astype(BF16)
    q = (jnp.dot(xn, wq_ref[...], preferred_element_type=F32) * (X_HEAD_DIM ** -0.5)).astype(BF16)
    heads = []
    for hd in range(X_HEADS):
        cols = slice(hd * X_HEAD_DIM, (hd + 1) * X_HEAD_DIM)
        s = lax.dot_general(q[:, cols], k_ref[:, cols], (((1,), (1,)), ((), ())),
                            preferred_element_type=F32)
        e = jnp.exp(s - jnp.max(s, axis=-1, keepdims=True))
        pv = jnp.dot(e.astype(BF16), v_ref[:, cols], preferred_element_type=F32)
        heads.append(pv / jnp.sum(e, axis=-1, keepdims=True))
    o = jnp.concatenate(heads, axis=1).astype(BF16)
    o_ref[...] = h + jnp.dot(o, wo_ref[...], preferred_element_type=F32)


def _xattn(h, k, v, g, wq, wo, seq_len, mem_len, side):
    n_tok = h.shape[0]
    ts = TILE_ATT
    tiles_per_seq = seq_len // ts
    side_in, side_out, side_shape = _side_cast_specs(side, n_tok // ts)
    return pl.pallas_call(
        _xattn_kernel,
        grid=(n_tok // ts,),
        in_specs=[pl.BlockSpec((ts, D_MODEL), lambda i: (i, 0)),
                  _const_spec(g.shape), _const_spec(wq.shape),
                  pl.BlockSpec((mem_len, D_MODEL), lambda i: (i // tiles_per_seq, 0)),
                  pl.BlockSpec((mem_len, D_MODEL), lambda i: (i // tiles_per_seq, 0)),
                  _const_spec(wo.shape), side_in],
        out_specs=[pl.BlockSpec((ts, D_MODEL), lambda i: (i, 0)), side_out],
        out_shape=[jax.ShapeDtypeStruct((n_tok, D_MODEL), F32), side_shape],
        compiler_params=_params("arbitrary"),
        name="xattn",
    )(h, g, wq, k, v, wo, side)


def _ffn_chunks(d_ff):
    return [(c0, min(c0 + FF_SUBCHUNK, d_ff)) for c0 in range(0, d_ff, FF_SUBCHUNK)]


def _ffn_kernel(h_ref, g_ref, wg_ref, wu_ref, wd_ref, side_ref, o_ref, side_out_ref, *, chunks):
    side_out_ref[...] = side_ref[...].astype(BF16)
    h = h_ref[...]
    xn = _rmsnorm(h, g_ref[...]).astype(BF16)
    acc = h
    for c0, c1 in chunks:
        gate = jnp.dot(xn, wg_ref[:, c0:c1], preferred_element_type=F32)
        up = jnp.dot(xn, wu_ref[:, c0:c1], preferred_element_type=F32)
        act = (jax.nn.silu(gate) * up).astype(BF16)
        acc = acc + jnp.dot(act, wd_ref[c0:c1, :], preferred_element_type=F32)
    o_ref[...] = acc


def _ffn(h, g, wg, wu, wd, side):
    n_tok = h.shape[0]
    ts = TILE_FFN
    kern = functools.partial(_ffn_kernel, chunks=_ffn_chunks(wg.shape[1]))
    side_in, side_out, side_shape = _side_cast_specs(side, n_tok // ts)
    return pl.pallas_call(
        kern,
        grid=(n_tok // ts,),
        in_specs=[pl.BlockSpec((ts, D_MODEL), lambda i: (i, 0)),
                  _const_spec(g.shape), _const_spec(wg.shape),
                  _const_spec(wu.shape), _const_spec(wd.shape), side_in],
        out_specs=[pl.BlockSpec((ts, D_MODEL), lambda i: (i, 0)), side_out],
        out_shape=[jax.ShapeDtypeStruct((n_tok, D_MODEL), F32), side_shape],
        compiler_params=_params("arbitrary"),
        name="ffn",
    )(h, g, wg, wu, wd, side)


META_IDX, META_RANK, META_W = 0, 2, 4


ROUTE_ROWS = 2 * SUBLANES


def _router_kernel(h_ref, g_ref, wr_ref, tri_ref, meta_ref, fields_ref, cnt_ref, run_ref):
    @pl.when(pl.program_id(0) == 0)
    def _():
        run_ref[...] = jnp.zeros_like(run_ref)

    hn = _rmsnorm(h_ref[...], g_ref[...])
    hn_hi = hn.astype(BF16)
    hn_lo = (hn - hn_hi.astype(F32)).astype(BF16)
    both = jnp.dot(hn_hi, wr_ref[...], preferred_element_type=F32)
    logits = (both[:, 0:LANES] + both[:, LANES:2 * LANES]
              + jnp.dot(hn_lo, wr_ref[:, 0:LANES], preferred_element_type=F32))
    lt = logits.T[0:ROUTE_ROWS, :]
    row = lax.broadcasted_iota(jnp.int32, lt.shape, 0).astype(F32)
    neg_inf = jnp.float32(-jnp.inf)
    l1 = jnp.where(row < N_EXPERTS, lt, neg_inf)
    m1 = jnp.max(l1, axis=0, keepdims=True)
    i1 = jnp.min(jnp.where(l1 == m1, row, float(ROUTE_ROWS)), axis=0, keepdims=True)
    l2 = jnp.where(row == i1, neg_inf, l1)
    m2 = jnp.max(l2, axis=0, keepdims=True)
    i2 = jnp.min(jnp.where(l2 == m2, row, float(ROUTE_ROWS)), axis=0, keepdims=True)
    e2 = jnp.exp(m2 - m1)
    w1 = 1.0 / (1.0 + e2)
    w2 = e2 / (1.0 + e2)
    sel1 = row == i1
    sel2 = row == i2
    onehot = jnp.where(sel1 | sel2, 1.0, 0.0)
    before = jnp.dot(onehot.astype(BF16), tri_ref[...], preferred_element_type=F32)
    run = run_ref[:, 0:1]
    rank = before + run
    r1 = jnp.sum(jnp.where(sel1, rank, 0.0), axis=0, keepdims=True)
    r2 = jnp.sum(jnp.where(sel2, rank, 0.0), axis=0, keepdims=True)
    total = run + jnp.sum(onehot, axis=1, keepdims=True)
    run_ref[...] = jnp.broadcast_to(total, run_ref.shape)
    cnt_ref[...] = jnp.broadcast_to(total, cnt_ref.shape)
    zero = jnp.zeros_like(w1)
    fields = jnp.concatenate([i1, i2, r1, r2, w1, w2, zero, zero], axis=0)
    fields_ref[...] = fields
    padded = jnp.concatenate([fields, jnp.zeros((LANES - SUBLANES, fields.shape[1]), F32)], axis=0)
    meta_ref[...] = padded.T


def _router(h, g, wr_pad, tri):
    n_tok = h.shape[0]
    ts = TILE_ROUTE
    return pl.pallas_call(
        _router_kernel,
        grid=(n_tok // ts,),
        in_specs=[pl.BlockSpec((ts, D_MODEL), lambda i: (i, 0)),
                  _const_spec(g.shape), _const_spec(wr_pad.shape), _const_spec(tri.shape)],
        out_specs=[pl.BlockSpec((ts, LANES), lambda i: (i, 0)),
                   pl.BlockSpec((SUBLANES, ts), lambda i: (0, i)),
                   pl.BlockSpec((ROUTE_ROWS, LANES), lambda i: (0, 0))],
        out_shape=[jax.ShapeDtypeStruct((n_tok, LANES), F32),
                   jax.ShapeDtypeStruct((SUBLANES, n_tok), F32),
                   jax.ShapeDtypeStruct((ROUTE_ROWS, LANES), F32)],
        scratch_shapes=[pltpu.VMEM((ROUTE_ROWS, LANES), F32)],
        compiler_params=_params("arbitrary"),
        name="router",
    )(h, g, wr_pad, tri)


def _rows_to_tiles(x, dst):
    n = x.shape[0]
    for g in range(n // SUBLANES):
        for j in range(ROW_TILE):
            dst[pl.ds(g * SUBLANES * ROW_TILE + j, SUBLANES, stride=ROW_TILE), :] = (
                x[g * SUBLANES:(g + 1) * SUBLANES, j * LANES:(j + 1) * LANES])


def _tiles_to_rows(src, n):
    groups = []
    for g in range(n // SUBLANES):
        groups.append(jnp.concatenate(
            [src[pl.ds(g * SUBLANES * ROW_TILE + j, SUBLANES, stride=ROW_TILE), :]
             for j in range(ROW_TILE)], axis=1))
    return jnp.concatenate(groups, axis=0)


def _tile_rows(ref, row):
    return ref.at[pl.ds(pl.multiple_of(row * ROW_TILE, ROW_TILE), ROW_TILE), :]


def _wait_rows(buf, sem):
    pltpu.make_async_copy(buf, buf, sem).wait()


def _dispatch_kernel(pos0_ref, pos1_ref, fill_ref, h_ref, g_ref, xs_ref, hn_buf, zero_buf, sems,
                     fill_sem, *, ts, tm):
    step = pl.program_id(0)
    slot = lax.rem(step, 2)

    @pl.when(step == 0)
    def _():
        zero_buf[...] = jnp.zeros_like(zero_buf)
        for k in range(2 * N_EXPERTS):
            @pl.when(fill_ref[k] >= 0)
            def _():
                cp = pltpu.make_async_copy(
                    zero_buf, xs_ref.at[pl.ds(fill_ref[k] * (tm * ROW_TILE), tm * ROW_TILE), :],
                    fill_sem)
                cp.start()
                cp.wait()

    rows = hn_buf.at[slot]
    _rows_to_tiles(_rmsnorm(h_ref[...], g_ref[...]), rows)

    def issue(q, carry):
        for l in range(LANES):
            src = _tile_rows(rows, q * LANES + l)
            for k, table in enumerate((pos0_ref, pos1_ref)):
                pltpu.make_async_copy(src, _tile_rows(xs_ref, table[q, l]),
                                      sems.at[slot, k]).start(priority=k)
        return carry

    lax.fori_loop(0, ts // LANES, issue, 0)

    @pl.when(step > 0)
    def _():
        for k in range(2):
            _wait_rows(hn_buf.at[1 - slot], sems.at[1 - slot, k])

    @pl.when(step == pl.num_programs(0) - 1)
    def _():
        for k in range(2):
            _wait_rows(rows, sems.at[slot, k])


def _dispatch(h, g, pos_tiles, fill_tiles, n_rows, tm):
    n_tok = h.shape[0]
    n_steps = pos_tiles[0].shape[0]
    ts = n_tok // n_steps
    kern = functools.partial(_dispatch_kernel, ts=ts, tm=tm)
    pos_spec = pl.BlockSpec((None,) + pos_tiles[0].shape[1:], lambda i: (i, 0, 0),
                            memory_space=pltpu.SMEM)
    return pl.pallas_call(
        kern,
        grid=(n_steps,),
        in_specs=[pos_spec, pos_spec,
                  pl.BlockSpec(memory_space=pltpu.SMEM),
                  pl.BlockSpec((ts, D_MODEL), lambda i: (i, 0)),
                  _const_spec(g.shape)],
        out_specs=pl.BlockSpec(memory_space=pl.ANY),
        out_shape=jax.ShapeDtypeStruct((n_rows * ROW_TILE, LANES), F32),
        scratch_shapes=[pltpu.VMEM((2, ts * ROW_TILE, LANES), F32),
                        pltpu.VMEM((tm * ROW_TILE, LANES), F32),
                        pltpu.SemaphoreType.DMA((2, 2)),
                        pltpu.SemaphoreType.DMA(())],
        compiler_params=_params("arbitrary"),
        name="dispatch",
    )(*pos_tiles, fill_tiles, h, g)


def _expert_kernel(te_ref, na_ref, x_ref, wg_ref, wu_ref, wd_ref, o_ref, *acc, tm, n_f):
    del te_ref
    acc_ref = acc[0] if acc else None
    f = pl.program_id(1)
    active = pl.program_id(0) < na_ref[0]

    @pl.when(jnp.logical_and(jnp.logical_not(active), f == 0))
    def _():
        o_ref[...] = jnp.zeros_like(o_ref)

    def swiglu_chunk(first, last):
        x = _tiles_to_rows(x_ref, tm).astype(BF16)
        part = None if first else acc_ref[...]
        for c0 in range(0, FF_CHUNK, FF_SUBCHUNK):
            c1 = min(c0 + FF_SUBCHUNK, FF_CHUNK)
            gate = jnp.dot(x, wg_ref[:, c0:c1], preferred_element_type=F32)
            up = jnp.dot(x, wu_ref[:, c0:c1], preferred_element_type=F32)
            act = (jax.nn.silu(gate) * up).astype(BF16)
            down = jnp.dot(act, wd_ref[c0:c1, :], preferred_element_type=F32)
            part = down if part is None else part + down
        if last:
            _rows_to_tiles(part, o_ref)
        else:
            acc_ref[...] = part

    if n_f == 1:
        pl.when(active)(functools.partial(swiglu_chunk, True, True))
        return
    pl.when(jnp.logical_and(active, f == 0))(functools.partial(swiglu_chunk, True, False))
    pl.when(jnp.logical_and(active, f == n_f - 1))(functools.partial(swiglu_chunk, False, True))
    if n_f > 2:
        pl.when(jnp.logical_and(active, jnp.logical_and(f > 0, f < n_f - 1)))(
            functools.partial(swiglu_chunk, False, False))


def _experts(xs, wg, wu, wd, tile_expert, n_active, tm):
    d_exp = wg.shape[2]
    n_f = d_exp // FF_CHUNK
    n_tiles = xs.shape[0] // (tm * ROW_TILE)

    def x_tile(i, na):
        return jnp.minimum(i, na[0] - 1)

    def chunk(i, f, na):
        return jnp.where(i < na[0], f, n_f - 1)

    grid_spec = pltpu.PrefetchScalarGridSpec(
        num_scalar_prefetch=2,
        grid=(n_tiles, n_f),
        in_specs=[
            pl.BlockSpec((tm * ROW_TILE, LANES), lambda i, f, te, na: (x_tile(i, na), 0)),
            pl.BlockSpec((None, D_MODEL, FF_CHUNK), lambda i, f, te, na: (te[i], 0, chunk(i, f, na))),
            pl.BlockSpec((None, D_MODEL, FF_CHUNK), lambda i, f, te, na: (te[i], 0, chunk(i, f, na))),
            pl.BlockSpec((None, FF_CHUNK, D_MODEL), lambda i, f, te, na: (te[i], chunk(i, f, na), 0)),
        ],
        out_specs=pl.BlockSpec((tm * ROW_TILE, LANES), lambda i, f, te, na: (i, 0)),
        scratch_shapes=[pltpu.VMEM((tm, D_MODEL), F32)] if n_f > 1 else [],
    )
    return pl.pallas_call(
        functools.partial(_expert_kernel, tm=tm, n_f=n_f),
        grid_spec=grid_spec,
        out_shape=jax.ShapeDtypeStruct(xs.shape, F32),
        compiler_params=_params("arbitrary", "arbitrary"),
        name="experts",
    )(tile_expert, n_active, xs, wg, wu, wd)


def _combine_kernel(pos0_ref, pos1_ref, next_pos0_ref, next_pos1_ref, h_ref, meta_ref, g_ref, ys_ref,
                    o_ref, bufs, sems, *, ts):
    step = pl.program_id(0)
    slot = lax.rem(step, 2)

    def gather(tables, dst_slot):
        def issue(q, carry):
            for l in range(LANES):
                for k, table in enumerate(tables):
                    dst = _tile_rows(bufs.at[dst_slot, k], q * LANES + l)
                    pltpu.make_async_copy(_tile_rows(ys_ref, table[q, l]), dst,
                                          sems.at[dst_slot, k]).start(priority=1)
            return carry

        lax.fori_loop(0, ts // LANES, issue, 0)

    @pl.when(step == 0)
    def _():
        gather((pos0_ref, pos1_ref), slot)

    @pl.when(step + 1 < pl.num_programs(0))
    def _():
        gather((next_pos0_ref, next_pos1_ref), 1 - slot)

    for k in range(2):
        _wait_rows(bufs.at[slot, k], sems.at[slot, k])
    meta = meta_ref[...]
    w1 = meta[:, META_W:META_W + 1]
    w2 = meta[:, META_W + 1:META_W + 2]
    y = h_ref[...] + (w1 * _tiles_to_rows(bufs.at[slot, 0], ts)
                      + w2 * _tiles_to_rows(bufs.at[slot, 1], ts))
    o_ref[...] = _rmsnorm(y, g_ref[...])


def _combine(h, meta, g, ys, pos_tiles):
    n_tok = h.shape[0]
    n_steps = pos_tiles[0].shape[0]
    ts = n_tok // n_steps
    kern = functools.partial(_combine_kernel, ts=ts)
    pos_block = (None,) + pos_tiles[0].shape[1:]
    pos_spec = pl.BlockSpec(pos_block, lambda i: (i, 0, 0), memory_space=pltpu.SMEM)
    next_spec = pl.BlockSpec(pos_block, lambda i: (jnp.minimum(i + 1, n_steps - 1), 0, 0),
                             memory_space=pltpu.SMEM)
    return pl.pallas_call(
        kern,
        grid=(n_steps,),
        in_specs=[pos_spec, pos_spec, next_spec, next_spec,
                  pl.BlockSpec((ts, D_MODEL), lambda i: (i, 0)),
                  pl.BlockSpec((ts, LANES), lambda i: (i, 0)),
                  _const_spec(g.shape),
                  pl.BlockSpec(memory_space=pl.ANY)],
        out_specs=pl.BlockSpec((ts, D_MODEL), lambda i: (i, 0)),
        out_shape=jax.ShapeDtypeStruct((n_tok, D_MODEL), F32),
        scratch_shapes=[pltpu.VMEM((2, 2, ts * ROW_TILE, LANES), F32),
                        pltpu.SemaphoreType.DMA((2, 2))],
        compiler_params=_params("arbitrary"),
        name="combine",
    )(*pos_tiles, *pos_tiles, h, meta, g, ys)


def _pos_tiles(pos, ts):
    return [p.reshape(p.shape[0] // ts, ts // LANES, LANES) for p in pos]


def _moe(h, norm_g, wr_pad, tri, wg, wu, wd, final_g):
    n_tok = h.shape[0]
    tm = TILE_EXPERT
    meta, fields, cnt = _router(h, norm_g, wr_pad, tri)

    counts = cnt[:N_EXPERTS, 0].astype(jnp.int32)
    tiles = (counts + (tm - 1)) // tm
    tile_end = jnp.cumsum(tiles)
    tile_start = tile_end - tiles
    n_tiles = (2 * n_tok) // tm + N_EXPERTS
    n_rows = n_tiles * tm

    def slots(k):
        idx = fields[META_IDX + k].astype(jnp.int32)
        rank = fields[META_RANK + k].astype(jnp.int32)
        first_tile = jnp.sum(jnp.where(idx[:, None] == jnp.arange(N_EXPERTS)[None, :],
                                       tile_start[None, :], 0), axis=1)
        return first_tile * tm + rank

    pos = [slots(0), slots(1)]
    n_active = tile_end[-1:].astype(jnp.int32)
    tail_tiles = jnp.where(tiles > 0, tile_end - 1, -1)
    slack_tiles = n_active[0] + jnp.arange(N_EXPERTS)
    slack_tiles = jnp.where(slack_tiles < n_tiles, slack_tiles, -1)
    fill_tiles = jnp.concatenate([tail_tiles, slack_tiles]).astype(jnp.int32)
    t = jnp.minimum(jnp.arange(n_tiles, dtype=jnp.int32), n_active[0] - 1)
    tile_expert = jnp.sum(t[:, None] >= tile_end[None, :], axis=1).astype(jnp.int32)

    xs = _dispatch(h, norm_g, _pos_tiles(pos, TILE_DISPATCH), fill_tiles, n_rows, tm)
    ys = _experts(xs, wg, wu, wd, tile_expert, n_active, tm)
    return _combine(h, meta, final_g, ys, _pos_tiles(pos, TILE_COMBINE))


def _pad_last(a, width):
    return jnp.pad(a, [(0, 0)] * (a.ndim - 1) + [(0, width - a.shape[-1])])


def _row(a):
    return a.reshape(1, -1)


def _router_weight(w):
    w_hi = w.astype(BF16)
    w_lo = (w - w_hi.astype(F32)).astype(BF16)
    return jnp.concatenate([_pad_last(w_hi, LANES), _pad_last(w_lo, LANES)], axis=1)


def _mixer_params(norm_g, w_in, conv_a_w, conv_a_b, ln_a_g, ln_a_b, conv_b_w, ln_c_g, ln_c_b,
                  gmlp_ws, gmlp_b, mix_out_g, w_mix_out):
    widths = [A_WIDTH, A_WIDTH, B_WIDTH, B_WIDTH, B_WIDTH, C_WIDTH, C_WIDTH]
    bounds = [0]
    for w in widths:
        bounds.append(bounds[-1] + w)
    w_in_p = jnp.concatenate(
        [_pad_last(w_in[:, bounds[s]:bounds[s + 1]], SEG) for s in range(N_IN_SEG)], axis=1)
    out_bounds = [0, A_WIDTH, A_WIDTH + B_WIDTH, A_WIDTH + B_WIDTH + C_WIDTH]
    out_g = jnp.concatenate(
        [_pad_last(mix_out_g[out_bounds[s]:out_bounds[s + 1]], SEG) for s in range(3)])
    w_out_p = jnp.concatenate(
        [jnp.pad(w_mix_out[out_bounds[s]:out_bounds[s + 1]],
                 [(0, SEG - (out_bounds[s + 1] - out_bounds[s])), (0, 0)]) for s in range(3)], axis=0)
    gbias = _pad_last(jnp.repeat(gmlp_b.T, HEAD_GROUP, axis=1), SEG)
    group = jnp.arange(MXU_DIM) // HEAD_GROUP
    gsum = ((group[:, None] == group[None, :]) * (1.0 / HEAD_GROUP)).astype(BF16)
    return {
        "norm_g": _row(norm_g), "w_in": w_in_p.astype(BF16),
        "conv_a_w": jnp.repeat(conv_a_w, SUBLANES, axis=0),
        "conv_a_b": jnp.broadcast_to(conv_a_b, (SUBLANES, A_WIDTH)),
        "ln_a_g": _row(ln_a_g), "ln_a_b": _row(ln_a_b),
        "conv_b_w": _pad_last(conv_b_w, SEG),
        "ln_c_g": _row(_pad_last(ln_c_g, SEG)), "ln_c_b": _row(_pad_last(ln_c_b, SEG)),
        "gmlp_ws": gmlp_ws.astype(BF16), "gmlp_bias": gbias,
        "out_g": _row(out_g), "w_out": w_out_p.astype(BF16), "gsum": gsum,
    }


def kernel(x, mem, norm_mix_g, w_in, conv_a_w, conv_a_b, ln_a_g, ln_a_b, conv_b_w, ln_c_g, ln_c_b,
           gmlp_ws, gmlp_b, mix_out_g, w_mix_out, norm_x_g, norm_mem_g, w_xq, w_xkv, w_xo,
           norm_ffn_g, ffn_w_gate, ffn_w_up, ffn_w_down, moe_router, moe_w_gate, moe_w_up,
           moe_w_down, norm_final_g):
    bsz, seq_len, _ = x.shape
    mem_len = mem.shape[1]
    depth = w_in.shape[0]
    assert depth == 2 and ffn_w_gate.shape[0] == 1 and moe_router.shape[0] == 1
    assert seq_len % TILE_MIX == 0 and seq_len % TILE_ATT == 0

    h = x.reshape(bsz * seq_len, D_MODEL)
    mem2d = mem.reshape(bsz * mem_len, D_MODEL)
    tri = (jnp.arange(TILE_ROUTE)[:, None] < jnp.arange(TILE_ROUTE)[None, :]).astype(BF16)
    n_exp, _, d_exp = moe_w_gate.shape[1:]
    sides = [moe_w_gate[0].reshape(n_exp * D_MODEL, d_exp), moe_w_up[0].reshape(n_exp * D_MODEL, d_exp),
             moe_w_down[0].reshape(n_exp * d_exp, D_MODEL)]
    casted = []

    def attend(h, layer):
        k, v = _kv_proj(mem2d, _row(norm_mem_g[layer]), w_xkv[layer].astype(BF16))
        h, done = _xattn(h, k, v, _row(norm_x_g[layer]), w_xq[layer].astype(BF16),
                         w_xo[layer].astype(BF16), seq_len, mem_len, sides[len(casted)])
        casted.append(done)
        return h

    def mix(h, layer):
        mp = _mixer_params(norm_mix_g[layer], w_in[layer], conv_a_w[layer], conv_a_b[layer],
                           ln_a_g[layer], ln_a_b[layer], conv_b_w[layer], ln_c_g[layer],
                           ln_c_b[layer], gmlp_ws[layer], gmlp_b[layer], mix_out_g[layer],
                           w_mix_out[layer])
        return _mixer(h, mp, seq_len)

    h = attend(mix(h, 0), 0)
    h, done = _ffn(h, _row(norm_ffn_g[0]), ffn_w_gate[0].astype(BF16), ffn_w_up[0].astype(BF16),
                   ffn_w_down[0].astype(BF16), sides[len(casted)])
    casted.append(done)
    h = attend(mix(h, 1), 1)
    out = _moe(h, _row(norm_ffn_g[1]), _router_weight(moe_router[0]), tri,
               casted[0].reshape(n_exp, D_MODEL, d_exp), casted[1].reshape(n_exp, D_MODEL, d_exp),
               casted[2].reshape(n_exp, d_exp, D_MODEL), _row(norm_final_g))
    return out.reshape(bsz, seq_len, D_MODEL)
```
